```python
import jax, jax.numpy as jnp
from jax import lax
import numpy as np

D_MODEL = 1024
BATCH = 4
SEQ = 4096
DEPTH = 2

GLA_HEADS = 4
GLA_DK = 64
GLA_DV = 128
GLA_GATE_RANK = 16
GLA_GATE_TAU = 16.0
GLA_CHUNK = 64
DSA_HEADS = 8
DSA_HD = 64
IDX_HEADS = 4
IDX_HD = 64
TOPK_MAX = 256
Q_BLOCK = 128
ROPE_THETA = 500000.0
ROPE_FRAC_DIV = 4
SGU_CHUNK = 128
SGU_GROUPS = 8
SGU_WIDTH = D_MODEL
D_FF = 2816
CONV_W = 3
EPS = 1e-6
LN_EPS = 1e-5

A_WIDTH = GLA_HEADS * GLA_DV
B_WIDTH = DSA_HEADS * DSA_HD
IN_WIDTHS = (
    GLA_HEADS * GLA_DK,
    GLA_HEADS * GLA_DK,
    GLA_HEADS * GLA_DV,
    A_WIDTH,
    GLA_GATE_RANK,
    DSA_HEADS * DSA_HD,
    DSA_HD,
    DSA_HD,
    IDX_HEADS * IDX_HD,
    IDX_HD,
    IDX_HEADS,
)
IN_WIDTH = sum(IN_WIDTHS)
N_EVEN = (DEPTH + 1) // 2
N_ODD = DEPTH // 2

kernel_name = "hybrid_gla_dsa_sgu_convffn"


def rmsnorm(x, g):
    xf = x.astype(jnp.float32)
    y = xf * lax.rsqrt(jnp.mean(xf * xf, axis=-1, keepdims=True) + EPS)
    return (y * g.astype(jnp.float32)).astype(x.dtype)


def layernorm(x, g, b):
    xf = x.astype(jnp.float32)
    mu = jnp.mean(xf, axis=-1, keepdims=True)
    xc = xf - mu
    y = xc * lax.rsqrt(jnp.mean(xc * xc, axis=-1, keepdims=True) + LN_EPS)
    return (y * g.astype(jnp.float32) + b.astype(jnp.float32)).astype(x.dtype)


def split_cols(a, widths):
    out, start = [], 0
    for w in widths:
        out.append(a[..., start:start + w])
        start += w
    return out


def rope_partial(x, pos):
    d = x.shape[-1]
    rd = d // ROPE_FRAC_DIV
    half = rd // 2
    inv = jnp.power(ROPE_THETA, -(jnp.arange(half, dtype=jnp.float32) * 2.0 / rd))
    ang = pos[:, None] * inv[None, :]
    cos = jnp.cos(ang)[:, None, :].astype(x.dtype)
    sin = jnp.sin(ang)[:, None, :].astype(x.dtype)
    x1, x2, rest = x[..., :half], x[..., half:rd], x[..., rd:]
    return jnp.concatenate([x1 * cos - x2 * sin, x2 * cos + x1 * sin, rest], axis=-1)


def gla_chunked(q, k, v, log_a):
    Bn, T, H, dk = q.shape
    dv = v.shape[-1]
    C = GLA_CHUNK
    N = T // C

    def chunks(a):
        return a.astype(jnp.float32).reshape(Bn, N, C, H, a.shape[-1]).transpose(0, 3, 1, 2, 4)

    qf = chunks(q) * (dk ** -0.5)
    kf, vf, la = chunks(k), chunks(v), chunks(log_a)
    b = jnp.cumsum(la, axis=3)
    b_last = b[:, :, :, -1:]
    b_mid = b[:, :, :, C // 2:C // 2 + 1]
    att = jnp.einsum('bhnid,bhnjd->bhnij', qf * jnp.exp(b - b_mid), kf * jnp.exp(b_mid - b))
    att = att * jnp.tril(jnp.ones((C, C), jnp.float32))
    o_intra = jnp.einsum('bhnij,bhnjd->bhnid', att, vf)
    dS = jnp.einsum('bhncd,bhnce->bhnde', kf * jnp.exp(b_last - b), vf)
    decay = jnp.exp(b_last[:, :, :, 0])

    def step(S, inp):
        dec, ds = inp
        return dec[..., None] * S + ds, S

    S0 = jnp.zeros((Bn, H, dk, dv), jnp.float32)
    _, S_prev = lax.scan(step, S0, (jnp.moveaxis(decay, 2, 0), jnp.moveaxis(dS, 2, 0)))
    S_prev = jnp.moveaxis(S_prev, 0, 2)
    o_inter = jnp.einsum('bhncd,bhnde->bhnce', qf * jnp.exp(b), S_prev)
    o = o_intra + o_inter
    return o.transpose(0, 2, 3, 1, 4).reshape(Bn, T, H, dv)


def dsa_attention(q, k, v, iq, ik, iw, topk):
    Bn, T, Hq, d = q.shape
    di = ik.shape[-1]
    NB = T // Q_BLOCK
    key_pos = jnp.arange(T, dtype=jnp.int32)
    ikf = ik.astype(jnp.float32)

    def to_blocks(a):
        return jnp.moveaxis(a.reshape((Bn, NB, Q_BLOCK) + a.shape[2:]), 1, 0)

    q_pos = key_pos.reshape(NB, Q_BLOCK)

    def one_block(args):
        qb, iqb, iwb, qp = args
        s_idx = jnp.einsum('bqhd,bsd->bqhs', iqb.astype(jnp.float32), ikf) * (di ** -0.5)
        score = jnp.einsum('bqh,bqhs->bqs', iwb.astype(jnp.float32), jax.nn.relu(s_idx))
        causal = key_pos[None, :] <= qp[:, None]
        score = jnp.where(causal[None], score, -jnp.inf)
        _, idx = lax.top_k(score, topk)
        kg = jax.vmap(lambda kb, ib: kb[ib])(k, idx)
        vg = jax.vmap(lambda vb, ib: vb[ib])(v, idx)
        valid = idx <= qp[None, :, None]
        s = jnp.einsum('bqhd,bqkd->bqhk', qb.astype(jnp.float32), kg.astype(jnp.float32)) * (d ** -0.5)
        s = jnp.where(valid[:, :, None, :], s, -jnp.inf)
        p = jax.nn.softmax(s, axis=-1)
        o = jnp.einsum('bqhk,bqkd->bqhd', p, vg.astype(jnp.float32))
        return o.astype(q.dtype)

    ob = lax.map(one_block, (to_blocks(q), to_blocks(iq), to_blocks(iw), q_pos))
    return jnp.moveaxis(ob, 0, 1).reshape(Bn, T, Hq * d)


def hybrid_attn_layer(x, g, w_in, w_a2, b_a, head_g, w_o, topk):
    Bn, T, _ = x.shape
    pos = jnp.arange(T, dtype=jnp.float32)
    h = rmsnorm(x, g)
    proj = h @ w_in
    aq, ak, av, ar, alr, bq, bk, bv, iq, ik, iw = split_cols(proj, IN_WIDTHS)
    log_a = jax.nn.log_sigmoid((alr @ w_a2 + b_a).astype(jnp.float32)) / GLA_GATE_TAU
    oa = gla_chunked(aq.reshape(Bn, T, GLA_HEADS, GLA_DK),
                     ak.reshape(Bn, T, GLA_HEADS, GLA_DK),
                     av.reshape(Bn, T, GLA_HEADS, GLA_DV),
                     log_a.reshape(Bn, T, GLA_HEADS, GLA_DK))
    oa = rmsnorm(oa, head_g).reshape(Bn, T, A_WIDTH).astype(x.dtype) * jax.nn.silu(ar)
    bq = rope_partial(bq.reshape(Bn, T, DSA_HEADS, DSA_HD), pos)
    bk = rope_partial(bk[:, :, None, :], pos)[:, :, 0, :]
    iq = rope_partial(iq.reshape(Bn, T, IDX_HEADS, IDX_HD), pos)
    ik = rope_partial(ik[:, :, None, :], pos)[:, :, 0, :]
    iw = iw * (IDX_HEADS ** -0.5)
    ob = dsa_attention(bq, bk, bv, iq, ik, iw, topk)
    return x + jnp.concatenate([oa, ob], axis=-1) @ w_o


def sgu_layer(x, g, w_uv, ln_g, ln_b, w_s, b_s, w_out):
    Bn, T, _ = x.shape
    NC = T // SGU_CHUNK
    gw = SGU_WIDTH // SGU_GROUPS
    h = rmsnorm(x, g)
    z = jax.nn.gelu(h @ w_uv)
    u, v = z[..., :SGU_WIDTH], z[..., SGU_WIDTH:]
    v = layernorm(v, ln_g, ln_b)
    vc = v.reshape(Bn, NC, SGU_CHUNK, SGU_GROUPS, gw)
    ws = w_s * jnp.tril(jnp.ones((SGU_CHUNK, SGU_CHUNK), w_s.dtype))
    mixed = jnp.einsum('gts,bnsgc->bntgc', ws, vc) + b_s.T[None, None, :, :, None]
    return x + (u * mixed.reshape(Bn, T, SGU_WIDTH)) @ w_out


def conv_ffn(x, g, w_up, conv_w, conv_b, w_down):
    T = x.shape[1]
    h = rmsnorm(x, g)
    a = h @ w_up
    ap = jnp.pad(a, ((0, 0), (CONV_W - 1, 0), (0, 0)))
    c = conv_b + sum(conv_w[j] * ap[:, j:j + T] for j in range(CONV_W))
    gate, up = c[..., :D_FF], c[..., D_FF:]
    return x + (jax.nn.silu(gate) * up) @ w_down


def setup_inputs(seed: int = 0) -> dict:
    key = jax.random.key(seed)
    ks = jax.random.split(key, 24)

    def nrm(k, shape, scale):
        return jax.random.normal(k, shape, jnp.float32) * scale

    D = D_MODEL
    return {
        "x": nrm(ks[0], (BATCH, SEQ, D), 1.0),
        "attn_norm": 1.0 + nrm(ks[1], (N_EVEN, D), 0.02),
        "attn_w_in": nrm(ks[2], (N_EVEN, D, IN_WIDTH), D ** -0.5),
        "gla_w_a2": nrm(ks[3], (N_EVEN, GLA_GATE_RANK, GLA_HEADS * GLA_DK), GLA_GATE_RANK ** -0.5),
        "gla_b_a": nrm(ks[4], (N_EVEN, GLA_HEADS * GLA_DK), 0.1),
        "gla_head_g": 1.0 + nrm(ks[5], (N_EVEN, GLA_DV), 0.02),
        "attn_w_o": nrm(ks[6], (N_EVEN, A_WIDTH + B_WIDTH, D), (A_WIDTH + B_WIDTH) ** -0.5),
        "sgu_norm": 1.0 + nrm(ks[7], (N_ODD, D), 0.02),
        "sgu_w_uv": nrm(ks[8], (N_ODD, D, 2 * SGU_WIDTH), D ** -0.5),
        "sgu_ln_g": 1.0 + nrm(ks[9], (N_ODD, SGU_WIDTH), 0.02),
        "sgu_ln_b": nrm(ks[10], (N_ODD, SGU_WIDTH), 0.02),
        "sgu_w_s": nrm(ks[11], (N_ODD, SGU_GROUPS, SGU_CHUNK, SGU_CHUNK), 0.5 * SGU_CHUNK ** -0.5),
        "sgu_b_s": 1.0 + nrm(ks[12], (N_ODD, SGU_GROUPS, SGU_CHUNK), 0.02),
        "sgu_w_out": nrm(ks[13], (N_ODD, SGU_WIDTH, D), SGU_WIDTH ** -0.5),
        "ffn_norm": 1.0 + nrm(ks[14], (DEPTH, D), 0.02),
        "ffn_w_up": nrm(ks[15], (DEPTH, D, 2 * D_FF), D ** -0.5),
        "ffn_conv_w": nrm(ks[16], (DEPTH, CONV_W, 2 * D_FF), CONV_W ** -0.5),
        "ffn_conv_b": nrm(ks[17], (DEPTH, 2 * D_FF), 0.02),
        "ffn_w_down": nrm(ks[18], (DEPTH, D_FF, D), D_FF ** -0.5),
        "final_norm": 1.0 + nrm(ks[19], (D,), 0.02),
    }


def reference(x, attn_norm, attn_w_in, gla_w_a2, gla_b_a, gla_head_g, attn_w_o,
              sgu_norm, sgu_w_uv, sgu_ln_g, sgu_ln_b, sgu_w_s, sgu_b_s, sgu_w_out,
              ffn_norm, ffn_w_up, ffn_conv_w, ffn_conv_b, ffn_w_down, final_norm):
    T = x.shape[1]
    topk = min(TOPK_MAX, T // 4)
    for i in range(DEPTH):
        j = i // 2
        if i % 2 == 0:
            x = hybrid_attn_layer(x, attn_norm[j], attn_w_in[j], gla_w_a2[j], gla_b_a[j],
                                  gla_head_g[j], attn_w_o[j], topk)
        else:
            x = sgu_layer(x, sgu_norm[j], sgu_w_uv[j], sgu_ln_g[j], sgu_ln_b[j],
                          sgu_w_s[j], sgu_b_s[j], sgu_w_out[j])
        x = conv_ffn(x, ffn_norm[i], ffn_w_up[i], ffn_conv_w[i], ffn_conv_b[i], ffn_w_down[i])
    return rmsnorm(x, final_norm)
```

```python
import functools

import jax
import jax.numpy as jnp
from jax import lax
from jax.experimental import pallas as pl
from jax.experimental.pallas import tpu as pltpu

D_MODEL = 1024
GLA_HEADS = 4
GLA_DK = 64
GLA_DV = 128
GLA_GATE_RANK = 16
GLA_GATE_TAU = 16.0
GLA_CHUNK = 64
DSA_HEADS = 8
DSA_HD = 64
IDX_HEADS = 4
IDX_HD = 64
TOPK_MAX = 256
ROPE_THETA = 500000.0
ROPE_FRAC_DIV = 4
SGU_CHUNK = 128
SGU_GROUPS = 8
D_FF = 2816
CONV_W = 3
EPS = 1e-6
LN_EPS = 1e-5

LANES = 128
SUBLANES = 8
VMEM_LIMIT = 56 * 1024 * 1024

_SEG_AQ = (0, 256)
_SEG_AK = (256, 512)
_SEG_AV = (512, 1024)
_SEG_AR = (1024, 1536)
_SEG_BQ = (1536, 2048)
_SEG_IQ = (2048, 2304)
_SEG_KV = (2304, 2432)
_SEG_MISC = (2432, 2560)
IN_PAD = 2560
MISC_ALR = 64
MISC_IW = 80

F32 = jnp.float32
BF16 = jnp.bfloat16
NEG_BIG = -1e30


def _dot(a, b):
    return jnp.dot(a, b, preferred_element_type=F32)


def _dot_nt(a, b):
    return lax.dot_general(a, b, (((1,), (1,)), ((), ())), preferred_element_type=F32)


def _dot_tn(a, b):
    return lax.dot_general(a, b, (((0,), (0,)), ((), ())), preferred_element_type=F32)


def _rmsnorm_rows(x, g):
    ms = jnp.mean(x * x, axis=-1, keepdims=True)
    return x * lax.rsqrt(ms + EPS) * g


def _params(*sem):
    return pltpu.CompilerParams(dimension_semantics=sem, vmem_limit_bytes=VMEM_LIMIT)


def _rope_slab(x, tab):
    half = DSA_HD // ROPE_FRAC_DIV // 2
    c = tab[:, 0:LANES]
    s_up = tab[:, LANES:2 * LANES]
    s_dn = tab[:, 2 * LANES:3 * LANES]
    return x * c + pltpu.roll(x, half, 1) * s_up + pltpu.roll(x, LANES - half, 1) * s_dn


def _in_proj_kernel(x_ref, g_ref, w_ref, wa2_ref, ba_ref, tab2_ref, tab1_ref,
                    aq_ref, ak_ref, av_ref, ar_ref, la_ref, bq_ref, iq_ref, kv_ref, misc_ref):
    h = _rmsnorm_rows(x_ref[...], g_ref[...]).astype(BF16)

    def seg(s):
        return _dot(h, w_ref[:, s[0]:s[1]])

    aq_ref[...] = (seg(_SEG_AQ) * (GLA_DK ** -0.5)).astype(aq_ref.dtype)
    ak_ref[...] = seg(_SEG_AK).astype(ak_ref.dtype)
    av_ref[...] = seg(_SEG_AV).astype(av_ref.dtype)
    ar_ref[...] = seg(_SEG_AR).astype(ar_ref.dtype)

    tab2 = tab2_ref[...]
    tab1 = tab1_ref[...]
    bq = seg(_SEG_BQ)
    for j in range((_SEG_BQ[1] - _SEG_BQ[0]) // LANES):
        sl = slice(j * LANES, (j + 1) * LANES)
        bq_ref[:, sl] = (_rope_slab(bq[:, sl], tab2) * (DSA_HD ** -0.5)).astype(bq_ref.dtype)
    iq = seg(_SEG_IQ)
    for j in range((_SEG_IQ[1] - _SEG_IQ[0]) // LANES):
        sl = slice(j * LANES, (j + 1) * LANES)
        iq_ref[:, sl] = (_rope_slab(iq[:, sl], tab2) * (IDX_HD ** -0.5)).astype(iq_ref.dtype)
    kv_ref[...] = _rope_slab(seg(_SEG_KV), tab1).astype(kv_ref.dtype)
    misc = seg(_SEG_MISC)
    misc_ref[...] = _rope_slab(misc, tab1)
    z = jnp.dot(misc, wa2_ref[...], preferred_element_type=F32,
                precision=lax.Precision.HIGHEST) + ba_ref[...]
    la_ref[...] = (jnp.minimum(z, 0.0) - jnp.log(1.0 + jnp.exp(-jnp.abs(z)))) * (1.0 / GLA_GATE_TAU)


def _rope_tables(T):
    rd = DSA_HD // ROPE_FRAC_DIV
    half = rd // 2
    pos = jnp.arange(T, dtype=F32)
    inv = jnp.power(ROPE_THETA, -(jnp.arange(half, dtype=F32) * 2.0 / rd))
    ang = pos[:, None] * inv[None, :]
    cos, sin = jnp.cos(ang), jnp.sin(ang)
    z = jnp.zeros((T, DSA_HD - rd), F32)
    zh = jnp.zeros((T, half), F32)
    c64 = jnp.concatenate([cos, cos, z + 1.0], axis=1)
    up64 = jnp.concatenate([zh, sin, z], axis=1)
    dn64 = jnp.concatenate([-sin, zh, z], axis=1)
    one64, zero64 = jnp.ones((T, DSA_HD), F32), jnp.zeros((T, DSA_HD), F32)
    tab2 = jnp.concatenate([c64, c64, up64, up64, dn64, dn64], axis=1)
    tab1 = jnp.concatenate([c64, one64, up64, zero64, dn64, zero64], axis=1)
    return tab2, tab1


def _in_proj(x2, g, w_in, w_a2, b_a, T, tm):
    N = x2.shape[0]
    widths = (256, 256, 512, 512, 16, 512, 64, 64, 256, 64, 4)
    offs = [0]
    for w in widths:
        offs.append(offs[-1] + w)
    aq, ak, av, ar, alr, bq, bk, bv, iq, ik, iw = [w_in[:, offs[i]:offs[i + 1]] for i in range(11)]
    pad = jnp.zeros((D_MODEL, LANES - IDX_HD - GLA_GATE_RANK - IDX_HEADS), w_in.dtype)
    wp = jnp.concatenate([aq, ak, av, ar, bq, iq, bk, bv, ik, alr, iw * (IDX_HEADS ** -0.5), pad],
                         axis=1).astype(BF16)
    wa2 = jnp.zeros((LANES, GLA_HEADS * GLA_DK), F32).at[MISC_ALR:MISC_ALR + GLA_GATE_RANK].set(w_a2)
    tab2, tab1 = _rope_tables(T)
    nt = T // tm
    row = lambda i: (i, 0)
    fixed = lambda i: (0, 0)
    tabm = lambda i: (i % nt, 0)
    outs = [(256, BF16), (256, BF16), (512, BF16), (512, BF16), (256, F32),
            (512, BF16), (256, BF16), (128, BF16), (128, F32)]
    return pl.pallas_call(
        _in_proj_kernel,
        grid=(N // tm,),
        in_specs=[
            pl.BlockSpec((tm, D_MODEL), row),
            pl.BlockSpec((1, D_MODEL), fixed),
            pl.BlockSpec((D_MODEL, IN_PAD), fixed),
            pl.BlockSpec((LANES, GLA_HEADS * GLA_DK), fixed),
            pl.BlockSpec((1, GLA_HEADS * GLA_DK), fixed),
            pl.BlockSpec((tm, 3 * LANES), tabm),
            pl.BlockSpec((tm, 3 * LANES), tabm),
        ],
        out_specs=[pl.BlockSpec((tm, w), row) for w, _ in outs],
        out_shape=[jax.ShapeDtypeStruct((N, w), d) for w, d in outs],
        compiler_params=_params("parallel"),
        name="in_proj",
    )(x2, g.reshape(1, -1), wp, wa2, b_a.reshape(1, -1), tab2, tab1)


def _gla_kernel(q_ref, k_ref, v_ref, la_ref, r_ref, hg_ref, o_ref, st_ref, *, n_chunks):
    C = GLA_CHUNK

    @pl.when(pl.program_id(1) == 0)
    def _():
        st_ref[...] = jnp.zeros_like(st_ref)

    ri = lax.broadcasted_iota(jnp.int32, (C, C), 0)
    ci = lax.broadcasted_iota(jnp.int32, (C, C), 1)
    tril = ri >= ci
    tri_f = jnp.where(tril, 1.0, 0.0).astype(F32)
    hg = hg_ref[...]

    def chunk(c, carry):
        rows = pl.ds(pl.multiple_of(c * C, C), C)
        la = la_ref[rows, :]
        b = jnp.dot(tri_f, la, preferred_element_type=F32, precision=lax.Precision.HIGHEST)
        b_mid = b[C // 2:C // 2 + 1, :]
        b_last = b[C - 1:C, :]
        q = q_ref[rows, :].astype(F32)
        k = k_ref[rows, :].astype(F32)
        qe = (q * jnp.exp(b - b_mid)).astype(BF16)
        ke = (k * jnp.exp(b_mid - b)).astype(BF16)
        kl = (k * jnp.exp(b_last - b)).astype(BF16)
        qb = (q * jnp.exp(b)).astype(BF16)
        dec = jnp.exp(b_last)
        for hh in range(GLA_HEADS):
            ks = slice(hh * GLA_DK, (hh + 1) * GLA_DK)
            vs = slice(hh * GLA_DV, (hh + 1) * GLA_DV)
            v = v_ref[rows, vs]
            att = jnp.where(tril, _dot_nt(qe[:, ks], ke[:, ks]), 0.0)
            st = st_ref[hh]
            o = _dot(att.astype(BF16), v) + _dot_nt(qb[:, ks], st.astype(BF16))
            st_ref[hh] = st * dec[:, ks] + _dot_tn(v, kl[:, ks])
            o = o * lax.rsqrt(jnp.mean(o * o, axis=-1, keepdims=True) + EPS) * hg
            r = r_ref[rows, vs].astype(F32)
            o_ref[rows, vs] = (o * (r * jax.nn.sigmoid(r))).astype(o_ref.dtype)
        return carry

    lax.fori_loop(0, n_chunks, chunk, 0)


def _gla(aq, ak, av, la, ar, head_g, B, T, tg):
    N = B * T
    nt = T // tg
    row = lambda b, i: (b * nt + i, 0)
    return pl.pallas_call(
        functools.partial(_gla_kernel, n_chunks=tg // GLA_CHUNK),
        grid=(B, nt),
        in_specs=[
            pl.BlockSpec((tg, 256), row), pl.BlockSpec((tg, 256), row), pl.BlockSpec((tg, 512), row),
            pl.BlockSpec((tg, 256), row), pl.BlockSpec((tg, 512), row),
            pl.BlockSpec((1, GLA_DV), lambda b, i: (0, 0)),
        ],
        out_specs=pl.BlockSpec((tg, 512), row),
        out_shape=jax.ShapeDtypeStruct((N, 512), BF16),
        scratch_shapes=[pltpu.VMEM((GLA_HEADS, GLA_DV, GLA_DK), F32)],
        compiler_params=_params("parallel", "arbitrary"),
        name="gla",
    )(aq, ak, av, la, ar, head_g.reshape(1, -1))


KEY_NEG_INF = -2139095041
KEY_POS_INF = 2139095040


def _key_to_f32(key):
    bits = jnp.where(key < 0, key ^ jnp.int32(0x7FFFFFFF), key)
    return lax.bitcast_convert_type(bits, F32)


def _dsa_kernel(bq_ref, iq_ref, qmisc_ref, kv_ref, kmisc_ref, o_ref,
                sc_ref, cut_ref, m_ref, l_ref, acc_ref, *, tq, tk, topk, cut_steps):
    qi = pl.program_id(1)
    q0 = qi * tq
    nkb = (q0 + tq + tk - 1) // tk
    qpos = q0 + lax.broadcasted_iota(jnp.int32, (tq, 1), 0)
    lane_iota = lax.broadcasted_iota(jnp.int32, (tq, tk), 1)

    iq = iq_ref[...]
    iw = [qmisc_ref[:, MISC_IW + h:MISC_IW + h + 1] for h in range(IDX_HEADS)]

    def score_block(kb, carry):
        krows = pl.ds(pl.multiple_of(kb * tk, tk), tk)
        ik = kmisc_ref[krows, 0:IDX_HD].astype(BF16)
        sc = jnp.zeros((tq, tk), F32)
        for h in range(IDX_HEADS):
            xh = _dot_nt(iq[:, h * IDX_HD:(h + 1) * IDX_HD], ik)
            sc = sc + iw[h] * jnp.maximum(xh, 0.0)
        sc = sc + 0.0
        kpos = kb * tk + lane_iota
        sc_ref[kb] = jnp.where(kpos <= qpos, sc, -jnp.inf)
        return carry

    lax.fori_loop(0, nkb, score_block, 0)

    def count_rows(ind):
        def body(kb, cnt):
            c = ind(sc_ref[kb], kb)
            part = c[:, 0:LANES]
            for j in range(1, tk // LANES):
                part = part + c[:, j * LANES:(j + 1) * LANES]
            return cnt + part
        cnt = lax.fori_loop(0, nkb, body, jnp.zeros((tq, LANES), jnp.int32))
        return jnp.sum(cnt, axis=1, keepdims=True)

    def bisect(_, lohi):
        lo, hi = lohi
        mid = (lo >> 1) + (hi >> 1) + (lo & hi & 1)
        cand = _key_to_f32(mid)
        ok = count_rows(lambda s, kb: jnp.where(s >= cand, 1, 0)) >= topk
        return jnp.where(ok, mid, lo), jnp.where(ok, hi, mid)

    lo0 = jnp.full((tq, 1), KEY_NEG_INF, jnp.int32)
    hi0 = jnp.full((tq, 1), KEY_POS_INF, jnp.int32)
    lo, _ = lax.fori_loop(0, 32, bisect, (lo0, hi0))
    thr = _key_to_f32(lo)
    n_gt = count_rows(lambda s, kb: jnp.where(s > thr, 1, 0))
    n_ge = count_rows(lambda s, kb: jnp.where(s >= thr, 1, 0))
    need = topk - n_gt
    tie = jnp.where(thr > -jnp.inf, jnp.where(n_ge > topk, 1, 0), 0)

    cut_ref[...] = qpos

    @pl.when(jnp.max(tie) > 0)
    def _():
        def cut_step(_, lohi):
            clo, chi = lohi
            mid = (clo + chi) >> 1
            ok = count_rows(lambda s, kb: jnp.where(
                s == thr, jnp.where(kb * tk + lane_iota <= mid, 1, 0), 0)) >= need
            return jnp.where(ok, clo, mid), jnp.where(ok, mid, chi)
        _, chi = lax.fori_loop(0, cut_steps, cut_step,
                               (jnp.full((tq, 1), -1, jnp.int32), qpos))
        cut_ref[...] = jnp.where(tie > 0, chi, qpos)

    cut = cut_ref[...]

    m_ref[...] = jnp.full(m_ref.shape, NEG_BIG, F32)
    l_ref[...] = jnp.zeros(l_ref.shape, F32)
    acc_ref[...] = jnp.zeros(acc_ref.shape, F32)
    bq = bq_ref[...]

    def attn_block(kb, carry):
        krows = pl.ds(pl.multiple_of(kb * tk, tk), tk)
        kvb = kv_ref[krows, :]
        kblk = kvb[:, 0:DSA_HD]
        vblk = kvb[:, DSA_HD:2 * DSA_HD]
        sc = sc_ref[kb]
        kpos = kb * tk + lane_iota
        bias = jnp.where(sc > thr, 0.0,
                         jnp.where(sc == thr, jnp.where(kpos <= cut, 0.0, NEG_BIG), NEG_BIG))
        for h in range(DSA_HEADS):
            hr = slice(h * tq, (h + 1) * tq)
            s = _dot_nt(bq[:, h * DSA_HD:(h + 1) * DSA_HD], kblk) + bias
            m_old = m_ref[hr, :]
            m_new = jnp.maximum(m_old, jnp.max(s, axis=1, keepdims=True))
            p = jnp.exp(s - m_new)
            alpha = jnp.exp(m_old - m_new)
            l_ref[hr, :] = alpha * l_ref[hr, :] + jnp.sum(p, axis=1, keepdims=True)
            acc_ref[hr, :] = alpha * acc_ref[hr, :] + _dot(p.astype(BF16), vblk)
            m_ref[hr, :] = m_new
        return carry

    lax.fori_loop(0, nkb, attn_block, 0)
    for h in range(DSA_HEADS):
        hr = slice(h * tq, (h + 1) * tq)
        o_ref[:, h * DSA_HD:(h + 1) * DSA_HD] = (acc_ref[hr, :] / l_ref[hr, :]).astype(o_ref.dtype)


def _dsa(bq, iq, misc, kv, B, T, tq, tk, topk):
    N = B * T
    nq = T // tq
    qrow = lambda b, i: (b * nq + i, 0)
    brow = lambda b, i: (b, 0)
    return pl.pallas_call(
        functools.partial(_dsa_kernel, tq=tq, tk=tk, topk=topk, cut_steps=T.bit_length()),
        grid=(B, nq),
        in_specs=[
            pl.BlockSpec((tq, 512), qrow), pl.BlockSpec((tq, 256), qrow), pl.BlockSpec((tq, LANES), qrow),
            pl.BlockSpec((T, LANES), brow), pl.BlockSpec((T, LANES), brow),
        ],
        out_specs=pl.BlockSpec((tq, 512), qrow),
        out_shape=jax.ShapeDtypeStruct((N, 512), BF16),
        scratch_shapes=[
            pltpu.VMEM((T // tk, tq, tk), F32),
            pltpu.VMEM((tq, 1), jnp.int32),
            pltpu.VMEM((DSA_HEADS * tq, 1), F32),
            pltpu.VMEM((DSA_HEADS * tq, 1), F32),
            pltpu.VMEM((DSA_HEADS * tq, DSA_HD), F32),
        ],
        compiler_params=_params("parallel", "arbitrary"),
        name="dsa",
    )(bq, iq, misc, kv, misc)


def _out_proj_kernel(x_ref, oa_ref, ob_ref, wa_ref, wb_ref, o_ref):
    o_ref[...] = x_ref[...] + _dot(oa_ref[...], wa_ref[...]) + _dot(ob_ref[...], wb_ref[...])


def _out_proj(x2, oa, ob, w_o, tm):
    N = x2.shape[0]
    wa = w_o[:512].astype(BF16)
    wb = w_o[512:].astype(BF16)
    row = lambda i: (i, 0)
    fixed = lambda i: (0, 0)
    return pl.pallas_call(
        _out_proj_kernel,
        grid=(N // tm,),
        in_specs=[pl.BlockSpec((tm, D_MODEL), row), pl.BlockSpec((tm, 512), row), pl.BlockSpec((tm, 512), row),
                  pl.BlockSpec((512, D_MODEL), fixed), pl.BlockSpec((512, D_MODEL), fixed)],
        out_specs=pl.BlockSpec((tm, D_MODEL), row),
        out_shape=jax.ShapeDtypeStruct((N, D_MODEL), F32),
        compiler_params=_params("parallel"),
        name="out_proj",
    )(x2, oa, ob, wa, wb)


def _ffn_kernel(x_ref, g_ref, wg_ref, wu_ref, cwg_ref, cwu_ref, cbg_ref, cbu_ref, wd_ref, fg_ref,
                o_ref, h_ref, ag_ref, au_ref, cg_ref, cu_ref, act_ref, acc_ref,
                *, tm, fc, tiles_per_seq, n_fc, final_norm, rb):
    i = pl.program_id(0)
    c = pl.program_id(1)
    H = SUBLANES

    @pl.when(c == 0)
    def _():
        h_ref[...] = _rmsnorm_rows(x_ref[...], g_ref[...]).astype(BF16)
        acc_ref[...] = jnp.zeros_like(acc_ref)

    first = (i % tiles_per_seq) == 0

    @pl.when(first)
    def _():
        ag_ref[0:H, :] = jnp.zeros((H, fc), F32)
        au_ref[0:H, :] = jnp.zeros((H, fc), F32)

    @pl.when(jnp.logical_not(first))
    def _():
        ag_ref[0:H, :] = cg_ref[c]
        au_ref[0:H, :] = cu_ref[c]

    hb = h_ref[...]
    ag_ref[H:H + tm, :] = _dot(hb, wg_ref[...])
    au_ref[H:H + tm, :] = _dot(hb, wu_ref[...])
    cg_ref[c] = ag_ref[tm:tm + H, :]
    cu_ref[c] = au_ref[tm:tm + H, :]

    cwg = cwg_ref[...]
    cwu = cwu_ref[...]
    cbg = cbg_ref[...]
    cbu = cbu_ref[...]

    def rows(r, carry):
        r0 = pl.multiple_of(r * rb, rb)
        wg = ag_ref[pl.ds(r0, rb + H), :]
        wu = au_ref[pl.ds(r0, rb + H), :]
        gate = cbg
        up = cbu
        for j in range(CONV_W):
            s0 = H - (CONV_W - 1) + j
            gate = gate + cwg[j:j + 1, :] * wg[s0:s0 + rb, :]
            up = up + cwu[j:j + 1, :] * wu[s0:s0 + rb, :]
        act_ref[pl.ds(r0, rb), :] = (gate * jax.nn.sigmoid(gate) * up).astype(BF16)
        return carry

    lax.fori_loop(0, tm // rb, rows, 0)
    acc_ref[...] += _dot(act_ref[...], wd_ref[...])

    @pl.when(c == n_fc - 1)
    def _():
        y = x_ref[...] + acc_ref[...]
        if final_norm:
            y = _rmsnorm_rows(y, fg_ref[...])
        o_ref[...] = y


def _conv_ffn(x2, g, w_up, conv_w, conv_b, w_down, final_g, T, tm, fc, final_norm):
    N = x2.shape[0]
    n_fc = D_FF // fc
    wup = w_up.astype(BF16)
    wdn = w_down.astype(BF16)
    cb = conv_b.reshape(1, -1)
    row = lambda i, c: (i, 0)
    fixed = lambda i, c: (0, 0)
    gcol = lambda i, c: (0, c)
    ucol = lambda i, c: (0, c + n_fc)
    return pl.pallas_call(
        functools.partial(_ffn_kernel, tm=tm, fc=fc, tiles_per_seq=T // tm, n_fc=n_fc,
                          final_norm=final_norm, rb=32),
        grid=(N // tm, n_fc),
        in_specs=[
            pl.BlockSpec((tm, D_MODEL), row),
            pl.BlockSpec((1, D_MODEL), fixed),
            pl.BlockSpec((D_MODEL, fc), gcol), pl.BlockSpec((D_MODEL, fc), ucol),
            pl.BlockSpec((CONV_W, fc), gcol), pl.BlockSpec((CONV_W, fc), ucol),
            pl.BlockSpec((1, fc), gcol), pl.BlockSpec((1, fc), ucol),
            pl.BlockSpec((fc, D_MODEL), lambda i, c: (c, 0)),
            pl.BlockSpec((1, D_MODEL), fixed),
        ],
        out_specs=pl.BlockSpec((tm, D_MODEL), row),
        out_shape=jax.ShapeDtypeStruct((N, D_MODEL), F32),
        scratch_shapes=[
            pltpu.VMEM((tm, D_MODEL), BF16),
            pltpu.VMEM((tm + SUBLANES, fc), F32), pltpu.VMEM((tm + SUBLANES, fc), F32),
            pltpu.VMEM((n_fc, SUBLANES, fc), F32), pltpu.VMEM((n_fc, SUBLANES, fc), F32),
            pltpu.VMEM((tm, fc), BF16),
            pltpu.VMEM((tm, D_MODEL), F32),
        ],
        compiler_params=_params("arbitrary", "arbitrary"),
        name="conv_ffn",
    )(x2, g.reshape(1, -1), wup, wup, conv_w, conv_w, cb, cb, wdn, final_g.reshape(1, -1))


def _sgu_kernel(x_ref, g_ref, wuv_ref, lng_ref, lnb_ref, ws_ref, bs_ref, wo_ref, o_ref,
                u_ref, v_ref, gated_ref, *, tm):
    W = D_MODEL
    gw = W // SGU_GROUPS
    h = _rmsnorm_rows(x_ref[...], g_ref[...]).astype(BF16)
    u_ref[...] = jax.nn.gelu(_dot(h, wuv_ref[:, 0:W]))
    v = jax.nn.gelu(_dot(h, wuv_ref[:, W:2 * W]))
    mu = jnp.mean(v, axis=-1, keepdims=True)
    vc = v - mu
    v = vc * lax.rsqrt(jnp.mean(vc * vc, axis=-1, keepdims=True) + LN_EPS) * lng_ref[...] + lnb_ref[...]
    v_ref[...] = v.astype(BF16)
    ri = lax.broadcasted_iota(jnp.int32, (SGU_CHUNK, SGU_CHUNK), 0)
    ci = lax.broadcasted_iota(jnp.int32, (SGU_CHUNK, SGU_CHUNK), 1)
    tril = ri >= ci
    for gi in range(SGU_GROUPS):
        cs = slice(gi * gw, (gi + 1) * gw)
        ws = jnp.where(tril, ws_ref[gi], 0.0).astype(BF16)
        bias = bs_ref[:, gi:gi + 1]
        for n in range(tm // SGU_CHUNK):
            rs = slice(n * SGU_CHUNK, (n + 1) * SGU_CHUNK)
            mixed = _dot(ws, v_ref[rs, cs]) + bias
            gated_ref[rs, cs] = (u_ref[rs, cs] * mixed).astype(BF16)
    o_ref[...] = x_ref[...] + _dot(gated_ref[...], wo_ref[...])


def _sgu(x2, g, w_uv, ln_g, ln_b, w_s, b_s, w_out, tm):
    N = x2.shape[0]
    row = lambda i: (i, 0)
    fixed = lambda i: (0, 0)
    return pl.pallas_call(
        functools.partial(_sgu_kernel, tm=tm),
        grid=(N // tm,),
        in_specs=[
            pl.BlockSpec((tm, D_MODEL), row),
            pl.BlockSpec((1, D_MODEL), fixed),
            pl.BlockSpec((D_MODEL, 2 * D_MODEL), fixed),
            pl.BlockSpec((1, D_MODEL), fixed), pl.BlockSpec((1, D_MODEL), fixed),
            pl.BlockSpec((SGU_GROUPS, SGU_CHUNK, SGU_CHUNK), lambda i: (0, 0, 0)),
            pl.BlockSpec((SGU_CHUNK, SGU_GROUPS), fixed),
            pl.BlockSpec((D_MODEL, D_MODEL), fixed),
        ],
        out_specs=pl.BlockSpec((tm, D_MODEL), row),
        out_shape=jax.ShapeDtypeStruct((N, D_MODEL), F32),
        scratch_shapes=[pltpu.VMEM((tm, D_MODEL), F32), pltpu.VMEM((tm, D_MODEL), BF16),
                        pltpu.VMEM((tm, D_MODEL), BF16)],
        compiler_params=_params("parallel"),
        name="sgu",
    )(x2, g.reshape(1, -1), w_uv.astype(BF16), ln_g.reshape(1, -1), ln_b.reshape(1, -1),
      w_s, b_s.T, w_out.astype(BF16))


def kernel(x, attn_norm, attn_w_in, gla_w_a2, gla_b_a, gla_head_g, attn_w_o, sgu_norm, sgu_w_uv, sgu_ln_g,
           sgu_ln_b, sgu_w_s, sgu_b_s, sgu_w_out, ffn_norm, ffn_w_up, ffn_conv_w, ffn_conv_b, ffn_w_down,
           final_norm):
    B, T, D = x.shape
    assert D == D_MODEL and T % 512 == 0
    topk = min(TOPK_MAX, T // 4)
    depth = ffn_norm.shape[0]
    x2 = x.reshape(B * T, D)
    for i in range(depth):
        j = i // 2
        if i % 2 == 0:
            aq, ak, av, ar, la, bq, iq, kv, misc = _in_proj(
                x2, attn_norm[j], attn_w_in[j], gla_w_a2[j], gla_b_a[j], T, 512)
            oa = _gla(aq, ak, av, la, ar, gla_head_g[j], B, T, 512)
            ob = _dsa(bq, iq, misc, kv, B, T, 128, 512, topk)
            x2 = _out_proj(x2, oa, ob, attn_w_o[j], 512)
        else:
            x2 = _sgu(x2, sgu_norm[j], sgu_w_uv[j], sgu_ln_g[j], sgu_ln_b[j], sgu_w_s[j], sgu_b_s[j],
                      sgu_w_out[j], 512)
        x2 = _conv_ffn(x2, ffn_norm[i], ffn_w_up[i], ffn_conv_w[i], ffn_conv_b[i], ffn_w_down[i],
                       final_norm, T, 512, 1408, i == depth - 1)
    return x2.reshape(B, T, D)
```

```python
import functools

import jax
import jax.numpy as jnp
from jax import lax
from jax.experimental import pallas as pl
from jax.experimental.pallas import tpu as pltpu

D_MODEL = 1024
GLA_HEADS = 4
GLA_DK = 64
GLA_DV = 128
GLA_GATE_RANK = 16
GLA_GATE_TAU = 16.0
GLA_CHUNK = 64
DSA_HEADS = 8
DSA_HD = 64
IDX_HEADS = 4
IDX_HD = 64
TOPK_MAX = 256
ROPE_THETA = 500000.0
ROPE_FRAC_DIV = 4
SGU_CHUNK = 128
SGU_GROUPS = 8
D_FF = 2816
CONV_W = 3
EPS = 1e-6
LN_EPS = 1e-5

LANES = 128
SUBLANES = 8
VMEM_LIMIT = 56 * 1024 * 1024

_SEG_AQ = (0, 256)
_SEG_AK = (256, 512)
_SEG_AV = (512, 1024)
_SEG_AR = (1024, 1536)
_SEG_BQ = (1536, 2048)
_SEG_IQ = (2048, 2304)
_SEG_KV = (2304, 2432)
_SEG_MISC = (2432, 2560)
IN_PAD = 2560
MISC_ALR = 64
MISC_IW = 80

F32 = jnp.float32
BF16 = jnp.bfloat16
NEG_BIG = -1e30
LOG2E = 1.4426950408889634


def _dot(a, b):
    return jnp.dot(a, b, preferred_element_type=F32)


def _dot_nt(a, b):
    return lax.dot_general(a, b, (((1,), (1,)), ((), ())), preferred_element_type=F32)


def _dot_tn(a, b):
    return lax.dot_general(a, b, (((0,), (0,)), ((), ())), preferred_element_type=F32)


def _rmsnorm_rows(x, g):
    ms = jnp.mean(x * x, axis=-1, keepdims=True)
    return x * lax.rsqrt(ms + EPS) * g


def _params(*sem):
    return pltpu.CompilerParams(dimension_semantics=sem, vmem_limit_bytes=VMEM_LIMIT)


def _rope_slab(x, tab):
    half = DSA_HD // ROPE_FRAC_DIV // 2
    c = tab[:, 0:LANES]
    s_up = tab[:, LANES:2 * LANES]
    s_dn = tab[:, 2 * LANES:3 * LANES]
    return x * c + pltpu.roll(x, half, 1) * s_up + pltpu.roll(x, LANES - half, 1) * s_dn


def _in_proj_kernel(x_ref, g_ref, w_ref, wa2_ref, ba_ref, tab2_ref, tab1_ref,
                    aq_ref, ak_ref, av_ref, ar_ref, la_ref, bq_ref, iq_ref, kv_ref, v1_ref, misc_ref):
    h = _rmsnorm_rows(x_ref[...], g_ref[...]).astype(BF16)

    def seg(s):
        return _dot(h, w_ref[:, s[0]:s[1]])

    aq_ref[...] = (seg(_SEG_AQ) * (GLA_DK ** -0.5)).astype(aq_ref.dtype)
    ak_ref[...] = seg(_SEG_AK).astype(ak_ref.dtype)
    av_ref[...] = seg(_SEG_AV).astype(av_ref.dtype)
    ar_ref[...] = seg(_SEG_AR).astype(ar_ref.dtype)

    tab2 = tab2_ref[...]
    tab1 = tab1_ref[...]
    bq = seg(_SEG_BQ)
    for j in range((_SEG_BQ[1] - _SEG_BQ[0]) // LANES):
        sl = slice(j * LANES, (j + 1) * LANES)
        bq_ref[:, sl] = (_rope_slab(bq[:, sl], tab2) * (DSA_HD ** -0.5 * LOG2E)).astype(bq_ref.dtype)
    iq = seg(_SEG_IQ)
    for j in range((_SEG_IQ[1] - _SEG_IQ[0]) // LANES):
        sl = slice(j * LANES, (j + 1) * LANES)
        iq_ref[:, sl] = (_rope_slab(iq[:, sl], tab2) * (IDX_HD ** -0.5)).astype(iq_ref.dtype)
    kv = _rope_slab(seg(_SEG_KV), tab1)
    kv_ref[...] = kv.astype(kv_ref.dtype)
    lane = lax.broadcasted_iota(jnp.int32, kv.shape, 1)
    v1 = jnp.where(lane < DSA_HD, pltpu.roll(kv, DSA_HD, 1), jnp.where(lane == DSA_HD, 1.0, 0.0))
    v1_ref[...] = v1.astype(v1_ref.dtype)
    misc = seg(_SEG_MISC)
    misc_ref[...] = _rope_slab(misc, tab1)
    z = jnp.dot(misc, wa2_ref[...], preferred_element_type=F32,
                precision=lax.Precision.HIGHEST) + ba_ref[...]
    la_ref[...] = (jnp.minimum(z, 0.0) - jnp.log(1.0 + jnp.exp(-jnp.abs(z)))) * (1.0 / GLA_GATE_TAU)


def _rope_tables(T):
    rd = DSA_HD // ROPE_FRAC_DIV
    half = rd // 2
    pos = jnp.arange(T, dtype=F32)
    inv = jnp.power(ROPE_THETA, -(jnp.arange(half, dtype=F32) * 2.0 / rd))
    ang = pos[:, None] * inv[None, :]
    cos, sin = jnp.cos(ang), jnp.sin(ang)
    z = jnp.zeros((T, DSA_HD - rd), F32)
    zh = jnp.zeros((T, half), F32)
    c64 = jnp.concatenate([cos, cos, z + 1.0], axis=1)
    up64 = jnp.concatenate([zh, sin, z], axis=1)
    dn64 = jnp.concatenate([-sin, zh, z], axis=1)
    one64, zero64 = jnp.ones((T, DSA_HD), F32), jnp.zeros((T, DSA_HD), F32)
    tab2 = jnp.concatenate([c64, c64, up64, up64, dn64, dn64], axis=1)
    tab1 = jnp.concatenate([c64, one64, up64, zero64, dn64, zero64], axis=1)
    return tab2, tab1


def _in_proj(x2, g, w_in, w_a2, b_a, T, tm):
    N = x2.shape[0]
    widths = (256, 256, 512, 512, 16, 512, 64, 64, 256, 64, 4)
    offs = [0]
    for w in widths:
        offs.append(offs[-1] + w)
    aq, ak, av, ar, alr, bq, bk, bv, iq, ik, iw = [w_in[:, offs[i]:offs[i + 1]] for i in range(11)]
    pad = jnp.zeros((D_MODEL, LANES - IDX_HD - GLA_GATE_RANK - IDX_HEADS), w_in.dtype)
    wp = jnp.concatenate([aq, ak, av, ar, bq, iq, bk, bv, ik, alr, iw * (IDX_HEADS ** -0.5), pad],
                         axis=1).astype(BF16)
    wa2 = jnp.zeros((LANES, GLA_HEADS * GLA_DK), F32).at[MISC_ALR:MISC_ALR + GLA_GATE_RANK].set(w_a2)
    tab2, tab1 = _rope_tables(T)
    nt = T // tm
    row = lambda i: (i, 0)
    fixed = lambda i: (0, 0)
    tabm = lambda i: (i % nt, 0)
    outs = [(256, BF16), (256, BF16), (512, BF16), (512, BF16), (256, F32),
            (512, BF16), (256, BF16), (128, BF16), (128, BF16), (128, F32)]
    return pl.pallas_call(
        _in_proj_kernel,
        grid=(N // tm,),
        in_specs=[
            pl.BlockSpec((tm, D_MODEL), row),
            pl.BlockSpec((1, D_MODEL), fixed),
            pl.BlockSpec((D_MODEL, IN_PAD), fixed),
            pl.BlockSpec((LANES, GLA_HEADS * GLA_DK), fixed),
            pl.BlockSpec((1, GLA_HEADS * GLA_DK), fixed),
            pl.BlockSpec((tm, 3 * LANES), tabm),
            pl.BlockSpec((tm, 3 * LANES), tabm),
        ],
        out_specs=[pl.BlockSpec((tm, w), row) for w, _ in outs],
        out_shape=[jax.ShapeDtypeStruct((N, w), d) for w, d in outs],
        compiler_params=_params("parallel"),
        name="in_proj",
    )(x2, g.reshape(1, -1), wp, wa2, b_a.reshape(1, -1), tab2, tab1)


def _gla_kernel(q_ref, k_ref, v_ref, la_ref, r_ref, hg_ref, o_ref, st_ref, *, n_chunks):
    C = GLA_CHUNK

    @pl.when(pl.program_id(1) == 0)
    def _():
        st_ref[...] = jnp.zeros_like(st_ref)

    ri = lax.broadcasted_iota(jnp.int32, (C, C), 0)
    ci = lax.broadcasted_iota(jnp.int32, (C, C), 1)
    tril = ri >= ci
    tri_f = jnp.where(tril, 1.0, 0.0).astype(F32)
    hg = hg_ref[...]

    def chunk(c, carry):
        rows = pl.ds(pl.multiple_of(c * C, C), C)
        la = la_ref[rows, :]
        b = jnp.dot(tri_f, la, preferred_element_type=F32, precision=lax.Precision.HIGHEST)
        b_mid = b[C // 2:C // 2 + 1, :]
        b_last = b[C - 1:C, :]
        q = q_ref[rows, :].astype(F32)
        k = k_ref[rows, :].astype(F32)
        qe = (q * jnp.exp(b - b_mid)).astype(BF16)
        ke = (k * jnp.exp(b_mid - b)).astype(BF16)
        kl = (k * jnp.exp(b_last - b)).astype(BF16)
        qb = (q * jnp.exp(b)).astype(BF16)
        dec = jnp.exp(b_last)
        for hh in range(GLA_HEADS):
            ks = slice(hh * GLA_DK, (hh + 1) * GLA_DK)
            vs = slice(hh * GLA_DV, (hh + 1) * GLA_DV)
            v = v_ref[rows, vs]
            att = jnp.where(tril, _dot_nt(qe[:, ks], ke[:, ks]), 0.0)
            st = st_ref[hh]
            o = _dot(att.astype(BF16), v) + _dot_nt(qb[:, ks], st.astype(BF16))
            st_ref[hh] = st * dec[:, ks] + _dot_tn(v, kl[:, ks])
            o = o * lax.rsqrt(jnp.mean(o * o, axis=-1, keepdims=True) + EPS) * hg
            r = r_ref[rows, vs].astype(F32)
            o_ref[rows, vs] = (o * (r * jax.nn.sigmoid(r))).astype(o_ref.dtype)
        return carry

    lax.fori_loop(0, n_chunks, chunk, 0)


def _gla(aq, ak, av, la, ar, head_g, B, T, tg):
    N = B * T
    nt = T // tg
    row = lambda b, i: (b * nt + i, 0)
    return pl.pallas_call(
        functools.partial(_gla_kernel, n_chunks=tg // GLA_CHUNK),
        grid=(B, nt),
        in_specs=[
            pl.BlockSpec((tg, 256), row), pl.BlockSpec((tg, 256), row), pl.BlockSpec((tg, 512), row),
            pl.BlockSpec((tg, 256), row), pl.BlockSpec((tg, 512), row),
            pl.BlockSpec((1, GLA_DV), lambda b, i: (0, 0)),
        ],
        out_specs=pl.BlockSpec((tg, 512), row),
        out_shape=jax.ShapeDtypeStruct((N, 512), BF16),
        scratch_shapes=[pltpu.VMEM((GLA_HEADS, GLA_DV, GLA_DK), F32)],
        compiler_params=_params("parallel", "arbitrary"),
        name="gla",
    )(aq, ak, av, la, ar, head_g.reshape(1, -1))


KEY_NEG_INF = -2139095041
KEY_POS_INF = 2139095040


def _key_to_f32(key):
    bits = jnp.where(key < 0, key ^ jnp.int32(0x7FFFFFFF), key)
    return lax.bitcast_convert_type(bits, F32)


def _dsa_kernel(bq_ref, iq_ref, qmisc_ref, kv_ref, v1_ref, kmisc_ref, tri_ref, o_ref,
                sc_ref, s_ref, qs_ref, mx_ref, acc_ref, *, tq, tk, topk):
    qi = pl.program_id(1)
    q0 = qi * tq
    nkb = (q0 + tq + tk - 1) // tk
    qpos = q0 + lax.broadcasted_iota(jnp.int32, (tq, 1), 0)
    lane_iota = lax.broadcasted_iota(jnp.int32, (tq, tk), 1)
    n_slab = tk // LANES

    def krows(kb):
        return pl.ds(pl.multiple_of(kb * tk, tk), tk)

    iq = iq_ref[...]
    iw = [qmisc_ref[:, MISC_IW + h:MISC_IW + h + 1] for h in range(IDX_HEADS)]

    def score_block(kb, carry):
        ik = kmisc_ref[krows(kb), 0:IDX_HD].astype(BF16)
        sc = jnp.zeros((tq, tk), F32)
        for h in range(IDX_HEADS):
            xh = _dot_nt(iq[:, h * IDX_HD:(h + 1) * IDX_HD], ik)
            sc = sc + iw[h] * jnp.maximum(xh, 0.0)
        sc = sc + 0.0
        kpos = kb * tk + lane_iota
        sc_ref[kb] = jnp.where(kpos <= qpos, sc, -jnp.inf)
        return carry

    lax.fori_loop(0, nkb, score_block, 0)

    def count_ge(cand):
        def body(kb, cnt):
            c = jnp.where(sc_ref[kb] >= cand, 1, 0)
            part = c[:, 0:LANES]
            for j in range(1, n_slab):
                part = part + c[:, j * LANES:(j + 1) * LANES]
            return cnt + part
        cnt = lax.fori_loop(0, nkb, body, jnp.zeros((tq, LANES), jnp.int32))
        return jnp.sum(cnt, axis=1, keepdims=True)

    def bisect(_, st):
        lo, hi, c_hi = st
        mid = (lo >> 1) + (hi >> 1) + (lo & hi & 1)
        c = count_ge(_key_to_f32(mid))
        ok = c >= topk
        return jnp.where(ok, mid, lo), jnp.where(ok, hi, mid), jnp.where(ok, c_hi, c)

    lo0 = jnp.full((tq, 1), KEY_NEG_INF, jnp.int32)
    hi0 = jnp.full((tq, 1), KEY_POS_INF, jnp.int32)
    lo, _, n_gt = lax.fori_loop(0, 32, bisect, (lo0, hi0, jnp.zeros((tq, 1), jnp.int32)))
    thr = _key_to_f32(lo)
    need = jnp.where(thr > -jnp.inf, topk - n_gt, 0).astype(F32)

    tri = tri_ref[...]

    def bias_block(kb, base):
        sc = sc_ref[kb]
        eq = jnp.where(sc == thr, 1.0, 0.0).astype(BF16)
        rank = _dot(eq, tri) + base
        sc_ref[kb] = jnp.where(sc > thr, 0.0,
                               jnp.where(sc == thr, jnp.where(rank <= need, 0.0, NEG_BIG), NEG_BIG))
        return rank[:, tk - 1:tk]

    lax.fori_loop(0, nkb, bias_block, jnp.zeros((tq, 1), F32))

    H = DSA_HEADS
    for h in range(H):
        qs_ref[h * tq:(h + 1) * tq, :] = bq_ref[:, h * DSA_HD:(h + 1) * DSA_HD]
    mx_ref[...] = jnp.full(mx_ref.shape, NEG_BIG, F32)

    def qk_block(kb, carry):
        s = _dot_nt(qs_ref[...], kv_ref[krows(kb), 0:DSA_HD])
        s = (s.reshape(H, tq, tk) + sc_ref[kb][None]).reshape(H * tq, tk)
        s_ref[kb] = s
        mx = mx_ref[...]
        for j in range(n_slab):
            mx = jnp.maximum(mx, s[:, j * LANES:(j + 1) * LANES])
        mx_ref[...] = mx
        return carry

    lax.fori_loop(0, nkb, qk_block, 0)
    mx_ref[...] = jnp.broadcast_to(jnp.max(mx_ref[...], axis=1, keepdims=True), mx_ref.shape)
    acc_ref[...] = jnp.zeros(acc_ref.shape, F32)

    def pv_block(kb, carry):
        m = mx_ref[...]
        p = jnp.concatenate(
            [jnp.exp2(s_ref[kb, :, j * LANES:(j + 1) * LANES] - m) for j in range(n_slab)], axis=1)
        acc_ref[...] += _dot(p.astype(BF16), v1_ref[krows(kb), :])
        return carry

    lax.fori_loop(0, nkb, pv_block, 0)
    for h in range(H):
        acc = acc_ref[h * tq:(h + 1) * tq, :]
        o_ref[:, h * DSA_HD:(h + 1) * DSA_HD] = (
            acc[:, 0:DSA_HD] / acc[:, DSA_HD:DSA_HD + 1]).astype(o_ref.dtype)


def _dsa(bq, iq, misc, kv, v1, B, T, tq, tk, topk):
    N = B * T
    nq = T // tq
    qrow = lambda b, i: (b * nq + i, 0)
    brow = lambda b, i: (b, 0)
    r = lax.broadcasted_iota(jnp.int32, (tk, tk), 0)
    c = lax.broadcasted_iota(jnp.int32, (tk, tk), 1)
    tri = jnp.where(r <= c, 1.0, 0.0).astype(BF16)
    return pl.pallas_call(
        functools.partial(_dsa_kernel, tq=tq, tk=tk, topk=topk),
        grid=(B, nq),
        in_specs=[
            pl.BlockSpec((tq, 512), qrow), pl.BlockSpec((tq, 256), qrow), pl.BlockSpec((tq, LANES), qrow),
            pl.BlockSpec((T, LANES), brow), pl.BlockSpec((T, LANES), brow), pl.BlockSpec((T, LANES), brow),
            pl.BlockSpec((tk, tk), lambda b, i: (0, 0)),
        ],
        out_specs=pl.BlockSpec((tq, 512), qrow),
        out_shape=jax.ShapeDtypeStruct((N, 512), BF16),
        scratch_shapes=[
            pltpu.VMEM((T // tk, tq, tk), F32),
            pltpu.VMEM((T // tk, DSA_HEADS * tq, tk), F32),
            pltpu.VMEM((DSA_HEADS * tq, DSA_HD), BF16),
            pltpu.VMEM((DSA_HEADS * tq, LANES), F32),
            pltpu.VMEM((DSA_HEADS * tq, LANES), F32),
        ],
        compiler_params=_params("parallel", "arbitrary"),
        name="dsa",
    )(bq, iq, misc, kv, v1, misc, tri)


def _out_proj_kernel(x_ref, oa_ref, ob_ref, wa_ref, wb_ref, o_ref):
    o_ref[...] = x_ref[...] + _dot(oa_ref[...], wa_ref[...]) + _dot(ob_ref[...], wb_ref[...])


def _out_proj(x2, oa, ob, w_o, tm):
    N = x2.shape[0]
    wa = w_o[:512].astype(BF16)
    wb = w_o[512:].astype(BF16)
    row = lambda i: (i, 0)
    fixed = lambda i: (0, 0)
    return pl.pallas_call(
        _out_proj_kernel,
        grid=(N // tm,),
        in_specs=[pl.BlockSpec((tm, D_MODEL), row), pl.BlockSpec((tm, 512), row), pl.BlockSpec((tm, 512), row),
                  pl.BlockSpec((512, D_MODEL), fixed), pl.BlockSpec((512, D_MODEL), fixed)],
        out_specs=pl.BlockSpec((tm, D_MODEL), row),
        out_shape=jax.ShapeDtypeStruct((N, D_MODEL), F32),
        compiler_params=_params("parallel"),
        name="out_proj",
    )(x2, oa, ob, wa, wb)


def _ffn_kernel(x_ref, g_ref, wg_ref, wu_ref, cwg_ref, cwu_ref, cbg_ref, cbu_ref, wd_ref, fg_ref,
                o_ref, h_ref, ag_ref, au_ref, cg_ref, cu_ref, act_ref, acc_ref,
                *, tm, fc, tiles_per_seq, n_fc, final_norm, rb):
    i = pl.program_id(0)
    c = pl.program_id(1)
    H = SUBLANES

    @pl.when(c == 0)
    def _():
        h_ref[...] = _rmsnorm_rows(x_ref[...], g_ref[...]).astype(BF16)
        acc_ref[...] = jnp.zeros_like(acc_ref)

    first = (i % tiles_per_seq) == 0

    @pl.when(first)
    def _():
        ag_ref[0:H, :] = jnp.zeros((H, fc), F32)
        au_ref[0:H, :] = jnp.zeros((H, fc), F32)

    @pl.when(jnp.logical_not(first))
    def _():
        ag_ref[0:H, :] = cg_ref[c]
        au_ref[0:H, :] = cu_ref[c]

    hb = h_ref[...]
    ag_ref[H:H + tm, :] = _dot(hb, wg_ref[...])
    au_ref[H:H + tm, :] = _dot(hb, wu_ref[...])
    cg_ref[c] = ag_ref[tm:tm + H, :]
    cu_ref[c] = au_ref[tm:tm + H, :]

    cwg = cwg_ref[...]
    cwu = cwu_ref[...]
    cbg = cbg_ref[...]
    cbu = cbu_ref[...]

    def rows(r, carry):
        r0 = pl.multiple_of(r * rb, rb)
        wg = ag_ref[pl.ds(r0, rb + H), :]
        wu = au_ref[pl.ds(r0, rb + H), :]
        gate = cbg
        up = cbu
        for j in range(CONV_W):
            s0 = H - (CONV_W - 1) + j
            gate = gate + cwg[j:j + 1, :] * wg[s0:s0 + rb, :]
            up = up + cwu[j:j + 1, :] * wu[s0:s0 + rb, :]
        act_ref[pl.ds(r0, rb), :] = (gate * jax.nn.sigmoid(gate) * up).astype(BF16)
        return carry

    lax.fori_loop(0, tm // rb, rows, 0)
    acc_ref[...] += _dot(act_ref[...], wd_ref[...])

    @pl.when(c == n_fc - 1)
    def _():
        y = x_ref[...] + acc_ref[...]
        if final_norm:
            y = _rmsnorm_rows(y, fg_ref[...])
        o_ref[...] = y


def _conv_ffn(x2, g, w_up, conv_w, conv_b, w_down, final_g, T, tm, fc, final_norm):
    N = x2.shape[0]
    n_fc = D_FF // fc
    wup = w_up.astype(BF16)
    wdn = w_down.astype(BF16)
    cb = conv_b.reshape(1, -1)
    row = lambda i, c: (i, 0)
    fixed = lambda i, c: (0, 0)
    gcol = lambda i, c: (0, c)
    ucol = lambda i, c: (0, c + n_fc)
    return pl.pallas_call(
        functools.partial(_ffn_kernel, tm=tm, fc=fc, tiles_per_seq=T // tm, n_fc=n_fc,
                          final_norm=final_norm, rb=32),
        grid=(N // tm, n_fc),
        in_specs=[
            pl.BlockSpec((tm, D_MODEL), row),
            pl.BlockSpec((1, D_MODEL), fixed),
            pl.BlockSpec((D_MODEL, fc), gcol), pl.BlockSpec((D_MODEL, fc), ucol),
            pl.BlockSpec((CONV_W, fc), gcol), pl.BlockSpec((CONV_W, fc), ucol),
            pl.BlockSpec((1, fc), gcol), pl.BlockSpec((1, fc), ucol),
            pl.BlockSpec((fc, D_MODEL), lambda i, c: (c, 0)),
            pl.BlockSpec((1, D_MODEL), fixed),
        ],
        out_specs=pl.BlockSpec((tm, D_MODEL), row),
        out_shape=jax.ShapeDtypeStruct((N, D_MODEL), F32),
        scratch_shapes=[
            pltpu.VMEM((tm, D_MODEL), BF16),
            pltpu.VMEM((tm + SUBLANES, fc), F32), pltpu.VMEM((tm + SUBLANES, fc), F32),
            pltpu.VMEM((n_fc, SUBLANES, fc), F32), pltpu.VMEM((n_fc, SUBLANES, fc), F32),
            pltpu.VMEM((tm, fc), BF16),
            pltpu.VMEM((tm, D_MODEL), F32),
        ],
        compiler_params=_params("arbitrary", "arbitrary"),
        name="conv_ffn",
    )(x2, g.reshape(1, -1), wup, wup, conv_w, conv_w, cb, cb, wdn, final_g.reshape(1, -1))


def _sgu_kernel(x_ref, g_ref, wuv_ref, lng_ref, lnb_ref, ws_ref, bs_ref, wo_ref, o_ref,
                u_ref, v_ref, gated_ref, *, tm):
    W = D_MODEL
    gw = W // SGU_GROUPS
    h = _rmsnorm_rows(x_ref[...], g_ref[...]).astype(BF16)
    u_ref[...] = jax.nn.gelu(_dot(h, wuv_ref[:, 0:W]))
    v = jax.nn.gelu(_dot(h, wuv_ref[:, W:2 * W]))
    mu = jnp.mean(v, axis=-1, keepdims=True)
    vc = v - mu
    v = vc * lax.rsqrt(jnp.mean(vc * vc, axis=-1, keepdims=True) + LN_EPS) * lng_ref[...] + lnb_ref[...]
    v_ref[...] = v.astype(BF16)
    ri = lax.broadcasted_iota(jnp.int32, (SGU_CHUNK, SGU_CHUNK), 0)
    ci = lax.broadcasted_iota(jnp.int32, (SGU_CHUNK, SGU_CHUNK), 1)
    tril = ri >= ci
    for gi in range(SGU_GROUPS):
        cs = slice(gi * gw, (gi + 1) * gw)
        ws = jnp.where(tril, ws_ref[gi], 0.0).astype(BF16)
        bias = bs_ref[:, gi:gi + 1]
        for n in range(tm // SGU_CHUNK):
            rs = slice(n * SGU_CHUNK, (n + 1) * SGU_CHUNK)
            mixed = _dot(ws, v_ref[rs, cs]) + bias
            gated_ref[rs, cs] = (u_ref[rs, cs] * mixed).astype(BF16)
    o_ref[...] = x_ref[...] + _dot(gated_ref[...], wo_ref[...])


def _sgu(x2, g, w_uv, ln_g, ln_b, w_s, b_s, w_out, tm):
    N = x2.shape[0]
    row = lambda i: (i, 0)
    fixed = lambda i: (0, 0)
    return pl.pallas_call(
        functools.partial(_sgu_kernel, tm=tm),
        grid=(N // tm,),
        in_specs=[
            pl.BlockSpec((tm, D_MODEL), row),
            pl.BlockSpec((1, D_MODEL), fixed),
            pl.BlockSpec((D_MODEL, 2 * D_MODEL), fixed),
            pl.BlockSpec((1, D_MODEL), fixed), pl.BlockSpec((1, D_MODEL), fixed),
            pl.BlockSpec((SGU_GROUPS, SGU_CHUNK, SGU_CHUNK), lambda i: (0, 0, 0)),
            pl.BlockSpec((SGU_CHUNK, SGU_GROUPS), fixed),
            pl.BlockSpec((D_MODEL, D_MODEL), fixed),
        ],
        out_specs=pl.BlockSpec((tm, D_MODEL), row),
        out_shape=jax.ShapeDtypeStruct((N, D_MODEL), F32),
        scratch_shapes=[pltpu.VMEM((tm, D_MODEL), F32), pltpu.VMEM((tm, D_MODEL), BF16),
                        pltpu.VMEM((tm, D_MODEL), BF16)],
        compiler_params=_params("parallel"),
        name="sgu",
    )(x2, g.reshape(1, -1), w_uv.astype(BF16), ln_g.reshape(1, -1), ln_b.reshape(1, -1),
      w_s, b_s.T, w_out.astype(BF16))


def kernel(x, attn_norm, attn_w_in, gla_w_a2, gla_b_a, gla_head_g, attn_w_o, sgu_norm, sgu_w_uv, sgu_ln_g,
           sgu_ln_b, sgu_w_s, sgu_b_s, sgu_w_out, ffn_norm, ffn_w_up, ffn_conv_w, ffn_conv_b, ffn_w_down,
           final_norm):
    B, T, D = x.shape
    assert D == D_MODEL and T % 512 == 0
    topk = min(TOPK_MAX, T // 4)
    depth = ffn_norm.shape[0]
    x2 = x.reshape(B * T, D)
    for i in range(depth):
        j = i // 2
        if i % 2 == 0:
            aq, ak, av, ar, la, bq, iq, kv, v1, misc = _in_proj(
                x2, attn_norm[j], attn_w_in[j], gla_w_a2[j], gla_b_a[j], T, 512)
            oa = _gla(aq, ak, av, la, ar, gla_head_g[j], B, T, 512)
            ob = _dsa(bq, iq, misc, kv, v1, B, T, 128, 512, topk)
            x2 = _out_proj(x2, oa, ob, attn_w_o[j], 512)
        else:
            x2 = _sgu(x2, sgu_norm[j], sgu_w_uv[j], sgu_ln_g[j], sgu_ln_b[j], sgu_w_s[j], sgu_b_s[j],
                      sgu_w_out[j], 512)
        x2 = _conv_ffn(x2, ffn_norm[i], ffn_w_up[i], ffn_conv_w[i], ffn_conv_b[i], ffn_w_down[i],
                       final_norm, T, 512, 1408, i == depth - 1)
    return x2.reshape(B, T, D)
```

```python
import functools

import jax
import jax.numpy as jnp
from jax import lax
from jax.experimental import pallas as pl
from jax.experimental.pallas import tpu as pltpu

D_MODEL = 1024
GLA_HEADS = 4
GLA_DK = 64
GLA_DV = 128
GLA_GATE_RANK = 16
GLA_GATE_TAU = 16.0
GLA_CHUNK = 64
DSA_HEADS = 8
DSA_HD = 64
IDX_HEADS = 4
IDX_HD = 64
TOPK_MAX = 256
ROPE_THETA = 500000.0
ROPE_FRAC_DIV = 4
SGU_CHUNK = 128
SGU_GROUPS = 8
D_FF = 2816
CONV_W = 3
EPS = 1e-6
LN_EPS = 1e-5

LANES = 128
SUBLANES = 8
VMEM_LIMIT = 56 * 1024 * 1024

_SEG_AQ = (0, 256)
_SEG_AK = (256, 512)
_SEG_AV = (512, 1024)
_SEG_AR = (1024, 1536)
_SEG_BQ = (1536, 2048)
_SEG_IQ = (2048, 2304)
_SEG_KK = (2304, 2432)
_SEG_MISC = (2432, 2560)
IN_PAD = 2560
MISC_ALR = 64
MISC_IW = 80

F32 = jnp.float32
BF16 = jnp.bfloat16
NEG_BIG = -1e30
LOG2E = 1.4426950408889634


def _dot(a, b):
    return jnp.dot(a, b, preferred_element_type=F32)


def _dot_nt(a, b):
    return lax.dot_general(a, b, (((1,), (1,)), ((), ())), preferred_element_type=F32)


def _dot_tn(a, b):
    return lax.dot_general(a, b, (((0,), (0,)), ((), ())), preferred_element_type=F32)


def _rmsnorm_rows(x, g):
    ms = jnp.mean(x * x, axis=-1, keepdims=True)
    return x * lax.rsqrt(ms + EPS) * g


def _params(*sem):
    return pltpu.CompilerParams(dimension_semantics=sem, vmem_limit_bytes=VMEM_LIMIT)


def _rope_slab(x, tab):
    half = DSA_HD // ROPE_FRAC_DIV // 2
    c = tab[:, 0:LANES]
    s_up = tab[:, LANES:2 * LANES]
    s_dn = tab[:, 2 * LANES:3 * LANES]
    return x * c + pltpu.roll(x, half, 1) * s_up + pltpu.roll(x, LANES - half, 1) * s_dn


def _in_proj_kernel(x_ref, g_ref, w_ref, wa2_ref, ba_ref, tab2_ref,
                    aq_ref, ak_ref, av_ref, ar_ref, la_ref, bq_ref, iq_ref, kk_ref, v1t_ref, misct_ref):
    h = _rmsnorm_rows(x_ref[...], g_ref[...]).astype(BF16)

    def seg(s):
        return _dot(h, w_ref[:, s[0]:s[1]])

    aq_ref[...] = (seg(_SEG_AQ) * (GLA_DK ** -0.5)).astype(aq_ref.dtype)
    ak_ref[...] = seg(_SEG_AK).astype(ak_ref.dtype)
    av_ref[...] = seg(_SEG_AV).astype(av_ref.dtype)
    ar_ref[...] = seg(_SEG_AR).astype(ar_ref.dtype)

    tab2 = tab2_ref[...]
    bq = seg(_SEG_BQ)
    for j in range((_SEG_BQ[1] - _SEG_BQ[0]) // LANES):
        sl = slice(j * LANES, (j + 1) * LANES)
        bq_ref[:, sl] = (_rope_slab(bq[:, sl], tab2) * (DSA_HD ** -0.5 * LOG2E)).astype(bq_ref.dtype)
    iq = seg(_SEG_IQ)
    for j in range((_SEG_IQ[1] - _SEG_IQ[0]) // LANES):
        sl = slice(j * LANES, (j + 1) * LANES)
        iq_ref[:, sl] = (_rope_slab(iq[:, sl], tab2) * (IDX_HD ** -0.5)).astype(iq_ref.dtype)
    kk_ref[...] = _rope_slab(seg(_SEG_KK), tab2).astype(kk_ref.dtype)
    misc = seg(_SEG_MISC)
    misct = misc.T
    misct_ref[...] = misct
    row = lax.broadcasted_iota(jnp.int32, misct.shape, 0)
    v1t = jnp.where(row < DSA_HD, misct, jnp.where(row == DSA_HD, 1.0, 0.0))
    v1t_ref[...] = v1t.astype(v1t_ref.dtype)
    z = jnp.dot(misc, wa2_ref[...], preferred_element_type=F32,
                precision=lax.Precision.HIGHEST) + ba_ref[...]
    la_ref[...] = (jnp.minimum(z, 0.0) - jnp.log(1.0 + jnp.exp(-jnp.abs(z)))) * (1.0 / GLA_GATE_TAU)


def _rope_tables(T):
    rd = DSA_HD // ROPE_FRAC_DIV
    half = rd // 2
    pos = jnp.arange(T, dtype=F32)
    inv = jnp.power(ROPE_THETA, -(jnp.arange(half, dtype=F32) * 2.0 / rd))
    ang = pos[:, None] * inv[None, :]
    cos, sin = jnp.cos(ang), jnp.sin(ang)
    z = jnp.zeros((T, DSA_HD - rd), F32)
    zh = jnp.zeros((T, half), F32)
    c64 = jnp.concatenate([cos, cos, z + 1.0], axis=1)
    up64 = jnp.concatenate([zh, sin, z], axis=1)
    dn64 = jnp.concatenate([-sin, zh, z], axis=1)
    return jnp.concatenate([c64, c64, up64, up64, dn64, dn64], axis=1)


def _in_proj(x2, g, w_in, w_a2, b_a, T, tm):
    N = x2.shape[0]
    widths = (256, 256, 512, 512, 16, 512, 64, 64, 256, 64, 4)
    offs = [0]
    for w in widths:
        offs.append(offs[-1] + w)
    aq, ak, av, ar, alr, bq, bk, bv, iq, ik, iw = [w_in[:, offs[i]:offs[i + 1]] for i in range(11)]
    pad = jnp.zeros((D_MODEL, LANES - IDX_HD - GLA_GATE_RANK - IDX_HEADS), w_in.dtype)
    wp = jnp.concatenate([aq, ak, av, ar, bq, iq, bk, ik, bv, alr, iw * (IDX_HEADS ** -0.5), pad],
                         axis=1).astype(BF16)
    wa2 = jnp.zeros((LANES, GLA_HEADS * GLA_DK), F32).at[MISC_ALR:MISC_ALR + GLA_GATE_RANK].set(w_a2)
    tab2 = _rope_tables(T)
    nt = T // tm
    row = lambda i: (i, 0)
    fixed = lambda i: (0, 0)
    tabm = lambda i: (i % nt, 0)
    outs = [(256, BF16), (256, BF16), (512, BF16), (512, BF16), (256, F32),
            (512, BF16), (256, BF16), (128, BF16)]
    col = lambda i: (0, i)
    return pl.pallas_call(
        _in_proj_kernel,
        grid=(N // tm,),
        in_specs=[
            pl.BlockSpec((tm, D_MODEL), row),
            pl.BlockSpec((1, D_MODEL), fixed),
            pl.BlockSpec((D_MODEL, IN_PAD), fixed),
            pl.BlockSpec((LANES, GLA_HEADS * GLA_DK), fixed),
            pl.BlockSpec((1, GLA_HEADS * GLA_DK), fixed),
            pl.BlockSpec((tm, 3 * LANES), tabm),
        ],
        out_specs=[pl.BlockSpec((tm, w), row) for w, _ in outs]
        + [pl.BlockSpec((LANES, tm), col), pl.BlockSpec((LANES, tm), col)],
        out_shape=[jax.ShapeDtypeStruct((N, w), d) for w, d in outs]
        + [jax.ShapeDtypeStruct((LANES, N), BF16), jax.ShapeDtypeStruct((LANES, N), F32)],
        compiler_params=_params("parallel"),
        name="in_proj",
    )(x2, g.reshape(1, -1), wp, wa2, b_a.reshape(1, -1), tab2)


def _gla_kernel(q_ref, k_ref, v_ref, la_ref, r_ref, hg_ref, o_ref, st_ref, *, n_chunks):
    C = GLA_CHUNK

    @pl.when(pl.program_id(1) == 0)
    def _():
        st_ref[...] = jnp.zeros_like(st_ref)

    ri = lax.broadcasted_iota(jnp.int32, (C, C), 0)
    ci = lax.broadcasted_iota(jnp.int32, (C, C), 1)
    tril = ri >= ci
    tri_f = jnp.where(tril, 1.0, 0.0).astype(F32)
    hg = hg_ref[...]

    def chunk(c, carry):
        rows = pl.ds(pl.multiple_of(c * C, C), C)
        la = la_ref[rows, :]
        b = jnp.dot(tri_f, la, preferred_element_type=F32, precision=lax.Precision.HIGHEST)
        b_mid = b[C // 2:C // 2 + 1, :]
        b_last = b[C - 1:C, :]
        q = q_ref[rows, :].astype(F32)
        k = k_ref[rows, :].astype(F32)
        qe = (q * jnp.exp(b - b_mid)).astype(BF16)
        ke = (k * jnp.exp(b_mid - b)).astype(BF16)
        kl = (k * jnp.exp(b_last - b)).astype(BF16)
        qb = (q * jnp.exp(b)).astype(BF16)
        dec = jnp.exp(b_last)
        for hh in range(GLA_HEADS):
            ks = slice(hh * GLA_DK, (hh + 1) * GLA_DK)
            vs = slice(hh * GLA_DV, (hh + 1) * GLA_DV)
            v = v_ref[rows, vs]
            att = jnp.where(tril, _dot_nt(qe[:, ks], ke[:, ks]), 0.0)
            st = st_ref[hh]
            o = _dot(att.astype(BF16), v) + _dot_nt(qb[:, ks], st.astype(BF16))
            st_ref[hh] = st * dec[:, ks] + _dot_tn(v, kl[:, ks])
            o = o * lax.rsqrt(jnp.mean(o * o, axis=-1, keepdims=True) + EPS) * hg
            r = r_ref[rows, vs].astype(F32)
            o_ref[rows, vs] = (o * (r * jax.nn.sigmoid(r))).astype(o_ref.dtype)
        return carry

    lax.fori_loop(0, n_chunks, chunk, 0)


def _gla(aq, ak, av, la, ar, head_g, B, T, tg):
    N = B * T
    nt = T // tg
    row = lambda b, i: (b * nt + i, 0)
    return pl.pallas_call(
        functools.partial(_gla_kernel, n_chunks=tg // GLA_CHUNK),
        grid=(B, nt),
        in_specs=[
            pl.BlockSpec((tg, 256), row), pl.BlockSpec((tg, 256), row), pl.BlockSpec((tg, 512), row),
            pl.BlockSpec((tg, 256), row), pl.BlockSpec((tg, 512), row),
            pl.BlockSpec((1, GLA_DV), lambda b, i: (0, 0)),
        ],
        out_specs=pl.BlockSpec((tg, 512), row),
        out_shape=jax.ShapeDtypeStruct((N, 512), BF16),
        scratch_shapes=[pltpu.VMEM((GLA_HEADS, GLA_DV, GLA_DK), F32)],
        compiler_params=_params("parallel", "arbitrary"),
        name="gla",
    )(aq, ak, av, la, ar, head_g.reshape(1, -1))


KEY_NEG_INF = -2139095041
KEY_POS_INF = 2139095040


def _key_to_f32(key):
    bits = jnp.where(key < 0, key ^ jnp.int32(0x7FFFFFFF), key)
    return lax.bitcast_convert_type(bits, F32)


def _dsa_kernel(bq_ref, iq_ref, qmisct_ref, kk_ref, v1t_ref, tri_ref, o_ref,
                sc_ref, s_ref, qs_ref, acc_ref, *, tq, tk, topk):
    qi = pl.program_id(1)
    q0 = qi * tq
    nkb = (q0 + tq + tk - 1) // tk
    qpos = q0 + lax.broadcasted_iota(jnp.int32, (1, tq), 1)
    key_iota = lax.broadcasted_iota(jnp.int32, (tk, tq), 0)
    H = DSA_HEADS
    S = SUBLANES

    def krows(kb):
        return pl.ds(pl.multiple_of(kb * tk, tk), tk)

    def fold_rows(x, op):
        parts = [x[j * S:(j + 1) * S, :] for j in range(x.shape[0] // S)]
        while len(parts) > 1:
            parts = [op(parts[j], parts[j + 1]) for j in range(0, len(parts) - 1, 2)] + (
                [parts[-1]] if len(parts) % 2 else [])
        return parts[0]

    iw = [qmisct_ref[MISC_IW + h:MISC_IW + h + 1, :] for h in range(IDX_HEADS)]

    def score_block(kb, carry):
        ik = kk_ref[krows(kb), IDX_HD:2 * IDX_HD]
        sc = jnp.zeros((tk, tq), F32)
        for h in range(IDX_HEADS):
            xh = _dot_nt(ik, iq_ref[:, h * IDX_HD:(h + 1) * IDX_HD])
            sc = sc + iw[h] * jnp.maximum(xh, 0.0)
        sc = sc + 0.0
        sc_ref[kb] = jnp.where(kb * tk + key_iota <= qpos, sc, -jnp.inf)
        return carry

    lax.fori_loop(0, nkb, score_block, 0)

    def count_ge(cand):
        def body(kb, cnt):
            return cnt + fold_rows(jnp.where(sc_ref[kb] >= cand, 1, 0), jnp.add)
        cnt = lax.fori_loop(0, nkb, body, jnp.zeros((S, tq), jnp.int32))
        return jnp.sum(cnt, axis=0, keepdims=True)

    def bisect(_, st):
        lo, hi, c_hi = st
        mid = (lo >> 1) + (hi >> 1) + (lo & hi & 1)
        c = count_ge(_key_to_f32(mid))
        ok = c >= topk
        return jnp.where(ok, mid, lo), jnp.where(ok, hi, mid), jnp.where(ok, c_hi, c)

    lo0 = jnp.full((1, tq), KEY_NEG_INF, jnp.int32)
    hi0 = jnp.full((1, tq), KEY_POS_INF, jnp.int32)
    lo, _, n_gt = lax.fori_loop(0, 32, bisect, (lo0, hi0, jnp.zeros((1, tq), jnp.int32)))
    thr = _key_to_f32(lo)
    need = jnp.where(thr > -jnp.inf, topk - n_gt, 0).astype(F32)

    tri = tri_ref[...]

    def bias_block(kb, base):
        sc = sc_ref[kb]
        eq = jnp.where(sc == thr, 1.0, 0.0).astype(BF16)
        rank = _dot(tri, eq) + base
        sc_ref[kb] = jnp.where(sc > thr, 0.0,
                               jnp.where(sc == thr, jnp.where(rank <= need, 0.0, NEG_BIG), NEG_BIG))
        return rank[tk - 1:tk, :]

    lax.fori_loop(0, nkb, bias_block, jnp.zeros((1, tq), F32))

    for h in range(H):
        qs_ref[h * tq:(h + 1) * tq, :] = bq_ref[:, h * DSA_HD:(h + 1) * DSA_HD]

    def qk_block(kb, mx):
        s = _dot_nt(kk_ref[krows(kb), 0:DSA_HD], qs_ref[...])
        s = s + jnp.concatenate([sc_ref[kb]] * H, axis=1)
        s_ref[kb] = s
        return jnp.maximum(mx, fold_rows(s, jnp.maximum))

    mx = lax.fori_loop(0, nkb, qk_block, jnp.full((S, H * tq), NEG_BIG, F32))
    m = jnp.max(mx, axis=0, keepdims=True)
    acc_ref[...] = jnp.zeros(acc_ref.shape, F32)

    def pv_block(kb, carry):
        p = jnp.exp2(s_ref[kb] - m).astype(BF16)
        acc_ref[...] += _dot(v1t_ref[:, krows(kb)], p)
        return carry

    lax.fori_loop(0, nkb, pv_block, 0)
    acc = acc_ref[...]
    out = acc[0:DSA_HD, :] / acc[DSA_HD:DSA_HD + 1, :]
    for h in range(H):
        o_ref[:, h * DSA_HD:(h + 1) * DSA_HD] = out[:, h * tq:(h + 1) * tq].T.astype(o_ref.dtype)


def _dsa(bq, iq, misct, kk, v1t, B, T, tq, tk, topk):
    N = B * T
    nq = T // tq
    qrow = lambda b, i: (b * nq + i, 0)
    r = lax.broadcasted_iota(jnp.int32, (tk, tk), 0)
    c = lax.broadcasted_iota(jnp.int32, (tk, tk), 1)
    tri = jnp.where(c <= r, 1.0, 0.0).astype(BF16)
    return pl.pallas_call(
        functools.partial(_dsa_kernel, tq=tq, tk=tk, topk=topk),
        grid=(B, nq),
        in_specs=[
            pl.BlockSpec((tq, 512), qrow), pl.BlockSpec((tq, 256), qrow),
            pl.BlockSpec((LANES, tq), lambda b, i: (0, b * nq + i)),
            pl.BlockSpec((T, LANES), lambda b, i: (b, 0)),
            pl.BlockSpec((LANES, T), lambda b, i: (0, b)),
            pl.BlockSpec((tk, tk), lambda b, i: (0, 0)),
        ],
        out_specs=pl.BlockSpec((tq, 512), qrow),
        out_shape=jax.ShapeDtypeStruct((N, 512), BF16),
        scratch_shapes=[
            pltpu.VMEM((T // tk, tk, tq), F32),
            pltpu.VMEM((T // tk, tk, DSA_HEADS * tq), F32),
            pltpu.VMEM((DSA_HEADS * tq, DSA_HD), BF16),
            pltpu.VMEM((LANES, DSA_HEADS * tq), F32),
        ],
        compiler_params=_params("parallel", "arbitrary"),
        name="dsa",
    )(bq, iq, misct, kk, v1t, tri)


def _out_proj_kernel(x_ref, oa_ref, ob_ref, wa_ref, wb_ref, o_ref):
    o_ref[...] = x_ref[...] + _dot(oa_ref[...], wa_ref[...]) + _dot(ob_ref[...], wb_ref[...])


def _out_proj(x2, oa, ob, w_o, tm):
    N = x2.shape[0]
    wa = w_o[:512].astype(BF16)
    wb = w_o[512:].astype(BF16)
    row = lambda i: (i, 0)
    fixed = lambda i: (0, 0)
    return pl.pallas_call(
        _out_proj_kernel,
        grid=(N // tm,),
        in_specs=[pl.BlockSpec((tm, D_MODEL), row), pl.BlockSpec((tm, 512), row), pl.BlockSpec((tm, 512), row),
                  pl.BlockSpec((512, D_MODEL), fixed), pl.BlockSpec((512, D_MODEL), fixed)],
        out_specs=pl.BlockSpec((tm, D_MODEL), row),
        out_shape=jax.ShapeDtypeStruct((N, D_MODEL), F32),
        compiler_params=_params("parallel"),
        name="out_proj",
    )(x2, oa, ob, wa, wb)


def _ffn_kernel(x_ref, g_ref, wg_ref, wu_ref, cwg_ref, cwu_ref, cbg_ref, cbu_ref, wd_ref, fg_ref,
                o_ref, h_ref, ag_ref, au_ref, cg_ref, cu_ref, act_ref, acc_ref,
                *, tm, fc, tiles_per_seq, n_fc, final_norm, rb):
    i = pl.program_id(0)
    c = pl.program_id(1)
    H = SUBLANES

    @pl.when(c == 0)
    def _():
        h_ref[...] = _rmsnorm_rows(x_ref[...], g_ref[...]).astype(BF16)
        acc_ref[...] = jnp.zeros_like(acc_ref)

    first = (i % tiles_per_seq) == 0

    @pl.when(first)
    def _():
        ag_ref[0:H, :] = jnp.zeros((H, fc), F32)
        au_ref[0:H, :] = jnp.zeros((H, fc), F32)

    @pl.when(jnp.logical_not(first))
    def _():
        ag_ref[0:H, :] = cg_ref[c]
        au_ref[0:H, :] = cu_ref[c]

    hb = h_ref[...]
    ag_ref[H:H + tm, :] = _dot(hb, wg_ref[...])
    au_ref[H:H + tm, :] = _dot(hb, wu_ref[...])
    cg_ref[c] = ag_ref[tm:tm + H, :]
    cu_ref[c] = au_ref[tm:tm + H, :]

    cwg = cwg_ref[...]
    cwu = cwu_ref[...]
    cbg = cbg_ref[...]
    cbu = cbu_ref[...]

    def rows(r, carry):
        r0 = pl.multiple_of(r * rb, rb)
        wg = ag_ref[pl.ds(r0, rb + H), :]
        wu = au_ref[pl.ds(r0, rb + H), :]
        gate = cbg
        up = cbu
        for j in range(CONV_W):
            s0 = H - (CONV_W - 1) + j
            gate = gate + cwg[j:j + 1, :] * wg[s0:s0 + rb, :]
            up = up + cwu[j:j + 1, :] * wu[s0:s0 + rb, :]
        act_ref[pl.ds(r0, rb), :] = (gate * jax.nn.sigmoid(gate) * up).astype(BF16)
        return carry

    lax.fori_loop(0, tm // rb, rows, 0)
    acc_ref[...] += _dot(act_ref[...], wd_ref[...])

    @pl.when(c == n_fc - 1)
    def _():
        y = x_ref[...] + acc_ref[...]
        if final_norm:
            y = _rmsnorm_rows(y, fg_ref[...])
        o_ref[...] = y


def _conv_ffn(x2, g, w_up, conv_w, conv_b, w_down, final_g, T, tm, fc, final_norm):
    N = x2.shape[0]
    n_fc = D_FF // fc
    wup = w_up.astype(BF16)
    wdn = w_down.astype(BF16)
    cb = conv_b.reshape(1, -1)
    row = lambda i, c: (i, 0)
    fixed = lambda i, c: (0, 0)
    gcol = lambda i, c: (0, c)
    ucol = lambda i, c: (0, c + n_fc)
    return pl.pallas_call(
        functools.partial(_ffn_kernel, tm=tm, fc=fc, tiles_per_seq=T // tm, n_fc=n_fc,
                          final_norm=final_norm, rb=32),
        grid=(N // tm, n_fc),
        in_specs=[
            pl.BlockSpec((tm, D_MODEL), row),
            pl.BlockSpec((1, D_MODEL), fixed),
            pl.BlockSpec((D_MODEL, fc), gcol), pl.BlockSpec((D_MODEL, fc), ucol),
            pl.BlockSpec((CONV_W, fc), gcol), pl.BlockSpec((CONV_W, fc), ucol),
            pl.BlockSpec((1, fc), gcol), pl.BlockSpec((1, fc), ucol),
            pl.BlockSpec((fc, D_MODEL), lambda i, c: (c, 0)),
            pl.BlockSpec((1, D_MODEL), fixed),
        ],
        out_specs=pl.BlockSpec((tm, D_MODEL), row),
        out_shape=jax.ShapeDtypeStruct((N, D_MODEL), F32),
        scratch_shapes=[
            pltpu.VMEM((tm, D_MODEL), BF16),
            pltpu.VMEM((tm + SUBLANES, fc), F32), pltpu.VMEM((tm + SUBLANES, fc), F32),
            pltpu.VMEM((n_fc, SUBLANES, fc), F32), pltpu.VMEM((n_fc, SUBLANES, fc), F32),
            pltpu.VMEM((tm, fc), BF16),
            pltpu.VMEM((tm, D_MODEL), F32),
        ],
        compiler_params=_params("arbitrary", "arbitrary"),
        name="conv_ffn",
    )(x2, g.reshape(1, -1), wup, wup, conv_w, conv_w, cb, cb, wdn, final_g.reshape(1, -1))


def _sgu_kernel(x_ref, g_ref, wuv_ref, lng_ref, lnb_ref, ws_ref, bs_ref, wo_ref, o_ref,
                u_ref, v_ref, gated_ref, *, tm):
    W = D_MODEL
    gw = W // SGU_GROUPS
    h = _rmsnorm_rows(x_ref[...], g_ref[...]).astype(BF16)
    u_ref[...] = jax.nn.gelu(_dot(h, wuv_ref[:, 0:W]))
    v = jax.nn.gelu(_dot(h, wuv_ref[:, W:2 * W]))
    mu = jnp.mean(v, axis=-1, keepdims=True)
    vc = v - mu
    v = vc * lax.rsqrt(jnp.mean(vc * vc, axis=-1, keepdims=True) + LN_EPS) * lng_ref[...] + lnb_ref[...]
    v_ref[...] = v.astype(BF16)
    ri = lax.broadcasted_iota(jnp.int32, (SGU_CHUNK, SGU_CHUNK), 0)
    ci = lax.broadcasted_iota(jnp.int32, (SGU_CHUNK, SGU_CHUNK), 1)
    tril = ri >= ci
    for gi in range(SGU_GROUPS):
        cs = slice(gi * gw, (gi + 1) * gw)
        ws = jnp.where(tril, ws_ref[gi], 0.0).astype(BF16)
        bias = bs_ref[:, gi:gi + 1]
        for n in range(tm // SGU_CHUNK):
            rs = slice(n * SGU_CHUNK, (n + 1) * SGU_CHUNK)
            mixed = _dot(ws, v_ref[rs, cs]) + bias
            gated_ref[rs, cs] = (u_ref[rs, cs] * mixed).astype(BF16)
    o_ref[...] = x_ref[...] + _dot(gated_ref[...], wo_ref[...])


def _sgu(x2, g, w_uv, ln_g, ln_b, w_s, b_s, w_out, tm):
    N = x2.shape[0]
    row = lambda i: (i, 0)
    fixed = lambda i: (0, 0)
    return pl.pallas_call(
        functools.partial(_sgu_kernel, tm=tm),
        grid=(N // tm,),
        in_specs=[
            pl.BlockSpec((tm, D_MODEL), row),
            pl.BlockSpec((1, D_MODEL), fixed),
            pl.BlockSpec((D_MODEL, 2 * D_MODEL), fixed),
            pl.BlockSpec((1, D_MODEL), fixed), pl.BlockSpec((1, D_MODEL), fixed),
            pl.BlockSpec((SGU_GROUPS, SGU_CHUNK, SGU_CHUNK), lambda i: (0, 0, 0)),
            pl.BlockSpec((SGU_CHUNK, SGU_GROUPS), fixed),
            pl.BlockSpec((D_MODEL, D_MODEL), fixed),
        ],
        out_specs=pl.BlockSpec((tm, D_MODEL), row),
        out_shape=jax.ShapeDtypeStruct((N, D_MODEL), F32),
        scratch_shapes=[pltpu.VMEM((tm, D_MODEL), F32), pltpu.VMEM((tm, D_MODEL), BF16),
                        pltpu.VMEM((tm, D_MODEL), BF16)],
        compiler_params=_params("parallel"),
        name="sgu",
    )(x2, g.reshape(1, -1), w_uv.astype(BF16), ln_g.reshape(1, -1), ln_b.reshape(1, -1),
      w_s, b_s.T, w_out.astype(BF16))


def kernel(x, attn_norm, attn_w_in, gla_w_a2, gla_b_a, gla_head_g, attn_w_o, sgu_norm, sgu_w_uv, sgu_ln_g,
           sgu_ln_b, sgu_w_s, sgu_b_s, sgu_w_out, ffn_norm, ffn_w_up, ffn_conv_w, ffn_conv_b, ffn_w_down,
           final_norm):
    B, T, D = x.shape
    assert D == D_MODEL and T % 512 == 0
    topk = min(TOPK_MAX, T // 4)
    depth = ffn_norm.shape[0]
    x2 = x.reshape(B * T, D)
    for i in range(depth):
        j = i // 2
        if i % 2 == 0:
            aq, ak, av, ar, la, bq, iq, kk, v1t, misct = _in_proj(
                x2, attn_norm[j], attn_w_in[j], gla_w_a2[j], gla_b_a[j], T, 512)
            oa = _gla(aq, ak, av, la, ar, gla_head_g[j], B, T, 512)
            ob = _dsa(bq, iq, misct, kk, v1t, B, T, 128, 512, topk)
            x2 = _out_proj(x2, oa, ob, attn_w_o[j], 512)
        else:
            x2 = _sgu(x2, sgu_norm[j], sgu_w_uv[j], sgu_ln_g[j], sgu_ln_b[j], sgu_w_s[j], sgu_b_s[j],
                      sgu_w_out[j], 512)
        x2 = _conv_ffn(x2, ffn_norm[i], ffn_w_up[i], ffn_conv_w[i], ffn_conv_b[i], ffn_w_down[i],
                       final_norm, T, 512, 1408, i == depth - 1)
    return x2.reshape(B, T, D)
```

```python
import functools

import jax
import jax.numpy as jnp
from jax import lax
from jax.experimental import pallas as pl
from jax.experimental.pallas import tpu as pltpu

D_MODEL = 1024
GLA_HEADS = 4
GLA_DK = 64
GLA_DV = 128
GLA_GATE_RANK = 16
GLA_GATE_TAU = 16.0
GLA_CHUNK = 64
DSA_HEADS = 8
DSA_HD = 64
IDX_HEADS = 4
IDX_HD = 64
TOPK_MAX = 256
ROPE_THETA = 500000.0
ROPE_FRAC_DIV = 4
SGU_CHUNK = 128
SGU_GROUPS = 8
D_FF = 2816
CONV_W = 3
EPS = 1e-6
LN_EPS = 1e-5

LANES = 128
SUBLANES = 8
VMEM_LIMIT = 56 * 1024 * 1024

_SEG_AQ = (0, 256)
_SEG_AK = (256, 512)
_SEG_AV = (512, 1024)
_SEG_AR = (1024, 1536)
_SEG_BQ = (1536, 2048)
_SEG_IQ = (2048, 2304)
_SEG_KK = (2304, 2432)
_SEG_MISC = (2432, 2560)
IN_PAD = 2560
MISC_ALR = 64
MISC_IW = 80

F32 = jnp.float32
BF16 = jnp.bfloat16
NEG_BIG = -1e30
LOG2E = 1.4426950408889634


def _dot(a, b):
    return jnp.dot(a, b, preferred_element_type=F32)


def _dot_nt(a, b):
    return lax.dot_general(a, b, (((1,), (1,)), ((), ())), preferred_element_type=F32)


def _dot_tn(a, b):
    return lax.dot_general(a, b, (((0,), (0,)), ((), ())), preferred_element_type=F32)


def _rmsnorm_rows(x, g):
    ms = jnp.mean(x * x, axis=-1, keepdims=True)
    return x * lax.rsqrt(ms + EPS) * g


def _params(*sem):
    return pltpu.CompilerParams(dimension_semantics=sem, vmem_limit_bytes=VMEM_LIMIT)


def _rope_slab(x, tab):
    half = DSA_HD // ROPE_FRAC_DIV // 2
    c = tab[:, 0:LANES]
    s_up = tab[:, LANES:2 * LANES]
    s_dn = tab[:, 2 * LANES:3 * LANES]
    return x * c + pltpu.roll(x, half, 1) * s_up + pltpu.roll(x, LANES - half, 1) * s_dn


def _in_proj_kernel(x_ref, g_ref, w_ref, wa2_ref, ba_ref, tab2_ref,
                    aq_ref, ak_ref, av_ref, ar_ref, la_ref, bq_ref, iq_ref, kk_ref, v1t_ref, misct_ref):
    h = _rmsnorm_rows(x_ref[...], g_ref[...]).astype(BF16)

    def seg(s):
        return _dot(h, w_ref[:, s[0]:s[1]])

    aq_ref[...] = (seg(_SEG_AQ) * (GLA_DK ** -0.5)).astype(aq_ref.dtype)
    ak_ref[...] = seg(_SEG_AK).astype(ak_ref.dtype)
    av_ref[...] = seg(_SEG_AV).astype(av_ref.dtype)
    ar_ref[...] = seg(_SEG_AR).astype(ar_ref.dtype)

    tab2 = tab2_ref[...]
    bq = seg(_SEG_BQ)
    for j in range((_SEG_BQ[1] - _SEG_BQ[0]) // LANES):
        sl = slice(j * LANES, (j + 1) * LANES)
        bq_ref[:, sl] = (_rope_slab(bq[:, sl], tab2) * (DSA_HD ** -0.5 * LOG2E)).astype(bq_ref.dtype)
    iq = seg(_SEG_IQ)
    for j in range((_SEG_IQ[1] - _SEG_IQ[0]) // LANES):
        sl = slice(j * LANES, (j + 1) * LANES)
        iq_ref[:, sl] = (_rope_slab(iq[:, sl], tab2) * (IDX_HD ** -0.5)).astype(iq_ref.dtype)
    kk_ref[...] = _rope_slab(seg(_SEG_KK), tab2).astype(kk_ref.dtype)
    misc = seg(_SEG_MISC)
    misct = misc.T
    misct_ref[...] = misct
    row = lax.broadcasted_iota(jnp.int32, misct.shape, 0)
    v1t = jnp.where(row < DSA_HD, misct, jnp.where(row == DSA_HD, 1.0, 0.0))
    v1t_ref[...] = v1t.astype(v1t_ref.dtype)
    z = jnp.dot(misc, wa2_ref[...], preferred_element_type=F32,
                precision=lax.Precision.HIGHEST) + ba_ref[...]
    la_ref[...] = (jnp.minimum(z, 0.0) - jnp.log(1.0 + jnp.exp(-jnp.abs(z)))) * (1.0 / GLA_GATE_TAU)


def _rope_tables(T):
    rd = DSA_HD // ROPE_FRAC_DIV
    half = rd // 2
    pos = jnp.arange(T, dtype=F32)
    inv = jnp.power(ROPE_THETA, -(jnp.arange(half, dtype=F32) * 2.0 / rd))
    ang = pos[:, None] * inv[None, :]
    cos, sin = jnp.cos(ang), jnp.sin(ang)
    z = jnp.zeros((T, DSA_HD - rd), F32)
    zh = jnp.zeros((T, half), F32)
    c64 = jnp.concatenate([cos, cos, z + 1.0], axis=1)
    up64 = jnp.concatenate([zh, sin, z], axis=1)
    dn64 = jnp.concatenate([-sin, zh, z], axis=1)
    return jnp.concatenate([c64, c64, up64, up64, dn64, dn64], axis=1)


def _in_proj(x2, g, w_in, w_a2, b_a, T, tm):
    N = x2.shape[0]
    widths = (256, 256, 512, 512, 16, 512, 64, 64, 256, 64, 4)
    offs = [0]
    for w in widths:
        offs.append(offs[-1] + w)
    aq, ak, av, ar, alr, bq, bk, bv, iq, ik, iw = [w_in[:, offs[i]:offs[i + 1]] for i in range(11)]
    pad = jnp.zeros((D_MODEL, LANES - IDX_HD - GLA_GATE_RANK - IDX_HEADS), w_in.dtype)
    wp = jnp.concatenate([aq, ak, av, ar, bq, iq, bk, ik, bv, alr, iw * (IDX_HEADS ** -0.5), pad],
                         axis=1).astype(BF16)
    wa2 = jnp.zeros((LANES, GLA_HEADS * GLA_DK), F32).at[MISC_ALR:MISC_ALR + GLA_GATE_RANK].set(w_a2)
    tab2 = _rope_tables(T)
    nt = T // tm
    row = lambda i: (i, 0)
    fixed = lambda i: (0, 0)
    tabm = lambda i: (i % nt, 0)
    outs = [(256, BF16), (256, BF16), (512, BF16), (512, BF16), (256, F32),
            (512, BF16), (256, BF16), (128, BF16)]
    col = lambda i: (0, i)
    return pl.pallas_call(
        _in_proj_kernel,
        grid=(N // tm,),
        in_specs=[
            pl.BlockSpec((tm, D_MODEL), row),
            pl.BlockSpec((1, D_MODEL), fixed),
            pl.BlockSpec((D_MODEL, IN_PAD), fixed),
            pl.BlockSpec((LANES, GLA_HEADS * GLA_DK), fixed),
            pl.BlockSpec((1, GLA_HEADS * GLA_DK), fixed),
            pl.BlockSpec((tm, 3 * LANES), tabm),
        ],
        out_specs=[pl.BlockSpec((tm, w), row) for w, _ in outs]
        + [pl.BlockSpec((LANES, tm), col), pl.BlockSpec((LANES, tm), col)],
        out_shape=[jax.ShapeDtypeStruct((N, w), d) for w, d in outs]
        + [jax.ShapeDtypeStruct((LANES, N), BF16), jax.ShapeDtypeStruct((LANES, N), F32)],
        compiler_params=_params("parallel"),
        name="in_proj",
    )(x2, g.reshape(1, -1), wp, wa2, b_a.reshape(1, -1), tab2)


def _gla_kernel(q_ref, k_ref, v_ref, la_ref, r_ref, hg_ref, o_ref, st_ref, *, n_chunks):
    C = GLA_CHUNK

    @pl.when(pl.program_id(1) == 0)
    def _():
        st_ref[...] = jnp.zeros_like(st_ref)

    ri = lax.broadcasted_iota(jnp.int32, (C, C), 0)
    ci = lax.broadcasted_iota(jnp.int32, (C, C), 1)
    tril = ri >= ci
    tri_f = jnp.where(tril, 1.0, 0.0).astype(F32)
    hg = hg_ref[...]

    def chunk(c, carry):
        rows = pl.ds(pl.multiple_of(c * C, C), C)
        la = la_ref[rows, :]
        b = jnp.dot(tri_f, la, preferred_element_type=F32, precision=lax.Precision.HIGHEST)
        b_mid = b[C // 2:C // 2 + 1, :]
        b_last = b[C - 1:C, :]
        q = q_ref[rows, :].astype(F32)
        k = k_ref[rows, :].astype(F32)
        qe = (q * jnp.exp(b - b_mid)).astype(BF16)
        ke = (k * jnp.exp(b_mid - b)).astype(BF16)
        kl = (k * jnp.exp(b_last - b)).astype(BF16)
        qb = (q * jnp.exp(b)).astype(BF16)
        dec = jnp.exp(b_last)
        for hh in range(GLA_HEADS):
            ks = slice(hh * GLA_DK, (hh + 1) * GLA_DK)
            vs = slice(hh * GLA_DV, (hh + 1) * GLA_DV)
            v = v_ref[rows, vs]
            att = jnp.where(tril, _dot_nt(qe[:, ks], ke[:, ks]), 0.0)
            st = st_ref[hh]
            o = _dot(att.astype(BF16), v) + _dot_nt(qb[:, ks], st.astype(BF16))
            st_ref[hh] = st * dec[:, ks] + _dot_tn(v, kl[:, ks])
            o = o * lax.rsqrt(jnp.mean(o * o, axis=-1, keepdims=True) + EPS) * hg
            r = r_ref[rows, vs].astype(F32)
            o_ref[rows, vs] = (o * (r * jax.nn.sigmoid(r))).astype(o_ref.dtype)
        return carry

    lax.fori_loop(0, n_chunks, chunk, 0)


def _gla(aq, ak, av, la, ar, head_g, B, T, tg):
    N = B * T
    nt = T // tg
    row = lambda b, i: (b * nt + i, 0)
    return pl.pallas_call(
        functools.partial(_gla_kernel, n_chunks=tg // GLA_CHUNK),
        grid=(B, nt),
        in_specs=[
            pl.BlockSpec((tg, 256), row), pl.BlockSpec((tg, 256), row), pl.BlockSpec((tg, 512), row),
            pl.BlockSpec((tg, 256), row), pl.BlockSpec((tg, 512), row),
            pl.BlockSpec((1, GLA_DV), lambda b, i: (0, 0)),
        ],
        out_specs=pl.BlockSpec((tg, 512), row),
        out_shape=jax.ShapeDtypeStruct((N, 512), BF16),
        scratch_shapes=[pltpu.VMEM((GLA_HEADS, GLA_DV, GLA_DK), F32)],
        compiler_params=_params("parallel", "arbitrary"),
        name="gla",
    )(aq, ak, av, la, ar, head_g.reshape(1, -1))


KEY_NEG_INF = -2139095041
KEY_POS_INF = 2139095040


SEARCH_ROUND = 3
SEARCH_WARMUP_ROUNDS = 4
SEARCH_INTERP_ROUNDS = 12
SEARCH_BISECT_ROUNDS = 11


def _f32_to_key(v):
    bits = lax.bitcast_convert_type(v, jnp.int32)
    return bits ^ ((bits >> 31) & jnp.int32(0x7FFFFFFF))


def _key_to_f32(key):
    bits = jnp.where(key < 0, key ^ jnp.int32(0x7FFFFFFF), key)
    return lax.bitcast_convert_type(bits, F32)


def _dsa_kernel(bq_ref, iq_ref, qmisct_ref, kk_ref, v1t_ref, tri_ref, o_ref,
                sc_ref, s_ref, qs_ref, acc_ref, *, tq, tk, topk):
    qi = pl.program_id(1)
    q0 = qi * tq
    nkb = (q0 + tq + tk - 1) // tk
    qpos = q0 + lax.broadcasted_iota(jnp.int32, (1, tq), 1)
    key_iota = lax.broadcasted_iota(jnp.int32, (tk, tq), 0)
    H = DSA_HEADS
    S = SUBLANES

    def krows(kb):
        return pl.ds(pl.multiple_of(kb * tk, tk), tk)

    def fold_rows(x, op):
        parts = [x[j * S:(j + 1) * S, :] for j in range(x.shape[0] // S)]
        while len(parts) > 1:
            parts = [op(parts[j], parts[j + 1]) for j in range(0, len(parts) - 1, 2)] + (
                [parts[-1]] if len(parts) % 2 else [])
        return parts[0]

    iw = [qmisct_ref[MISC_IW + h:MISC_IW + h + 1, :] for h in range(IDX_HEADS)]
    for h in range(IDX_HEADS):
        qs_ref[h * tq:(h + 1) * tq, :] = iq_ref[:, h * IDX_HD:(h + 1) * IDX_HD]

    def score_block(kb, mm):
        x = _dot_nt(kk_ref[krows(kb), IDX_HD:2 * IDX_HD], qs_ref[0:IDX_HEADS * tq, :])
        sc = iw[0] * jnp.maximum(x[:, 0:tq], 0.0)
        for h in range(1, IDX_HEADS):
            sc = sc + iw[h] * jnp.maximum(x[:, h * tq:(h + 1) * tq], 0.0)
        sc = sc + 0.0
        sc_ref[kb] = jnp.where(kb * tk + key_iota <= qpos, sc, -jnp.inf)
        return jnp.maximum(mm[0], fold_rows(sc, jnp.maximum)), jnp.minimum(mm[1], fold_rows(sc, jnp.minimum))

    mx8, mn8 = lax.fori_loop(0, nkb, score_block,
                             (jnp.full((S, tq), -jnp.inf, F32), jnp.full((S, tq), jnp.inf, F32)))
    hi_bound = jnp.max(mx8, axis=0, keepdims=True)
    lo_bound = jnp.min(mn8, axis=0, keepdims=True)

    def count_ge(cand):
        def body(kb, cnt):
            return cnt + fold_rows(jnp.where(sc_ref[kb] >= cand, 1, 0), jnp.add)
        cnt = lax.fori_loop(0, nkb, body, jnp.zeros((S, tq), jnp.int32))
        return jnp.sum(cnt, axis=0, keepdims=True)

    def step(st, pick):
        lo, hi, c_lo, c_hi, g_lo, g_hi, last = st
        done = jnp.where(c_lo == topk, 1, jnp.where(hi <= lo + 1, 1, 0))
        cand = jnp.minimum(jnp.maximum(pick(lo, hi, g_lo, g_hi), lo + 1), hi - 1)
        cand = jnp.where(done > 0, lo, cand)
        c = count_ge(_key_to_f32(cand))
        g = c.astype(F32) - (topk - 0.5)
        up = jnp.where(done > 0, 0, jnp.where(c >= topk, 1, 0))
        dn = jnp.where(done > 0, 0, jnp.where(c >= topk, 0, 1))
        g_hi = jnp.where(up * last > 0, g_hi * 0.5, g_hi)
        g_lo = jnp.where(dn * last < 0, g_lo * 0.5, g_lo)
        return (jnp.where(up > 0, cand, lo), jnp.where(dn > 0, cand, hi),
                jnp.where(up > 0, c, c_lo), jnp.where(dn > 0, c, c_hi),
                jnp.where(up > 0, g, g_lo), jnp.where(dn > 0, g, g_hi),
                jnp.where(up > 0, 1, jnp.where(dn > 0, -1, last)))

    def interpolate(lo, hi, g_lo, g_hi):
        v_lo, v_hi = _key_to_f32(lo), _key_to_f32(hi)
        return _f32_to_key(v_lo + (v_hi - v_lo) * (g_lo / (g_lo - g_hi)))

    def midpoint(lo, hi, g_lo, g_hi):
        return (lo >> 1) + (hi >> 1) + (lo & hi & 1)

    def unfinished(st):
        lo, hi, c_lo = st[0], st[1], st[2]
        return jnp.max(jnp.where(c_lo == topk, 0.0, jnp.where(hi <= lo + 1, 0.0, 1.0))) > 0.0

    def one_round(st, pick):
        for _ in range(SEARCH_ROUND):
            st = step(st, pick)
        return st

    def search(st, pick, max_rounds):
        return lax.while_loop(lambda c: jnp.logical_and(c[0] < max_rounds, unfinished(c[1])),
                              lambda c: (c[0] + 1, one_round(c[1], pick)), (jnp.int32(0), st))[1]

    n_causal = qpos + 1
    small = n_causal <= topk
    lo0 = jnp.where(small, KEY_NEG_INF, _f32_to_key(lo_bound))
    hi0 = _f32_to_key(hi_bound) + 1
    c_lo0 = jnp.where(small, topk, n_causal)
    st = (lo0, hi0, c_lo0, jnp.zeros((1, tq), jnp.int32), c_lo0.astype(F32) - (topk - 0.5),
          jnp.full((1, tq), 0.5 - topk, F32), jnp.zeros((1, tq), jnp.int32))
    st = step(st, lambda lo, hi, g_lo, g_hi: jnp.zeros_like(lo))
    st = step(st, lambda lo, hi, g_lo, g_hi: jnp.ones_like(lo))
    st = lax.fori_loop(0, SEARCH_WARMUP_ROUNDS, lambda _, st: one_round(st, interpolate), st)
    st = search(st, interpolate, SEARCH_INTERP_ROUNDS)
    st = search(st, midpoint, SEARCH_BISECT_ROUNDS)
    lo, c_lo, c_hi = st[0], st[2], st[3]
    thr = _key_to_f32(lo)
    need = jnp.where(small, 0, jnp.where(c_lo == topk, topk, topk - c_hi)).astype(F32)

    tri = tri_ref[...]

    def bias_block(kb, base):
        sc = sc_ref[kb]
        eq = jnp.where(sc == thr, 1.0, 0.0).astype(BF16)
        rank = _dot(tri, eq) + base
        sc_ref[kb] = jnp.where(sc > thr, 0.0,
                               jnp.where(sc == thr, jnp.where(rank <= need, 0.0, NEG_BIG), NEG_BIG))
        return rank[tk - 1:tk, :]

    lax.fori_loop(0, nkb, bias_block, jnp.zeros((1, tq), F32))

    for h in range(H):
        qs_ref[h * tq:(h + 1) * tq, :] = bq_ref[:, h * DSA_HD:(h + 1) * DSA_HD]

    def qk_block(kb, mx):
        s = _dot_nt(kk_ref[krows(kb), 0:DSA_HD], qs_ref[...])
        s = s + jnp.concatenate([sc_ref[kb]] * H, axis=1)
        s_ref[kb] = s
        return jnp.maximum(mx, fold_rows(s, jnp.maximum))

    mx = lax.fori_loop(0, nkb, qk_block, jnp.full((S, H * tq), NEG_BIG, F32))
    m = jnp.max(mx, axis=0, keepdims=True)
    acc_ref[...] = jnp.zeros(acc_ref.shape, F32)

    def pv_block(kb, carry):
        p = jnp.exp2(s_ref[kb] - m).astype(BF16)
        acc_ref[...] += _dot(v1t_ref[:, krows(kb)], p)
        return carry

    lax.fori_loop(0, nkb, pv_block, 0)
    acc = acc_ref[...]
    out = acc[0:DSA_HD, :] / acc[DSA_HD:DSA_HD + 1, :]
    for h in range(H):
        o_ref[:, h * DSA_HD:(h + 1) * DSA_HD] = out[:, h * tq:(h + 1) * tq].T.astype(o_ref.dtype)


def _dsa(bq, iq, misct, kk, v1t, B, T, tq, tk, topk):
    N = B * T
    nq = T // tq
    qrow = lambda b, i: (b * nq + i, 0)
    r = lax.broadcasted_iota(jnp.int32, (tk, tk), 0)
    c = lax.broadcasted_iota(jnp.int32, (tk, tk), 1)
    tri = jnp.where(c <= r, 1.0, 0.0).astype(BF16)
    return pl.pallas_call(
        functools.partial(_dsa_kernel, tq=tq, tk=tk, topk=topk),
        grid=(B, nq),
        in_specs=[
            pl.BlockSpec((tq, 512), qrow), pl.BlockSpec((tq, 256), qrow),
            pl.BlockSpec((LANES, tq), lambda b, i: (0, b * nq + i)),
            pl.BlockSpec((T, LANES), lambda b, i: (b, 0)),
            pl.BlockSpec((LANES, T), lambda b, i: (0, b)),
            pl.BlockSpec((tk, tk), lambda b, i: (0, 0)),
        ],
        out_specs=pl.BlockSpec((tq, 512), qrow),
        out_shape=jax.ShapeDtypeStruct((N, 512), BF16),
        scratch_shapes=[
            pltpu.VMEM((T // tk, tk, tq), F32),
            pltpu.VMEM((T // tk, tk, DSA_HEADS * tq), F32),
            pltpu.VMEM((DSA_HEADS * tq, DSA_HD), BF16),
            pltpu.VMEM((LANES, DSA_HEADS * tq), F32),
        ],
        compiler_params=_params("parallel", "arbitrary"),
        name="dsa",
    )(bq, iq, misct, kk, v1t, tri)


def _out_proj_kernel(x_ref, oa_ref, ob_ref, wa_ref, wb_ref, o_ref):
    o_ref[...] = x_ref[...] + _dot(oa_ref[...], wa_ref[...]) + _dot(ob_ref[...], wb_ref[...])


def _out_proj(x2, oa, ob, w_o, tm):
    N = x2.shape[0]
    wa = w_o[:512].astype(BF16)
    wb = w_o[512:].astype(BF16)
    row = lambda i: (i, 0)
    fixed = lambda i: (0, 0)
    return pl.pallas_call(
        _out_proj_kernel,
        grid=(N // tm,),
        in_specs=[pl.BlockSpec((tm, D_MODEL), row), pl.BlockSpec((tm, 512), row), pl.BlockSpec((tm, 512), row),
                  pl.BlockSpec((512, D_MODEL), fixed), pl.BlockSpec((512, D_MODEL), fixed)],
        out_specs=pl.BlockSpec((tm, D_MODEL), row),
        out_shape=jax.ShapeDtypeStruct((N, D_MODEL), F32),
        compiler_params=_params("parallel"),
        name="out_proj",
    )(x2, oa, ob, wa, wb)


FFN_CHUNKS = (768, 768, 768, 512)


def _ffn_kernel(x_ref, g_ref, wup_ref, cw_ref, cb_ref, wd_ref, fg_ref, o_ref, h_ref, act_ref, acc_ref, *ab_refs,
                tm, tiles_per_seq, final_norm, rb):
    i = pl.program_id(0)
    H = SUBLANES
    n_c = len(FFN_CHUNKS)
    ag_refs, au_refs = ab_refs[:n_c], ab_refs[n_c:]
    offs = [sum(FFN_CHUNKS[:c]) for c in range(n_c)]
    first = (i % tiles_per_seq) == 0

    @pl.when(first)
    def _():
        for ref in ab_refs:
            ref[0:H, :] = jnp.zeros((H, ref.shape[1]), F32)

    @pl.when(jnp.logical_not(first))
    def _():
        for ref in ab_refs:
            ref[0:H, :] = ref[tm:tm + H, :]

    def up_proj(c):
        o, w = offs[c], FFN_CHUNKS[c]
        ag_refs[c][H:H + tm, :] = _dot(hb, wup_ref[:, o:o + w])
        au_refs[c][H:H + tm, :] = _dot(hb, wup_ref[:, D_FF + o:D_FF + o + w])

    def gate_act(c):
        o, w = offs[c], FFN_CHUNKS[c]
        cwg, cwu = cw_ref[:, o:o + w], cw_ref[:, D_FF + o:D_FF + o + w]
        cbg, cbu = cb_ref[:, o:o + w], cb_ref[:, D_FF + o:D_FF + o + w]
        for r0 in range(0, tm, rb):
            wg = ag_refs[c][r0:r0 + rb + H, :]
            wu = au_refs[c][r0:r0 + rb + H, :]
            gate, up = cbg, cbu
            for j in range(CONV_W):
                s0 = H - (CONV_W - 1) + j
                gate = gate + cwg[j:j + 1, :] * wg[s0:s0 + rb, :]
                up = up + cwu[j:j + 1, :] * wu[s0:s0 + rb, :]
            act_ref[r0:r0 + rb, o:o + w] = (gate * jax.nn.sigmoid(gate) * up).astype(BF16)

    def down_proj(c):
        o, w = offs[c], FFN_CHUNKS[c]
        d = _dot(act_ref[:, o:o + w], wd_ref[o:o + w, :])
        if c == 0:
            acc_ref[...] = d
        else:
            acc_ref[...] += d

    h_ref[...] = _rmsnorm_rows(x_ref[...], g_ref[...]).astype(BF16)
    hb = h_ref[...]
    up_proj(0)
    for c in range(n_c):
        if c + 1 < n_c:
            up_proj(c + 1)
        gate_act(c)
        down_proj(c)
    y = x_ref[...] + acc_ref[...]
    if final_norm:
        y = _rmsnorm_rows(y, fg_ref[...])
    o_ref[...] = y


def _conv_ffn(x2, g, w_up, conv_w, conv_b, w_down, final_g, T, tm, final_norm):
    N = x2.shape[0]
    row = lambda i: (i, 0)
    fixed = lambda i: (0, 0)
    def wspec(shape):
        return pl.BlockSpec(shape, fixed, pipeline_mode=pl.Buffered(1))
    return pl.pallas_call(
        functools.partial(_ffn_kernel, tm=tm, tiles_per_seq=T // tm, final_norm=final_norm, rb=64),
        grid=(N // tm,),
        in_specs=[
            pl.BlockSpec((tm, D_MODEL), row),
            pl.BlockSpec((1, D_MODEL), fixed),
            wspec((D_MODEL, 2 * D_FF)),
            pl.BlockSpec((CONV_W, 2 * D_FF), fixed),
            pl.BlockSpec((1, 2 * D_FF), fixed),
            wspec((D_FF, D_MODEL)),
            pl.BlockSpec((1, D_MODEL), fixed),
        ],
        out_specs=pl.BlockSpec((tm, D_MODEL), row),
        out_shape=jax.ShapeDtypeStruct((N, D_MODEL), F32),
        scratch_shapes=[
            pltpu.VMEM((tm, D_MODEL), BF16),
            pltpu.VMEM((tm, D_FF), BF16),
            pltpu.VMEM((tm, D_MODEL), F32),
        ] + [pltpu.VMEM((tm + SUBLANES, w), F32) for w in FFN_CHUNKS] * 2,
        compiler_params=_params("arbitrary"),
        name="conv_ffn",
    )(x2, g.reshape(1, -1), w_up.astype(BF16), conv_w, conv_b.reshape(1, -1), w_down.astype(BF16),
      final_g.reshape(1, -1))


def _sgu_kernel(x_ref, g_ref, wuv_ref, lng_ref, lnb_ref, ws_ref, bs_ref, wo_ref, o_ref,
                u_ref, v_ref, gated_ref, *, tm):
    W = D_MODEL
    gw = W // SGU_GROUPS
    h = _rmsnorm_rows(x_ref[...], g_ref[...]).astype(BF16)
    u_ref[...] = jax.nn.gelu(_dot(h, wuv_ref[:, 0:W]))
    v = jax.nn.gelu(_dot(h, wuv_ref[:, W:2 * W]))
    mu = jnp.mean(v, axis=-1, keepdims=True)
    vc = v - mu
    v = vc * lax.rsqrt(jnp.mean(vc * vc, axis=-1, keepdims=True) + LN_EPS) * lng_ref[...] + lnb_ref[...]
    v_ref[...] = v.astype(BF16)
    ri = lax.broadcasted_iota(jnp.int32, (SGU_CHUNK, SGU_CHUNK), 0)
    ci = lax.broadcasted_iota(jnp.int32, (SGU_CHUNK, SGU_CHUNK), 1)
    tril = ri >= ci
    for gi in range(SGU_GROUPS):
        cs = slice(gi * gw, (gi + 1) * gw)
        ws = jnp.where(tril, ws_ref[gi], 0.0).astype(BF16)
        bias = bs_ref[:, gi:gi + 1]
        for n in range(tm // SGU_CHUNK):
            rs = slice(n * SGU_CHUNK, (n + 1) * SGU_CHUNK)
            mixed = _dot(ws, v_ref[rs, cs]) + bias
            gated_ref[rs, cs] = (u_ref[rs, cs] * mixed).astype(BF16)
    o_ref[...] = x_ref[...] + _dot(gated_ref[...], wo_ref[...])


def _sgu(x2, g, w_uv, ln_g, ln_b, w_s, b_s, w_out, tm):
    N = x2.shape[0]
    row = lambda i: (i, 0)
    fixed = lambda i: (0, 0)
    return pl.pallas_call(
        functools.partial(_sgu_kernel, tm=tm),
        grid=(N // tm,),
        in_specs=[
            pl.BlockSpec((tm, D_MODEL), row),
            pl.BlockSpec((1, D_MODEL), fixed),
            pl.BlockSpec((D_MODEL, 2 * D_MODEL), fixed),
            pl.BlockSpec((1, D_MODEL), fixed), pl.BlockSpec((1, D_MODEL), fixed),
            pl.BlockSpec((SGU_GROUPS, SGU_CHUNK, SGU_CHUNK), lambda i: (0, 0, 0)),
            pl.BlockSpec((SGU_CHUNK, SGU_GROUPS), fixed),
            pl.BlockSpec((D_MODEL, D_MODEL), fixed),
        ],
        out_specs=pl.BlockSpec((tm, D_MODEL), row),
        out_shape=jax.ShapeDtypeStruct((N, D_MODEL), F32),
        scratch_shapes=[pltpu.VMEM((tm, D_MODEL), F32), pltpu.VMEM((tm, D_MODEL), BF16),
                        pltpu.VMEM((tm, D_MODEL), BF16)],
        compiler_params=_params("parallel"),
        name="sgu",
    )(x2, g.reshape(1, -1), w_uv.astype(BF16), ln_g.reshape(1, -1), ln_b.reshape(1, -1),
      w_s, b_s.T, w_out.astype(BF16))


def kernel(x, attn_norm, attn_w_in, gla_w_a2, gla_b_a, gla_head_g, attn_w_o, sgu_norm, sgu_w_uv, sgu_ln_g,
           sgu_ln_b, sgu_w_s, sgu_b_s, sgu_w_out, ffn_norm, ffn_w_up, ffn_conv_w, ffn_conv_b, ffn_w_down,
           final_norm):
    B, T, D = x.shape
    assert D == D_MODEL and T % 512 == 0
    topk = min(TOPK_MAX, T // 4)
    depth = ffn_norm.shape[0]
    x2 = x.reshape(B * T, D)
    for i in range(depth):
        j = i // 2
        if i % 2 == 0:
            aq, ak, av, ar, la, bq, iq, kk, v1t, misct = _in_proj(
                x2, attn_norm[j], attn_w_in[j], gla_w_a2[j], gla_b_a[j], T, 512)
            oa = _gla(aq, ak, av, la, ar, gla_head_g[j], B, T, 512)
            ob = _dsa(bq, iq, misct, kk, v1t, B, T, 128, 512, topk)
            x2 = _out_proj(x2, oa, ob, attn_w_o[j], 512)
        else:
            x2 = _sgu(x2, sgu_norm[j], sgu_w_uv[j], sgu_ln_g[j], sgu_ln_b[j], sgu_w_s[j], sgu_b_s[j],
                      sgu_w_out[j], 512)
        x2 = _conv_ffn(x2, ffn_norm[i], ffn_w_up[i], ffn_conv_w[i], ffn_conv_b[i], ffn_w_down[i],
                       final_norm, T, 512, i == depth - 1)
    return x2.reshape(B, T, D)
```

```python
import functools

import jax
import jax.numpy as jnp
from jax import lax
from jax.experimental import pallas as pl
from jax.experimental.pallas import tpu as pltpu

D_MODEL = 1024
GLA_HEADS = 4
GLA_DK = 64
GLA_DV = 128
GLA_GATE_RANK = 16
GLA_GATE_TAU = 16.0
GLA_CHUNK = 64
DSA_HEADS = 8
DSA_HD = 64
IDX_HEADS = 4
IDX_HD = 64
TOPK_MAX = 256
ROPE_THETA = 500000.0
ROPE_FRAC_DIV = 4
SGU_CHUNK = 128
SGU_GROUPS = 8
D_FF = 2816
CONV_W = 3
EPS = 1e-6
LN_EPS = 1e-5

LANES = 128
SUBLANES = 8
VMEM_LIMIT = 56 * 1024 * 1024

_SEG_AQ = (0, 256)
_SEG_AK = (256, 512)
_SEG_AV = (512, 1024)
_SEG_AR = (1024, 1536)
_SEG_BQ = (1536, 2048)
_SEG_IQ = (2048, 2304)
_SEG_KK = (2304, 2432)
_SEG_MISC = (2432, 2560)
IN_PAD = 2560
MISC_ALR = 64
MISC_IW = 80

F32 = jnp.float32
BF16 = jnp.bfloat16
NEG_BIG = -1e30
LOG2E = 1.4426950408889634


def _dot(a, b):
    return jnp.dot(a, b, preferred_element_type=F32)


def _dot_nt(a, b):
    return lax.dot_general(a, b, (((1,), (1,)), ((), ())), preferred_element_type=F32)


def _dot_tn(a, b):
    return lax.dot_general(a, b, (((0,), (0,)), ((), ())), preferred_element_type=F32)


def _rmsnorm_rows(x, g):
    ms = jnp.mean(x * x, axis=-1, keepdims=True)
    return x * lax.rsqrt(ms + EPS) * g


def _params(*sem):
    return pltpu.CompilerParams(dimension_semantics=sem, vmem_limit_bytes=VMEM_LIMIT)


def _rope_slab(x, tab):
    half = DSA_HD // ROPE_FRAC_DIV // 2
    c = tab[:, 0:LANES]
    s_up = tab[:, LANES:2 * LANES]
    s_dn = tab[:, 2 * LANES:3 * LANES]
    return x * c + pltpu.roll(x, half, 1) * s_up + pltpu.roll(x, LANES - half, 1) * s_dn


def _in_proj_kernel(x_ref, g_ref, w_ref, wa2_ref, ba_ref, tab2_ref,
                    aq_ref, ak_ref, av_ref, ar_ref, la_ref, bq_ref, iq_ref, kk_ref, v1t_ref, misct_ref):
    h = _rmsnorm_rows(x_ref[...], g_ref[...]).astype(BF16)

    def seg(s):
        return _dot(h, w_ref[:, s[0]:s[1]])

    aq_ref[...] = (seg(_SEG_AQ) * (GLA_DK ** -0.5)).astype(aq_ref.dtype)
    ak_ref[...] = seg(_SEG_AK).astype(ak_ref.dtype)
    av_ref[...] = seg(_SEG_AV).astype(av_ref.dtype)
    ar_ref[...] = seg(_SEG_AR).astype(ar_ref.dtype)

    tab2 = tab2_ref[...]
    bq = seg(_SEG_BQ)
    for j in range((_SEG_BQ[1] - _SEG_BQ[0]) // LANES):
        sl = slice(j * LANES, (j + 1) * LANES)
        bq_ref[:, sl] = (_rope_slab(bq[:, sl], tab2) * (DSA_HD ** -0.5 * LOG2E)).astype(bq_ref.dtype)
    iq = seg(_SEG_IQ)
    for j in range((_SEG_IQ[1] - _SEG_IQ[0]) // LANES):
        sl = slice(j * LANES, (j + 1) * LANES)
        iq_ref[:, sl] = (_rope_slab(iq[:, sl], tab2) * (IDX_HD ** -0.5)).astype(iq_ref.dtype)
    kk_ref[...] = _rope_slab(seg(_SEG_KK), tab2).astype(kk_ref.dtype)
    misc = seg(_SEG_MISC)
    misct = misc.T
    misct_ref[...] = misct
    row = lax.broadcasted_iota(jnp.int32, misct.shape, 0)
    v1t = jnp.where(row < DSA_HD, misct, jnp.where(row == DSA_HD, 1.0, 0.0))
    v1t_ref[...] = v1t.astype(v1t_ref.dtype)
    z = jnp.dot(misc, wa2_ref[...], preferred_element_type=F32,
                precision=lax.Precision.HIGHEST) + ba_ref[...]
    la_ref[...] = (jnp.minimum(z, 0.0) - jnp.log(1.0 + jnp.exp(-jnp.abs(z)))) * (1.0 / GLA_GATE_TAU)


def _rope_tables(T):
    rd = DSA_HD // ROPE_FRAC_DIV
    half = rd // 2
    pos = jnp.arange(T, dtype=F32)
    inv = jnp.power(ROPE_THETA, -(jnp.arange(half, dtype=F32) * 2.0 / rd))
    ang = pos[:, None] * inv[None, :]
    cos, sin = jnp.cos(ang), jnp.sin(ang)
    z = jnp.zeros((T, DSA_HD - rd), F32)
    zh = jnp.zeros((T, half), F32)
    c64 = jnp.concatenate([cos, cos, z + 1.0], axis=1)
    up64 = jnp.concatenate([zh, sin, z], axis=1)
    dn64 = jnp.concatenate([-sin, zh, z], axis=1)
    return jnp.concatenate([c64, c64, up64, up64, dn64, dn64], axis=1)


def _in_proj(x2, g, w_in, w_a2, b_a, T, tm):
    N = x2.shape[0]
    widths = (256, 256, 512, 512, 16, 512, 64, 64, 256, 64, 4)
    offs = [0]
    for w in widths:
        offs.append(offs[-1] + w)
    aq, ak, av, ar, alr, bq, bk, bv, iq, ik, iw = [w_in[:, offs[i]:offs[i + 1]] for i in range(11)]
    pad = jnp.zeros((D_MODEL, LANES - IDX_HD - GLA_GATE_RANK - IDX_HEADS), w_in.dtype)
    wp = jnp.concatenate([aq, ak, av, ar, bq, iq, bk, ik, bv, alr, iw * (IDX_HEADS ** -0.5), pad],
                         axis=1).astype(BF16)
    wa2 = jnp.zeros((LANES, GLA_HEADS * GLA_DK), F32).at[MISC_ALR:MISC_ALR + GLA_GATE_RANK].set(w_a2)
    tab2 = _rope_tables(T)
    nt = T // tm
    row = lambda i: (i, 0)
    fixed = lambda i: (0, 0)
    tabm = lambda i: (i % nt, 0)
    outs = [(256, BF16), (256, BF16), (512, BF16), (512, BF16), (256, F32),
            (512, BF16), (256, BF16), (128, BF16)]
    col = lambda i: (0, i)
    return pl.pallas_call(
        _in_proj_kernel,
        grid=(N // tm,),
        in_specs=[
            pl.BlockSpec((tm, D_MODEL), row),
            pl.BlockSpec((1, D_MODEL), fixed),
            pl.BlockSpec((D_MODEL, IN_PAD), fixed),
            pl.BlockSpec((LANES, GLA_HEADS * GLA_DK), fixed),
            pl.BlockSpec((1, GLA_HEADS * GLA_DK), fixed),
            pl.BlockSpec((tm, 3 * LANES), tabm),
        ],
        out_specs=[pl.BlockSpec((tm, w), row) for w, _ in outs]
        + [pl.BlockSpec((LANES, tm), col), pl.BlockSpec((LANES, tm), col)],
        out_shape=[jax.ShapeDtypeStruct((N, w), d) for w, d in outs]
        + [jax.ShapeDtypeStruct((LANES, N), BF16), jax.ShapeDtypeStruct((LANES, N), F32)],
        compiler_params=_params("parallel"),
        name="in_proj",
    )(x2, g.reshape(1, -1), wp, wa2, b_a.reshape(1, -1), tab2)


def _gla_kernel(q_ref, k_ref, v_ref, la_ref, r_ref, hg_ref, o_ref, st_ref, *, n_chunks):
    C = GLA_CHUNK

    @pl.when(pl.program_id(1) == 0)
    def _():
        st_ref[...] = jnp.zeros_like(st_ref)

    ri = lax.broadcasted_iota(jnp.int32, (C, C), 0)
    ci = lax.broadcasted_iota(jnp.int32, (C, C), 1)
    tril = ri >= ci
    tri_f = jnp.where(tril, 1.0, 0.0).astype(F32)
    hg = hg_ref[...]

    def chunk(c, carry):
        rows = pl.ds(pl.multiple_of(c * C, C), C)
        la = la_ref[rows, :]
        b = jnp.dot(tri_f, la, preferred_element_type=F32, precision=lax.Precision.HIGHEST)
        b_mid = b[C // 2:C // 2 + 1, :]
        b_last = b[C - 1:C, :]
        q = q_ref[rows, :].astype(F32)
        k = k_ref[rows, :].astype(F32)
        qe = (q * jnp.exp(b - b_mid)).astype(BF16)
        ke = (k * jnp.exp(b_mid - b)).astype(BF16)
        kl = (k * jnp.exp(b_last - b)).astype(BF16)
        qb = (q * jnp.exp(b)).astype(BF16)
        dec = jnp.exp(b_last)
        for hh in range(GLA_HEADS):
            ks = slice(hh * GLA_DK, (hh + 1) * GLA_DK)
            vs = slice(hh * GLA_DV, (hh + 1) * GLA_DV)
            v = v_ref[rows, vs]
            att = jnp.where(tril, _dot_nt(qe[:, ks], ke[:, ks]), 0.0)
            st = st_ref[hh]
            o = _dot(att.astype(BF16), v) + _dot_nt(qb[:, ks], st.astype(BF16))
            st_ref[hh] = st * dec[:, ks] + _dot_tn(v, kl[:, ks])
            o = o * lax.rsqrt(jnp.mean(o * o, axis=-1, keepdims=True) + EPS) * hg
            r = r_ref[rows, vs].astype(F32)
            o_ref[rows, vs] = (o * (r * jax.nn.sigmoid(r))).astype(o_ref.dtype)
        return carry

    lax.fori_loop(0, n_chunks, chunk, 0)


def _gla(aq, ak, av, la, ar, head_g, B, T, tg):
    N = B * T
    nt = T // tg
    row = lambda b, i: (b * nt + i, 0)
    return pl.pallas_call(
        functools.partial(_gla_kernel, n_chunks=tg // GLA_CHUNK),
        grid=(B, nt),
        in_specs=[
            pl.BlockSpec((tg, 256), row), pl.BlockSpec((tg, 256), row), pl.BlockSpec((tg, 512), row),
            pl.BlockSpec((tg, 256), row), pl.BlockSpec((tg, 512), row),
            pl.BlockSpec((1, GLA_DV), lambda b, i: (0, 0)),
        ],
        out_specs=pl.BlockSpec((tg, 512), row),
        out_shape=jax.ShapeDtypeStruct((N, 512), BF16),
        scratch_shapes=[pltpu.VMEM((GLA_HEADS, GLA_DV, GLA_DK), F32)],
        compiler_params=_params("parallel", "arbitrary"),
        name="gla",
    )(aq, ak, av, la, ar, head_g.reshape(1, -1))


SUBNORMAL_MAX = 0x007FFFFF
KEY_POS_INF = 0x7F800000 - SUBNORMAL_MAX
KEY_NEG_INF = -KEY_POS_INF
SEARCH_ROUND = 3
SEARCH_WARMUP_ROUNDS = 4
SEARCH_INTERP_ROUNDS = 12
SEARCH_BISECT_ROUNDS = 11


def _f32_to_key(v):
    bits = lax.bitcast_convert_type(v, jnp.int32)
    kmag = jnp.maximum((bits & jnp.int32(0x7FFFFFFF)) - SUBNORMAL_MAX, 0)
    return jnp.where(bits < 0, -kmag, kmag)


def _key_to_f32(key):
    kmag = jnp.abs(key)
    mag = jnp.where(kmag > 0, kmag + SUBNORMAL_MAX, 0)
    return lax.bitcast_convert_type(jnp.where(key < 0, mag | jnp.int32(-2147483648), mag), F32)


def _dsa_kernel(bq_ref, iq_ref, qmisct_ref, kk_ref, v1t_ref, tri_ref, o_ref,
                sc_ref, s_ref, qs_ref, acc_ref, *, tq, tk, topk):
    qi = pl.program_id(1)
    q0 = qi * tq
    nkb = (q0 + tq + tk - 1) // tk
    qpos = q0 + lax.broadcasted_iota(jnp.int32, (1, tq), 1)
    key_iota = lax.broadcasted_iota(jnp.int32, (tk, tq), 0)
    H = DSA_HEADS
    S = SUBLANES

    def krows(kb):
        return pl.ds(pl.multiple_of(kb * tk, tk), tk)

    def fold_rows(x, op):
        parts = [x[j * S:(j + 1) * S, :] for j in range(x.shape[0] // S)]
        while len(parts) > 1:
            parts = [op(parts[j], parts[j + 1]) for j in range(0, len(parts) - 1, 2)] + (
                [parts[-1]] if len(parts) % 2 else [])
        return parts[0]

    iw = [qmisct_ref[MISC_IW + h:MISC_IW + h + 1, :] for h in range(IDX_HEADS)]
    for h in range(IDX_HEADS):
        qs_ref[h * tq:(h + 1) * tq, :] = iq_ref[:, h * IDX_HD:(h + 1) * IDX_HD]

    def score_block(kb, mm):
        x = _dot_nt(kk_ref[krows(kb), IDX_HD:2 * IDX_HD], qs_ref[0:IDX_HEADS * tq, :])
        sc = iw[0] * jnp.maximum(x[:, 0:tq], 0.0)
        for h in range(1, IDX_HEADS):
            sc = sc + iw[h] * jnp.maximum(x[:, h * tq:(h + 1) * tq], 0.0)
        sc = sc + 0.0
        sc_ref[kb] = jnp.where(kb * tk + key_iota <= qpos, sc, -jnp.inf)
        return jnp.maximum(mm[0], fold_rows(sc, jnp.maximum)), jnp.minimum(mm[1], fold_rows(sc, jnp.minimum))

    mx8, mn8 = lax.fori_loop(0, nkb, score_block,
                             (jnp.full((S, tq), -jnp.inf, F32), jnp.full((S, tq), jnp.inf, F32)))
    hi_bound = jnp.max(mx8, axis=0, keepdims=True)
    lo_bound = jnp.min(mn8, axis=0, keepdims=True)

    def count_ge(cand):
        def body(kb, cnt):
            return cnt + fold_rows(jnp.where(sc_ref[kb] >= cand, 1, 0), jnp.add)
        cnt = lax.fori_loop(0, nkb, body, jnp.zeros((S, tq), jnp.int32))
        return jnp.sum(cnt, axis=0, keepdims=True)

    def step(st, pick):
        lo, hi, c_lo, c_hi, g_lo, g_hi, last = st
        done = jnp.where(c_lo == topk, 1, jnp.where(hi <= lo + 1, 1, 0))
        cand = jnp.minimum(jnp.maximum(pick(lo, hi, g_lo, g_hi), lo + 1), hi - 1)
        cand = jnp.where(done > 0, lo, cand)
        c = count_ge(_key_to_f32(cand))
        g = c.astype(F32) - (topk - 0.5)
        up = jnp.where(done > 0, 0, jnp.where(c >= topk, 1, 0))
        dn = jnp.where(done > 0, 0, jnp.where(c >= topk, 0, 1))
        g_hi = jnp.where(up * last > 0, g_hi * 0.5, g_hi)
        g_lo = jnp.where(dn * last < 0, g_lo * 0.5, g_lo)
        return (jnp.where(up > 0, cand, lo), jnp.where(dn > 0, cand, hi),
                jnp.where(up > 0, c, c_lo), jnp.where(dn > 0, c, c_hi),
                jnp.where(up > 0, g, g_lo), jnp.where(dn > 0, g, g_hi),
                jnp.where(up > 0, 1, jnp.where(dn > 0, -1, last)))

    def interpolate(lo, hi, g_lo, g_hi):
        v_lo, v_hi = _key_to_f32(lo), _key_to_f32(hi)
        return _f32_to_key(v_lo + (v_hi - v_lo) * (g_lo / (g_lo - g_hi)))

    def midpoint(lo, hi, g_lo, g_hi):
        return (lo >> 1) + (hi >> 1) + (lo & hi & 1)

    def unfinished(st):
        lo, hi, c_lo = st[0], st[1], st[2]
        return jnp.max(jnp.where(c_lo == topk, 0.0, jnp.where(hi <= lo + 1, 0.0, 1.0))) > 0.0

    def one_round(st, pick):
        for _ in range(SEARCH_ROUND):
            st = step(st, pick)
        return st

    def search(st, pick, max_rounds):
        return lax.while_loop(lambda c: jnp.logical_and(c[0] < max_rounds, unfinished(c[1])),
                              lambda c: (c[0] + 1, one_round(c[1], pick)), (jnp.int32(0), st))[1]

    n_causal = qpos + 1
    small = n_causal <= topk
    lo0 = jnp.where(small, KEY_NEG_INF, _f32_to_key(lo_bound))
    hi0 = _f32_to_key(hi_bound) + 1
    c_lo0 = jnp.where(small, topk, n_causal)
    st = (lo0, hi0, c_lo0, jnp.zeros((1, tq), jnp.int32), c_lo0.astype(F32) - (topk - 0.5),
          jnp.full((1, tq), 0.5 - topk, F32), jnp.zeros((1, tq), jnp.int32))
    st = step(st, lambda lo, hi, g_lo, g_hi: jnp.zeros_like(lo))
    st = step(st, lambda lo, hi, g_lo, g_hi: jnp.ones_like(lo))
    st = lax.fori_loop(0, SEARCH_WARMUP_ROUNDS, lambda _, st: one_round(st, interpolate), st)
    st = search(st, interpolate, SEARCH_INTERP_ROUNDS)
    st = search(st, midpoint, SEARCH_BISECT_ROUNDS)
    lo, c_lo, c_hi = st[0], st[2], st[3]
    thr = _key_to_f32(lo)
    need = jnp.where(small, 0, jnp.where(c_lo == topk, topk, topk - c_hi)).astype(F32)

    for h in range(H):
        qs_ref[h * tq:(h + 1) * tq, :] = bq_ref[:, h * DSA_HD:(h + 1) * DSA_HD]
    tri = tri_ref[...]

    def qk_block(kb, carry):
        mx, base = carry
        sc = sc_ref[kb]
        eq = jnp.where(sc == thr, 1.0, 0.0).astype(BF16)
        rank = _dot(tri, eq) + base
        bias = jnp.where(sc > thr, 0.0, jnp.where(sc == thr, jnp.where(rank <= need, 0.0, NEG_BIG), NEG_BIG))
        s = _dot_nt(kk_ref[krows(kb), 0:DSA_HD], qs_ref[...])
        s = s + jnp.concatenate([bias] * H, axis=1)
        s_ref[kb] = s
        return jnp.maximum(mx, fold_rows(s, jnp.maximum)), rank[tk - 1:tk, :]

    mx, _ = lax.fori_loop(0, nkb, qk_block, (jnp.full((S, H * tq), NEG_BIG, F32), jnp.zeros((1, tq), F32)))
    m = jnp.max(mx, axis=0, keepdims=True)
    acc_ref[...] = jnp.zeros(acc_ref.shape, F32)

    def pv_block(kb, carry):
        p = jnp.exp2(s_ref[kb] - m).astype(BF16)
        acc_ref[...] += _dot(v1t_ref[:, krows(kb)], p)
        return carry

    lax.fori_loop(0, nkb, pv_block, 0)
    acc = acc_ref[...]
    out = acc[0:DSA_HD, :] / acc[DSA_HD:DSA_HD + 1, :]
    for h in range(H):
        o_ref[:, h * DSA_HD:(h + 1) * DSA_HD] = out[:, h * tq:(h + 1) * tq].T.astype(o_ref.dtype)


def _dsa(bq, iq, misct, kk, v1t, B, T, tq, tk, topk):
    N = B * T
    nq = T // tq
    qrow = lambda b, i: (b * nq + i, 0)
    r = lax.broadcasted_iota(jnp.int32, (tk, tk), 0)
    c = lax.broadcasted_iota(jnp.int32, (tk, tk), 1)
    tri = jnp.where(c <= r, 1.0, 0.0).astype(BF16)
    return pl.pallas_call(
        functools.partial(_dsa_kernel, tq=tq, tk=tk, topk=topk),
        grid=(B, nq),
        in_specs=[
            pl.BlockSpec((tq, 512), qrow), pl.BlockSpec((tq, 256), qrow),
            pl.BlockSpec((LANES, tq), lambda b, i: (0, b * nq + i)),
            pl.BlockSpec((T, LANES), lambda b, i: (b, 0)),
            pl.BlockSpec((LANES, T), lambda b, i: (0, b)),
            pl.BlockSpec((tk, tk), lambda b, i: (0, 0)),
        ],
        out_specs=pl.BlockSpec((tq, 512), qrow),
        out_shape=jax.ShapeDtypeStruct((N, 512), BF16),
        scratch_shapes=[
            pltpu.VMEM((T // tk, tk, tq), F32),
            pltpu.VMEM((T // tk, tk, DSA_HEADS * tq), F32),
            pltpu.VMEM((DSA_HEADS * tq, DSA_HD), BF16),
            pltpu.VMEM((LANES, DSA_HEADS * tq), F32),
        ],
        compiler_params=_params("parallel", "arbitrary"),
        name="dsa",
    )(bq, iq, misct, kk, v1t, tri)


def _out_proj_kernel(x_ref, oa_ref, ob_ref, wa_ref, wb_ref, o_ref):
    o_ref[...] = x_ref[...] + _dot(oa_ref[...], wa_ref[...]) + _dot(ob_ref[...], wb_ref[...])


def _out_proj(x2, oa, ob, w_o, tm):
    N = x2.shape[0]
    wa = w_o[:512].astype(BF16)
    wb = w_o[512:].astype(BF16)
    row = lambda i: (i, 0)
    fixed = lambda i: (0, 0)
    return pl.pallas_call(
        _out_proj_kernel,
        grid=(N // tm,),
        in_specs=[pl.BlockSpec((tm, D_MODEL), row), pl.BlockSpec((tm, 512), row), pl.BlockSpec((tm, 512), row),
                  pl.BlockSpec((512, D_MODEL), fixed), pl.BlockSpec((512, D_MODEL), fixed)],
        out_specs=pl.BlockSpec((tm, D_MODEL), row),
        out_shape=jax.ShapeDtypeStruct((N, D_MODEL), F32),
        compiler_params=_params("parallel"),
        name="out_proj",
    )(x2, oa, ob, wa, wb)


FFN_CHUNKS = (768, 768, 768, 512)


def _ffn_kernel(x_ref, g_ref, wup_ref, cw_ref, cb_ref, wd_ref, fg_ref, o_ref, h_ref, act_ref, acc_ref, *ab_refs,
                tm, tiles_per_seq, final_norm, rb):
    i = pl.program_id(0)
    H = SUBLANES
    n_c = len(FFN_CHUNKS)
    ag_refs, au_refs = ab_refs[:n_c], ab_refs[n_c:]
    offs = [sum(FFN_CHUNKS[:c]) for c in range(n_c)]
    first = (i % tiles_per_seq) == 0

    @pl.when(first)
    def _():
        for ref in ab_refs:
            ref[0:H, :] = jnp.zeros((H, ref.shape[1]), F32)

    @pl.when(jnp.logical_not(first))
    def _():
        for ref in ab_refs:
            ref[0:H, :] = ref[tm:tm + H, :]

    def up_proj(c):
        o, w = offs[c], FFN_CHUNKS[c]
        ag_refs[c][H:H + tm, :] = _dot(hb, wup_ref[:, o:o + w])
        au_refs[c][H:H + tm, :] = _dot(hb, wup_ref[:, D_FF + o:D_FF + o + w])

    def gate_act(c):
        o, w = offs[c], FFN_CHUNKS[c]
        cwg, cwu = cw_ref[:, o:o + w], cw_ref[:, D_FF + o:D_FF + o + w]
        cbg, cbu = cb_ref[:, o:o + w], cb_ref[:, D_FF + o:D_FF + o + w]
        for r0 in range(0, tm, rb):
            wg = ag_refs[c][r0:r0 + rb + H, :]
            wu = au_refs[c][r0:r0 + rb + H, :]
            gate, up = cbg, cbu
            for j in range(CONV_W):
                s0 = H - (CONV_W - 1) + j
                gate = gate + cwg[j:j + 1, :] * wg[s0:s0 + rb, :]
                up = up + cwu[j:j + 1, :] * wu[s0:s0 + rb, :]
            act_ref[r0:r0 + rb, o:o + w] = (gate * jax.nn.sigmoid(gate) * up).astype(BF16)

    def down_proj(c):
        o, w = offs[c], FFN_CHUNKS[c]
        d = _dot(act_ref[:, o:o + w], wd_ref[o:o + w, :])
        if c == 0:
            acc_ref[...] = d
        else:
            acc_ref[...] += d

    h_ref[...] = _rmsnorm_rows(x_ref[...], g_ref[...]).astype(BF16)
    hb = h_ref[...]
    up_proj(0)
    for c in range(n_c):
        if c + 1 < n_c:
            up_proj(c + 1)
        gate_act(c)
        down_proj(c)
    y = x_ref[...] + acc_ref[...]
    if final_norm:
        y = _rmsnorm_rows(y, fg_ref[...])
    o_ref[...] = y


def _conv_ffn(x2, g, w_up, conv_w, conv_b, w_down, final_g, T, tm, final_norm):
    N = x2.shape[0]
    row = lambda i: (i, 0)
    fixed = lambda i: (0, 0)
    def wspec(shape):
        return pl.BlockSpec(shape, fixed, pipeline_mode=pl.Buffered(1))
    return pl.pallas_call(
        functools.partial(_ffn_kernel, tm=tm, tiles_per_seq=T // tm, final_norm=final_norm, rb=64),
        grid=(N // tm,),
        in_specs=[
            pl.BlockSpec((tm, D_MODEL), row),
            pl.BlockSpec((1, D_MODEL), fixed),
            wspec((D_MODEL, 2 * D_FF)),
            pl.BlockSpec((CONV_W, 2 * D_FF), fixed),
            pl.BlockSpec((1, 2 * D_FF), fixed),
            wspec((D_FF, D_MODEL)),
            pl.BlockSpec((1, D_MODEL), fixed),
        ],
        out_specs=pl.BlockSpec((tm, D_MODEL), row),
        out_shape=jax.ShapeDtypeStruct((N, D_MODEL), F32),
        scratch_shapes=[
            pltpu.VMEM((tm, D_MODEL), BF16),
            pltpu.VMEM((tm, D_FF), BF16),
            pltpu.VMEM((tm, D_MODEL), F32),
        ] + [pltpu.VMEM((tm + SUBLANES, w), F32) for w in FFN_CHUNKS] * 2,
        compiler_params=_params("arbitrary"),
        name="conv_ffn",
    )(x2, g.reshape(1, -1), w_up.astype(BF16), conv_w, conv_b.reshape(1, -1), w_down.astype(BF16),
      final_g.reshape(1, -1))


def _sgu_kernel(x_ref, g_ref, wuv_ref, lng_ref, lnb_ref, ws_ref, bs_ref, wo_ref, o_ref,
                u_ref, v_ref, gated_ref, *, tm):
    W = D_MODEL
    gw = W // SGU_GROUPS
    h = _rmsnorm_rows(x_ref[...], g_ref[...]).astype(BF16)
    u_ref[...] = jax.nn.gelu(_dot(h, wuv_ref[:, 0:W]))
    v = jax.nn.gelu(_dot(h, wuv_ref[:, W:2 * W]))
    mu = jnp.mean(v, axis=-1, keepdims=True)
    vc = v - mu
    v = vc * lax.rsqrt(jnp.mean(vc * vc, axis=-1, keepdims=True) + LN_EPS) * lng_ref[...] + lnb_ref[...]
    v_ref[...] = v.astype(BF16)
    ri = lax.broadcasted_iota(jnp.int32, (SGU_CHUNK, SGU_CHUNK), 0)
    ci = lax.broadcasted_iota(jnp.int32, (SGU_CHUNK, SGU_CHUNK), 1)
    tril = ri >= ci
    for gi in range(SGU_GROUPS):
        cs = slice(gi * gw, (gi + 1) * gw)
        ws = jnp.where(tril, ws_ref[gi], 0.0).astype(BF16)
        bias = bs_ref[:, gi:gi + 1]
        for n in range(tm // SGU_CHUNK):
            rs = slice(n * SGU_CHUNK, (n + 1) * SGU_CHUNK)
            mixed = _dot(ws, v_ref[rs, cs]) + bias
            gated_ref[rs, cs] = (u_ref[rs, cs] * mixed).astype(BF16)
    o_ref[...] = x_ref[...] + _dot(gated_ref[...], wo_ref[...])


def _sgu(x2, g, w_uv, ln_g, ln_b, w_s, b_s, w_out, tm):
    N = x2.shape[0]
    row = lambda i: (i, 0)
    fixed = lambda i: (0, 0)
    return pl.pallas_call(
        functools.partial(_sgu_kernel, tm=tm),
        grid=(N // tm,),
        in_specs=[
            pl.BlockSpec((tm, D_MODEL), row),
            pl.BlockSpec((1, D_MODEL), fixed),
            pl.BlockSpec((D_MODEL, 2 * D_MODEL), fixed),
            pl.BlockSpec((1, D_MODEL), fixed), pl.BlockSpec((1, D_MODEL), fixed),
            pl.BlockSpec((SGU_GROUPS, SGU_CHUNK, SGU_CHUNK), lambda i: (0, 0, 0)),
            pl.BlockSpec((SGU_CHUNK, SGU_GROUPS), fixed),
            pl.BlockSpec((D_MODEL, D_MODEL), fixed),
        ],
        out_specs=pl.BlockSpec((tm, D_MODEL), row),
        out_shape=jax.ShapeDtypeStruct((N, D_MODEL), F32),
        scratch_shapes=[pltpu.VMEM((tm, D_MODEL), F32), pltpu.VMEM((tm, D_MODEL), BF16),
                        pltpu.VMEM((tm, D_MODEL), BF16)],
        compiler_params=_params("parallel"),
        name="sgu",
    )(x2, g.reshape(1, -1), w_uv.astype(BF16), ln_g.reshape(1, -1), ln_b.reshape(1, -1),
      w_s, b_s.T, w_out.astype(BF16))


def kernel(x, attn_norm, attn_w_in, gla_w_a2, gla_b_a, gla_head_g, attn_w_o, sgu_norm, sgu_w_uv, sgu_ln_g,
           sgu_ln_b, sgu_w_s, sgu_b_s, sgu_w_out, ffn_norm, ffn_w_up, ffn_conv_w, ffn_conv_b, ffn_w_down,
           final_norm):
    B, T, D = x.shape
    assert D == D_MODEL and T % 512 == 0
    topk = min(TOPK_MAX, T // 4)
    depth = ffn_norm.shape[0]
    x2 = x.reshape(B * T, D)
    for i in range(depth):
        j = i // 2
        if i % 2 == 0:
            aq, ak, av, ar, la, bq, iq, kk, v1t, misct = _in_proj(
                x2, attn_norm[j], attn_w_in[j], gla_w_a2[j], gla_b_a[j], T, 512)
            oa = _gla(aq, ak, av, la, ar, gla_head_g[j], B, T, 512)
            ob = _dsa(bq, iq, misct, kk, v1t, B, T, 128, 512, topk)
            x2 = _out_proj(x2, oa, ob, attn_w_o[j], 512)
        else:
            x2 = _sgu(x2, sgu_norm[j], sgu_w_uv[j], sgu_ln_g[j], sgu_ln_b[j], sgu_w_s[j], sgu_b_s[j],
                      sgu_w_out[j], 512)
        x2 = _conv_ffn(x2, ffn_norm[i], ffn_w_up[i], ffn_conv_w[i], ffn_conv_b[i], ffn_w_down[i],
                       final_norm, T, 512, i == depth - 1)
    return x2.reshape(B, T, D)
```

```python
import functools

import jax
import jax.numpy as jnp
from jax import lax
from jax.experimental import pallas as pl
from jax.experimental.pallas import tpu as pltpu

D_MODEL = 1024
GLA_HEADS = 4
GLA_DK = 64
GLA_DV = 128
GLA_GATE_RANK = 16
GLA_GATE_TAU = 16.0
GLA_CHUNK = 64
DSA_HEADS = 8
DSA_HD = 64
IDX_HEADS = 4
IDX_HD = 64
TOPK_MAX = 256
ROPE_THETA = 500000.0
ROPE_FRAC_DIV = 4
SGU_CHUNK = 128
SGU_GROUPS = 8
D_FF = 2816
CONV_W = 3
EPS = 1e-6
LN_EPS = 1e-5

LANES = 128
SUBLANES = 8
VMEM_LIMIT = 56 * 1024 * 1024

_SEG_AQ = (0, 256)
_SEG_AK = (256, 512)
_SEG_AV = (512, 1024)
_SEG_AR = (1024, 1536)
_SEG_BQ = (1536, 2048)
_SEG_IQ = (2048, 2304)
_SEG_KK = (2304, 2432)
_SEG_MISC = (2432, 2560)
IN_PAD = 2560
MISC_ALR = 64
MISC_IW = 80

F32 = jnp.float32
BF16 = jnp.bfloat16
NEG_BIG = -1e30
LOG2E = 1.4426950408889634


def _dot(a, b):
    return jnp.dot(a, b, preferred_element_type=F32)


def _dot_nt(a, b):
    return lax.dot_general(a, b, (((1,), (1,)), ((), ())), preferred_element_type=F32)


def _dot_tn(a, b):
    return lax.dot_general(a, b, (((0,), (0,)), ((), ())), preferred_element_type=F32)


def _rmsnorm_rows(x, g):
    ms = jnp.mean(x * x, axis=-1, keepdims=True)
    return x * lax.rsqrt(ms + EPS) * g


def _params(*sem):
    return pltpu.CompilerParams(dimension_semantics=sem, vmem_limit_bytes=VMEM_LIMIT)


def _rope_slab(x, tab):
    half = DSA_HD // ROPE_FRAC_DIV // 2
    c = tab[:, 0:LANES]
    s_up = tab[:, LANES:2 * LANES]
    s_dn = tab[:, 2 * LANES:3 * LANES]
    return x * c + pltpu.roll(x, half, 1) * s_up + pltpu.roll(x, LANES - half, 1) * s_dn


def _in_proj_kernel(x_ref, g_ref, w_ref, wa2_ref, ba_ref, tab2_ref,
                    aq_ref, ak_ref, av_ref, ar_ref, la_ref, bq_ref, iq_ref, kk_ref, v1t_ref, misct_ref):
    h = _rmsnorm_rows(x_ref[...], g_ref[...]).astype(BF16)

    def seg(s):
        return _dot(h, w_ref[:, s[0]:s[1]])

    aq_ref[...] = (seg(_SEG_AQ) * (GLA_DK ** -0.5)).astype(aq_ref.dtype)
    ak_ref[...] = seg(_SEG_AK).astype(ak_ref.dtype)
    av_ref[...] = seg(_SEG_AV).astype(av_ref.dtype)
    ar_ref[...] = seg(_SEG_AR).astype(ar_ref.dtype)

    tab2 = tab2_ref[...]
    bq = seg(_SEG_BQ)
    for j in range((_SEG_BQ[1] - _SEG_BQ[0]) // LANES):
        sl = slice(j * LANES, (j + 1) * LANES)
        bq_ref[:, sl] = (_rope_slab(bq[:, sl], tab2) * (DSA_HD ** -0.5 * LOG2E)).astype(bq_ref.dtype)
    iq = seg(_SEG_IQ)
    for j in range((_SEG_IQ[1] - _SEG_IQ[0]) // LANES):
        sl = slice(j * LANES, (j + 1) * LANES)
        iq_ref[:, sl] = (_rope_slab(iq[:, sl], tab2) * (IDX_HD ** -0.5)).astype(iq_ref.dtype)
    kk_ref[...] = _rope_slab(seg(_SEG_KK), tab2).astype(kk_ref.dtype)
    misc = seg(_SEG_MISC)
    misct = misc.T
    misct_ref[...] = misct
    row = lax.broadcasted_iota(jnp.int32, misct.shape, 0)
    v1t = jnp.where(row < DSA_HD, misct, jnp.where(row == DSA_HD, 1.0, 0.0))
    v1t_ref[...] = v1t.astype(v1t_ref.dtype)
    z = jnp.dot(misc, wa2_ref[...], preferred_element_type=F32,
                precision=lax.Precision.HIGHEST) + ba_ref[...]
    la_ref[...] = (jnp.minimum(z, 0.0) - jnp.log(1.0 + jnp.exp(-jnp.abs(z)))) * (1.0 / GLA_GATE_TAU)


def _rope_tables(T):
    rd = DSA_HD // ROPE_FRAC_DIV
    half = rd // 2
    pos = jnp.arange(T, dtype=F32)
    inv = jnp.power(ROPE_THETA, -(jnp.arange(half, dtype=F32) * 2.0 / rd))
    ang = pos[:, None] * inv[None, :]
    cos, sin = jnp.cos(ang), jnp.sin(ang)
    z = jnp.zeros((T, DSA_HD - rd), F32)
    zh = jnp.zeros((T, half), F32)
    c64 = jnp.concatenate([cos, cos, z + 1.0], axis=1)
    up64 = jnp.concatenate([zh, sin, z], axis=1)
    dn64 = jnp.concatenate([-sin, zh, z], axis=1)
    return jnp.concatenate([c64, c64, up64, up64, dn64, dn64], axis=1)


def _in_proj(x2, g, w_in, w_a2, b_a, T, tm):
    N = x2.shape[0]
    widths = (256, 256, 512, 512, 16, 512, 64, 64, 256, 64, 4)
    offs = [0]
    for w in widths:
        offs.append(offs[-1] + w)
    aq, ak, av, ar, alr, bq, bk, bv, iq, ik, iw = [w_in[:, offs[i]:offs[i + 1]] for i in range(11)]
    pad = jnp.zeros((D_MODEL, LANES - IDX_HD - GLA_GATE_RANK - IDX_HEADS), w_in.dtype)
    wp = jnp.concatenate([aq, ak, av, ar, bq, iq, bk, ik, bv, alr, iw * (IDX_HEADS ** -0.5), pad],
                         axis=1).astype(BF16)
    wa2 = jnp.zeros((LANES, GLA_HEADS * GLA_DK), F32).at[MISC_ALR:MISC_ALR + GLA_GATE_RANK].set(w_a2)
    tab2 = _rope_tables(T)
    nt = T // tm
    row = lambda i: (i, 0)
    fixed = lambda i: (0, 0)
    tabm = lambda i: (i % nt, 0)
    outs = [(256, BF16), (256, BF16), (512, BF16), (512, BF16), (256, F32),
            (512, BF16), (256, BF16), (128, BF16)]
    col = lambda i: (0, i)
    return pl.pallas_call(
        _in_proj_kernel,
        grid=(N // tm,),
        in_specs=[
            pl.BlockSpec((tm, D_MODEL), row),
            pl.BlockSpec((1, D_MODEL), fixed),
            pl.BlockSpec((D_MODEL, IN_PAD), fixed),
            pl.BlockSpec((LANES, GLA_HEADS * GLA_DK), fixed),
            pl.BlockSpec((1, GLA_HEADS * GLA_DK), fixed),
            pl.BlockSpec((tm, 3 * LANES), tabm),
        ],
        out_specs=[pl.BlockSpec((tm, w), row) for w, _ in outs]
        + [pl.BlockSpec((LANES, tm), col), pl.BlockSpec((LANES, tm), col)],
        out_shape=[jax.ShapeDtypeStruct((N, w), d) for w, d in outs]
        + [jax.ShapeDtypeStruct((LANES, N), BF16), jax.ShapeDtypeStruct((LANES, N), F32)],
        compiler_params=_params("parallel"),
        name="in_proj",
    )(x2, g.reshape(1, -1), wp, wa2, b_a.reshape(1, -1), tab2)


def _gla_kernel(q_ref, k_ref, v_ref, la_ref, r_ref, hg_ref, o_ref, st_ref, *, n_chunks):
    C = GLA_CHUNK

    @pl.when(pl.program_id(1) == 0)
    def _():
        st_ref[...] = jnp.zeros_like(st_ref)

    ri = lax.broadcasted_iota(jnp.int32, (C, C), 0)
    ci = lax.broadcasted_iota(jnp.int32, (C, C), 1)
    tril = ri >= ci
    tri_f = jnp.where(tril, 1.0, 0.0).astype(F32)
    hg = hg_ref[...]

    def chunk(c, carry):
        rows = pl.ds(pl.multiple_of(c * C, C), C)
        la = la_ref[rows, :]
        b = jnp.dot(tri_f, la, preferred_element_type=F32, precision=lax.Precision.HIGHEST)
        b_mid = b[C // 2:C // 2 + 1, :]
        b_last = b[C - 1:C, :]
        q = q_ref[rows, :].astype(F32)
        k = k_ref[rows, :].astype(F32)
        qe = (q * jnp.exp(b - b_mid)).astype(BF16)
        ke = (k * jnp.exp(b_mid - b)).astype(BF16)
        kl = (k * jnp.exp(b_last - b)).astype(BF16)
        qb = (q * jnp.exp(b)).astype(BF16)
        dec = jnp.exp(b_last)
        for hh in range(GLA_HEADS):
            ks = slice(hh * GLA_DK, (hh + 1) * GLA_DK)
            vs = slice(hh * GLA_DV, (hh + 1) * GLA_DV)
            v = v_ref[rows, vs]
            att = jnp.where(tril, _dot_nt(qe[:, ks], ke[:, ks]), 0.0)
            st = st_ref[hh]
            o = _dot(att.astype(BF16), v) + _dot_nt(qb[:, ks], st.astype(BF16))
            st_ref[hh] = st * dec[:, ks] + _dot_tn(v, kl[:, ks])
            o = o * lax.rsqrt(jnp.mean(o * o, axis=-1, keepdims=True) + EPS) * hg
            r = r_ref[rows, vs].astype(F32)
            o_ref[rows, vs] = (o * (r * jax.nn.sigmoid(r))).astype(o_ref.dtype)
        return carry

    lax.fori_loop(0, n_chunks, chunk, 0)


def _gla(aq, ak, av, la, ar, head_g, B, T, tg):
    N = B * T
    nt = T // tg
    row = lambda b, i: (b * nt + i, 0)
    return pl.pallas_call(
        functools.partial(_gla_kernel, n_chunks=tg // GLA_CHUNK),
        grid=(B, nt),
        in_specs=[
            pl.BlockSpec((tg, 256), row), pl.BlockSpec((tg, 256), row), pl.BlockSpec((tg, 512), row),
            pl.BlockSpec((tg, 256), row), pl.BlockSpec((tg, 512), row),
            pl.BlockSpec((1, GLA_DV), lambda b, i: (0, 0)),
        ],
        out_specs=pl.BlockSpec((tg, 512), row),
        out_shape=jax.ShapeDtypeStruct((N, 512), BF16),
        scratch_shapes=[pltpu.VMEM((GLA_HEADS, GLA_DV, GLA_DK), F32)],
        compiler_params=_params("parallel", "arbitrary"),
        name="gla",
    )(aq, ak, av, la, ar, head_g.reshape(1, -1))


MIN_NORMAL = 1.1754943508222875e-38
SEARCH_ROUND = 3
SEARCH_WARMUP_ROUNDS = 4
SEARCH_INTERP_ROUNDS = 12
SEARCH_BISECT_ROUNDS = 100


def _dsa_kernel(bq_ref, iq_ref, qmisct_ref, kk_ref, v1t_ref, tri_ref, o_ref,
                sc_ref, s_ref, qs_ref, acc_ref, *, tq, tk, topk):
    qi = pl.program_id(1)
    q0 = qi * tq
    nkb = (q0 + tq + tk - 1) // tk
    qpos = q0 + lax.broadcasted_iota(jnp.int32, (1, tq), 1)
    key_iota = lax.broadcasted_iota(jnp.int32, (tk, tq), 0)
    H = DSA_HEADS
    S = SUBLANES

    def krows(kb):
        return pl.ds(pl.multiple_of(kb * tk, tk), tk)

    def fold_rows(x, op):
        parts = [x[j * S:(j + 1) * S, :] for j in range(x.shape[0] // S)]
        while len(parts) > 1:
            parts = [op(parts[j], parts[j + 1]) for j in range(0, len(parts) - 1, 2)] + (
                [parts[-1]] if len(parts) % 2 else [])
        return parts[0]

    iw = [qmisct_ref[MISC_IW + h:MISC_IW + h + 1, :] for h in range(IDX_HEADS)]
    for h in range(IDX_HEADS):
        qs_ref[h * tq:(h + 1) * tq, :] = iq_ref[:, h * IDX_HD:(h + 1) * IDX_HD]

    def score_block(kb, mm):
        x = _dot_nt(kk_ref[krows(kb), IDX_HD:2 * IDX_HD], qs_ref[0:IDX_HEADS * tq, :])
        sc = iw[0] * jnp.maximum(x[:, 0:tq], 0.0)
        for h in range(1, IDX_HEADS):
            sc = sc + iw[h] * jnp.maximum(x[:, h * tq:(h + 1) * tq], 0.0)
        sc = sc + 0.0
        sc_ref[kb] = jnp.where(kb * tk + key_iota <= qpos, sc, -jnp.inf)
        return jnp.maximum(mm[0], fold_rows(sc, jnp.maximum)), jnp.minimum(mm[1], fold_rows(sc, jnp.minimum))

    mx8, mn8 = lax.fori_loop(0, nkb, score_block,
                             (jnp.full((S, tq), -jnp.inf, F32), jnp.full((S, tq), jnp.inf, F32)))
    hi_bound = jnp.max(mx8, axis=0, keepdims=True)
    lo_bound = jnp.min(mn8, axis=0, keepdims=True)

    def count_ge(cand):
        def body(kb, cnt):
            return cnt + fold_rows(jnp.where(sc_ref[kb] >= cand, 1, 0), jnp.add)
        cnt = lax.fori_loop(0, nkb, body, jnp.zeros((S, tq), jnp.int32))
        return jnp.sum(cnt, axis=0, keepdims=True)

    def midpoint(lo, hi):
        return 0.5 * lo + 0.5 * hi

    def finished(lo, hi, c_lo):
        mid = midpoint(lo, hi)
        closed = jnp.where(mid <= lo, 1, jnp.where(mid >= hi, 1, 0))
        closed = jnp.where(lo == 0.0, jnp.where(hi == MIN_NORMAL, 1, closed), closed)
        return jnp.where(c_lo == topk, 1, closed)

    def step(st, pick):
        lo, hi, c_lo, c_hi, g_lo, g_hi, last = st
        done = finished(lo, hi, c_lo)
        cand = pick(lo, hi, g_lo, g_hi)
        cand = jnp.where(cand > lo, jnp.where(cand < hi, cand, midpoint(lo, hi)), midpoint(lo, hi))
        cand = jnp.where(done > 0, lo, cand)
        c = count_ge(cand)
        g = c.astype(F32) - (topk - 0.5)
        up = jnp.where(done > 0, 0, jnp.where(c >= topk, 1, 0))
        dn = jnp.where(done > 0, 0, jnp.where(c >= topk, 0, 1))
        g_hi = jnp.where(up * last > 0, g_hi * 0.5, g_hi)
        g_lo = jnp.where(dn * last < 0, g_lo * 0.5, g_lo)
        return (jnp.where(up > 0, cand, lo), jnp.where(dn > 0, cand, hi),
                jnp.where(up > 0, c, c_lo), jnp.where(dn > 0, c, c_hi),
                jnp.where(up > 0, g, g_lo), jnp.where(dn > 0, g, g_hi),
                jnp.where(up > 0, 1, jnp.where(dn > 0, -1, last)))

    def interpolate(lo, hi, g_lo, g_hi):
        return lo + (hi - lo) * (g_lo / (g_lo - g_hi))

    def bisect(lo, hi, g_lo, g_hi):
        return midpoint(lo, hi)

    def unfinished(st):
        return jnp.max(jnp.where(finished(st[0], st[1], st[2]) > 0, 0.0, 1.0)) > 0.0

    def one_round(st, pick):
        for _ in range(SEARCH_ROUND):
            st = step(st, pick)
        return st

    def search(st, pick, max_rounds):
        return lax.while_loop(lambda c: jnp.logical_and(c[0] < max_rounds, unfinished(c[1])),
                              lambda c: (c[0] + 1, one_round(c[1], pick)), (jnp.int32(0), st))[1]

    n_causal = qpos + 1
    small = n_causal <= topk
    lo0 = jnp.where(small, -jnp.inf, lo_bound)
    hi0 = hi_bound + jnp.maximum(jnp.abs(hi_bound) * 1e-6, 1e-37)
    c_lo0 = jnp.where(small, topk, n_causal)
    st = (lo0, hi0, c_lo0, jnp.zeros((1, tq), jnp.int32), c_lo0.astype(F32) - (topk - 0.5),
          jnp.full((1, tq), 0.5 - topk, F32), jnp.zeros((1, tq), jnp.int32))
    st = step(st, lambda lo, hi, g_lo, g_hi: jnp.zeros_like(lo))
    st = step(st, lambda lo, hi, g_lo, g_hi: jnp.full_like(lo, MIN_NORMAL))
    st = lax.fori_loop(0, SEARCH_WARMUP_ROUNDS, lambda _, st: one_round(st, interpolate), st)
    st = search(st, interpolate, SEARCH_INTERP_ROUNDS)
    st = search(st, bisect, SEARCH_BISECT_ROUNDS)
    thr, c_lo, c_hi = st[0], st[2], st[3]
    need = jnp.where(small, 0, jnp.where(c_lo == topk, topk, topk - c_hi)).astype(F32)

    for h in range(H):
        qs_ref[h * tq:(h + 1) * tq, :] = bq_ref[:, h * DSA_HD:(h + 1) * DSA_HD]
    tri = tri_ref[...]

    def qk_block(kb, carry):
        mx, base = carry
        sc = sc_ref[kb]
        eq = jnp.where(sc == thr, 1.0, 0.0).astype(BF16)
        rank = _dot(tri, eq) + base
        bias = jnp.where(sc > thr, 0.0, jnp.where(sc == thr, jnp.where(rank <= need, 0.0, NEG_BIG), NEG_BIG))
        s = _dot_nt(kk_ref[krows(kb), 0:DSA_HD], qs_ref[...])
        s = s + jnp.concatenate([bias] * H, axis=1)
        s_ref[kb] = s
        return jnp.maximum(mx, fold_rows(s, jnp.maximum)), rank[tk - 1:tk, :]

    mx, _ = lax.fori_loop(0, nkb, qk_block, (jnp.full((S, H * tq), NEG_BIG, F32), jnp.zeros((1, tq), F32)))
    m = jnp.max(mx, axis=0, keepdims=True)
    acc_ref[...] = jnp.zeros(acc_ref.shape, F32)

    def pv_block(kb, carry):
        p = jnp.exp2(s_ref[kb] - m).astype(BF16)
        acc_ref[...] += _dot(v1t_ref[:, krows(kb)], p)
        return carry

    lax.fori_loop(0, nkb, pv_block, 0)
    acc = acc_ref[...]
    out = acc[0:DSA_HD, :] / acc[DSA_HD:DSA_HD + 1, :]
    for h in range(H):
        o_ref[:, h * DSA_HD:(h + 1) * DSA_HD] = out[:, h * tq:(h + 1) * tq].T.astype(o_ref.dtype)


def _dsa(bq, iq, misct, kk, v1t, B, T, tq, tk, topk):
    N = B * T
    nq = T // tq
    qrow = lambda b, i: (b * nq + i, 0)
    r = lax.broadcasted_iota(jnp.int32, (tk, tk), 0)
    c = lax.broadcasted_iota(jnp.int32, (tk, tk), 1)
    tri = jnp.where(c <= r, 1.0, 0.0).astype(BF16)
    return pl.pallas_call(
        functools.partial(_dsa_kernel, tq=tq, tk=tk, topk=topk),
        grid=(B, nq),
        in_specs=[
            pl.BlockSpec((tq, 512), qrow), pl.BlockSpec((tq, 256), qrow),
            pl.BlockSpec((LANES, tq), lambda b, i: (0, b * nq + i)),
            pl.BlockSpec((T, LANES), lambda b, i: (b, 0)),
            pl.BlockSpec((LANES, T), lambda b, i: (0, b)),
            pl.BlockSpec((tk, tk), lambda b, i: (0, 0)),
        ],
        out_specs=pl.BlockSpec((tq, 512), qrow),
        out_shape=jax.ShapeDtypeStruct((N, 512), BF16),
        scratch_shapes=[
            pltpu.VMEM((T // tk, tk, tq), F32),
            pltpu.VMEM((T // tk, tk, DSA_HEADS * tq), F32),
            pltpu.VMEM((DSA_HEADS * tq, DSA_HD), BF16),
            pltpu.VMEM((LANES, DSA_HEADS * tq), F32),
        ],
        compiler_params=_params("parallel", "arbitrary"),
        name="dsa",
    )(bq, iq, misct, kk, v1t, tri)


def _out_proj_kernel(x_ref, oa_ref, ob_ref, wa_ref, wb_ref, o_ref):
    o_ref[...] = x_ref[...] + _dot(oa_ref[...], wa_ref[...]) + _dot(ob_ref[...], wb_ref[...])


def _out_proj(x2, oa, ob, w_o, tm):
    N = x2.shape[0]
    wa = w_o[:512].astype(BF16)
    wb = w_o[512:].astype(BF16)
    row = lambda i: (i, 0)
    fixed = lambda i: (0, 0)
    return pl.pallas_call(
        _out_proj_kernel,
        grid=(N // tm,),
        in_specs=[pl.BlockSpec((tm, D_MODEL), row), pl.BlockSpec((tm, 512), row), pl.BlockSpec((tm, 512), row),
                  pl.BlockSpec((512, D_MODEL), fixed), pl.BlockSpec((512, D_MODEL), fixed)],
        out_specs=pl.BlockSpec((tm, D_MODEL), row),
        out_shape=jax.ShapeDtypeStruct((N, D_MODEL), F32),
        compiler_params=_params("parallel"),
        name="out_proj",
    )(x2, oa, ob, wa, wb)


FFN_CHUNKS = (768, 768, 768, 512)


def _ffn_kernel(x_ref, g_ref, wup_ref, cw_ref, cb_ref, wd_ref, fg_ref, o_ref, h_ref, act_ref, acc_ref, *ab_refs,
                tm, tiles_per_seq, final_norm, rb):
    i = pl.program_id(0)
    H = SUBLANES
    n_c = len(FFN_CHUNKS)
    ag_refs, au_refs = ab_refs[:n_c], ab_refs[n_c:]
    offs = [sum(FFN_CHUNKS[:c]) for c in range(n_c)]
    first = (i % tiles_per_seq) == 0

    @pl.when(first)
    def _():
        for ref in ab_refs:
            ref[0:H, :] = jnp.zeros((H, ref.shape[1]), F32)

    @pl.when(jnp.logical_not(first))
    def _():
        for ref in ab_refs:
            ref[0:H, :] = ref[tm:tm + H, :]

    def up_proj(c):
        o, w = offs[c], FFN_CHUNKS[c]
        ag_refs[c][H:H + tm, :] = _dot(hb, wup_ref[:, o:o + w])
        au_refs[c][H:H + tm, :] = _dot(hb, wup_ref[:, D_FF + o:D_FF + o + w])

    def gate_act(c):
        o, w = offs[c], FFN_CHUNKS[c]
        cwg, cwu = cw_ref[:, o:o + w], cw_ref[:, D_FF + o:D_FF + o + w]
        cbg, cbu = cb_ref[:, o:o + w], cb_ref[:, D_FF + o:D_FF + o + w]
        for r0 in range(0, tm, rb):
            wg = ag_refs[c][r0:r0 + rb + H, :]
            wu = au_refs[c][r0:r0 + rb + H, :]
            gate, up = cbg, cbu
            for j in range(CONV_W):
                s0 = H - (CONV_W - 1) + j
                gate = gate + cwg[j:j + 1, :] * wg[s0:s0 + rb, :]
                up = up + cwu[j:j + 1, :] * wu[s0:s0 + rb, :]
            act_ref[r0:r0 + rb, o:o + w] = (gate * jax.nn.sigmoid(gate) * up).astype(BF16)

    def down_proj(c):
        o, w = offs[c], FFN_CHUNKS[c]
        d = _dot(act_ref[:, o:o + w], wd_ref[o:o + w, :])
        if c == 0:
            acc_ref[...] = d
        else:
            acc_ref[...] += d

    h_ref[...] = _rmsnorm_rows(x_ref[...], g_ref[...]).astype(BF16)
    hb = h_ref[...]
    up_proj(0)
    for c in range(n_c):
        if c + 1 < n_c:
            up_proj(c + 1)
        gate_act(c)
        down_proj(c)
    y = x_ref[...] + acc_ref[...]
    if final_norm:
        y = _rmsnorm_rows(y, fg_ref[...])
    o_ref[...] = y


def _conv_ffn(x2, g, w_up, conv_w, conv_b, w_down, final_g, T, tm, final_norm):
    N = x2.shape[0]
    row = lambda i: (i, 0)
    fixed = lambda i: (0, 0)
    def wspec(shape):
        return pl.BlockSpec(shape, fixed, pipeline_mode=pl.Buffered(1))
    return pl.pallas_call(
        functools.partial(_ffn_kernel, tm=tm, tiles_per_seq=T // tm, final_norm=final_norm, rb=64),
        grid=(N // tm,),
        in_specs=[
            pl.BlockSpec((tm, D_MODEL), row),
            pl.BlockSpec((1, D_MODEL), fixed),
            wspec((D_MODEL, 2 * D_FF)),
            pl.BlockSpec((CONV_W, 2 * D_FF), fixed),
            pl.BlockSpec((1, 2 * D_FF), fixed),
            wspec((D_FF, D_MODEL)),
            pl.BlockSpec((1, D_MODEL), fixed),
        ],
        out_specs=pl.BlockSpec((tm, D_MODEL), row),
        out_shape=jax.ShapeDtypeStruct((N, D_MODEL), F32),
        scratch_shapes=[
            pltpu.VMEM((tm, D_MODEL), BF16),
            pltpu.VMEM((tm, D_FF), BF16),
            pltpu.VMEM((tm, D_MODEL), F32),
        ] + [pltpu.VMEM((tm + SUBLANES, w), F32) for w in FFN_CHUNKS] * 2,
        compiler_params=_params("arbitrary"),
        name="conv_ffn",
    )(x2, g.reshape(1, -1), w_up.astype(BF16), conv_w, conv_b.reshape(1, -1), w_down.astype(BF16),
      final_g.reshape(1, -1))


def _sgu_kernel(x_ref, g_ref, wuv_ref, lng_ref, lnb_ref, ws_ref, bs_ref, wo_ref, o_ref,
                u_ref, v_ref, gated_ref, *, tm):
    W = D_MODEL
    gw = W // SGU_GROUPS
    h = _rmsnorm_rows(x_ref[...], g_ref[...]).astype(BF16)
    u_ref[...] = jax.nn.gelu(_dot(h, wuv_ref[:, 0:W]))
    v = jax.nn.gelu(_dot(h, wuv_ref[:, W:2 * W]))
    mu = jnp.mean(v, axis=-1, keepdims=True)
    vc = v - mu
    v = vc * lax.rsqrt(jnp.mean(vc * vc, axis=-1, keepdims=True) + LN_EPS) * lng_ref[...] + lnb_ref[...]
    v_ref[...] = v.astype(BF16)
    ri = lax.broadcasted_iota(jnp.int32, (SGU_CHUNK, SGU_CHUNK), 0)
    ci = lax.broadcasted_iota(jnp.int32, (SGU_CHUNK, SGU_CHUNK), 1)
    tril = ri >= ci
    for gi in range(SGU_GROUPS):
        cs = slice(gi * gw, (gi + 1) * gw)
        ws = jnp.where(tril, ws_ref[gi], 0.0).astype(BF16)
        bias = bs_ref[:, gi:gi + 1]
        for n in range(tm // SGU_CHUNK):
            rs = slice(n * SGU_CHUNK, (n + 1) * SGU_CHUNK)
            mixed = _dot(ws, v_ref[rs, cs]) + bias
            gated_ref[rs, cs] = (u_ref[rs, cs] * mixed).astype(BF16)
    o_ref[...] = x_ref[...] + _dot(gated_ref[...], wo_ref[...])


def _sgu(x2, g, w_uv, ln_g, ln_b, w_s, b_s, w_out, tm):
    N = x2.shape[0]
    row = lambda i: (i, 0)
    fixed = lambda i: (0, 0)
    return pl.pallas_call(
        functools.partial(_sgu_kernel, tm=tm),
        grid=(N // tm,),
        in_specs=[
            pl.BlockSpec((tm, D_MODEL), row),
            pl.BlockSpec((1, D_MODEL), fixed),
            pl.BlockSpec((D_MODEL, 2 * D_MODEL), fixed),
            pl.BlockSpec((1, D_MODEL), fixed), pl.BlockSpec((1, D_MODEL), fixed),
            pl.BlockSpec((SGU_GROUPS, SGU_CHUNK, SGU_CHUNK), lambda i: (0, 0, 0)),
            pl.BlockSpec((SGU_CHUNK, SGU_GROUPS), fixed),
            pl.BlockSpec((D_MODEL, D_MODEL), fixed),
        ],
        out_specs=pl.BlockSpec((tm, D_MODEL), row),
        out_shape=jax.ShapeDtypeStruct((N, D_MODEL), F32),
        scratch_shapes=[pltpu.VMEM((tm, D_MODEL), F32), pltpu.VMEM((tm, D_MODEL), BF16),
                        pltpu.VMEM((tm, D_MODEL), BF16)],
        compiler_params=_params("parallel"),
        name="sgu",
    )(x2, g.reshape(1, -1), w_uv.astype(BF16), ln_g.reshape(1, -1), ln_b.reshape(1, -1),
      w_s, b_s.T, w_out.astype(BF16))


def kernel(x, attn_norm, attn_w_in, gla_w_a2, gla_b_a, gla_head_g, attn_w_o, sgu_norm, sgu_w_uv, sgu_ln_g,
           sgu_ln_b, sgu_w_s, sgu_b_s, sgu_w_out, ffn_norm, ffn_w_up, ffn_conv_w, ffn_conv_b, ffn_w_down,
           final_norm):
    B, T, D = x.shape
    assert D == D_MODEL and T % 512 == 0
    topk = min(TOPK_MAX, T // 4)
    depth = ffn_norm.shape[0]
    x2 = x.reshape(B * T, D)
    for i in range(depth):
        j = i // 2
        if i % 2 == 0:
            aq, ak, av, ar, la, bq, iq, kk, v1t, misct = _in_proj(
                x2, attn_norm[j], attn_w_in[j], gla_w_a2[j], gla_b_a[j], T, 512)
            oa = _gla(aq, ak, av, la, ar, gla_head_g[j], B, T, 512)
            ob = _dsa(bq, iq, misct, kk, v1t, B, T, 128, 512, topk)
            x2 = _out_proj(x2, oa, ob, attn_w_o[j], 512)
        else:
            x2 = _sgu(x2, sgu_norm[j], sgu_w_uv[j], sgu_ln_g[j], sgu_ln_b[j], sgu_w_s[j], sgu_b_s[j],
                      sgu_w_out[j], 512)
        x2 = _conv_ffn(x2, ffn_norm[i], ffn_w_up[i], ffn_conv_w[i], ffn_conv_b[i], ffn_w_down[i],
                       final_norm, T, 512, i == depth - 1)
    return x2.reshape(B, T, D)
```

```python
import functools

import jax
import jax.numpy as jnp
from jax import lax
from jax.experimental import pallas as pl
from jax.experimental.pallas import tpu as pltpu

D_MODEL = 1024
GLA_HEADS = 4
GLA_DK = 64
GLA_DV = 128
GLA_GATE_RANK = 16
GLA_GATE_TAU = 16.0
GLA_CHUNK = 64
DSA_HEADS = 8
DSA_HD = 64
IDX_HEADS = 4
IDX_HD = 64
TOPK_MAX = 256
ROPE_THETA = 500000.0
ROPE_FRAC_DIV = 4
SGU_CHUNK = 128
SGU_GROUPS = 8
D_FF = 2816
CONV_W = 3
EPS = 1e-6
LN_EPS = 1e-5

LANES = 128
SUBLANES = 8
VMEM_LIMIT = 56 * 1024 * 1024

_SEG_AQ = (0, 256)
_SEG_AK = (256, 512)
_SEG_AV = (512, 1024)
_SEG_AR = (1024, 1536)
_SEG_BQ = (1536, 2048)
_SEG_IQ = (2048, 2304)
_SEG_KK = (2304, 2432)
_SEG_MISC = (2432, 2560)
IN_PAD = 2560
MISC_ALR = 64
MISC_IW = 80

F32 = jnp.float32
BF16 = jnp.bfloat16
NEG_BIG = -1e30
LOG2E = 1.4426950408889634


def _dot(a, b):
    return jnp.dot(a, b, preferred_element_type=F32)


def _dot_nt(a, b):
    return lax.dot_general(a, b, (((1,), (1,)), ((), ())), preferred_element_type=F32)


def _dot_tn(a, b):
    return lax.dot_general(a, b, (((0,), (0,)), ((), ())), preferred_element_type=F32)


def _rmsnorm_rows(x, g):
    ms = jnp.mean(x * x, axis=-1, keepdims=True)
    return x * lax.rsqrt(ms + EPS) * g


def _params(*sem):
    return pltpu.CompilerParams(dimension_semantics=sem, vmem_limit_bytes=VMEM_LIMIT)


def _rope_slab(x, tab):
    half = DSA_HD // ROPE_FRAC_DIV // 2
    c = tab[:, 0:LANES]
    s_up = tab[:, LANES:2 * LANES]
    s_dn = tab[:, 2 * LANES:3 * LANES]
    return x * c + pltpu.roll(x, half, 1) * s_up + pltpu.roll(x, LANES - half, 1) * s_dn


def _in_proj_kernel(x_ref, g_ref, w_ref, wa2_ref, ba_ref, tab2_ref,
                    aq_ref, ak_ref, av_ref, ar_ref, la_ref, bq_ref, iq_ref, kk_ref, v1t_ref, misct_ref):
    h = _rmsnorm_rows(x_ref[...], g_ref[...]).astype(BF16)

    def seg(s):
        return _dot(h, w_ref[:, s[0]:s[1]])

    aq_ref[...] = (seg(_SEG_AQ) * (GLA_DK ** -0.5)).astype(aq_ref.dtype)
    ak_ref[...] = seg(_SEG_AK).astype(ak_ref.dtype)
    av_ref[...] = seg(_SEG_AV).astype(av_ref.dtype)
    ar_ref[...] = seg(_SEG_AR).astype(ar_ref.dtype)

    tab2 = tab2_ref[...]
    bq = seg(_SEG_BQ)
    for j in range((_SEG_BQ[1] - _SEG_BQ[0]) // LANES):
        sl = slice(j * LANES, (j + 1) * LANES)
        bq_ref[:, sl] = (_rope_slab(bq[:, sl], tab2) * (DSA_HD ** -0.5 * LOG2E)).astype(bq_ref.dtype)
    iq = seg(_SEG_IQ)
    for j in range((_SEG_IQ[1] - _SEG_IQ[0]) // LANES):
        sl = slice(j * LANES, (j + 1) * LANES)
        iq_ref[:, sl] = (_rope_slab(iq[:, sl], tab2) * (IDX_HD ** -0.5)).astype(iq_ref.dtype)
    kk_ref[...] = _rope_slab(seg(_SEG_KK), tab2).astype(kk_ref.dtype)
    misc = seg(_SEG_MISC)
    misct = misc.T
    misct_ref[...] = misct
    row = lax.broadcasted_iota(jnp.int32, misct.shape, 0)
    v1t = jnp.where(row < DSA_HD, misct, jnp.where(row == DSA_HD, 1.0, 0.0))
    v1t_ref[...] = v1t.astype(v1t_ref.dtype)
    z = _dot(misc.astype(BF16), wa2_ref[...]) + ba_ref[...]
    la_ref[...] = (jnp.minimum(z, 0.0) - jnp.log(1.0 + jnp.exp(-jnp.abs(z)))) * (1.0 / GLA_GATE_TAU)


def _rope_tables(T):
    rd = DSA_HD // ROPE_FRAC_DIV
    half = rd // 2
    pos = jnp.arange(T, dtype=F32)
    inv = jnp.power(ROPE_THETA, -(jnp.arange(half, dtype=F32) * 2.0 / rd))
    ang = pos[:, None] * inv[None, :]
    cos, sin = jnp.cos(ang), jnp.sin(ang)
    z = jnp.zeros((T, DSA_HD - rd), F32)
    zh = jnp.zeros((T, half), F32)
    c64 = jnp.concatenate([cos, cos, z + 1.0], axis=1)
    up64 = jnp.concatenate([zh, sin, z], axis=1)
    dn64 = jnp.concatenate([-sin, zh, z], axis=1)
    return jnp.concatenate([c64, c64, up64, up64, dn64, dn64], axis=1)


def _in_proj(x2, g, w_in, w_a2, b_a, T, tm):
    N = x2.shape[0]
    widths = (256, 256, 512, 512, 16, 512, 64, 64, 256, 64, 4)
    offs = [0]
    for w in widths:
        offs.append(offs[-1] + w)
    aq, ak, av, ar, alr, bq, bk, bv, iq, ik, iw = [w_in[:, offs[i]:offs[i + 1]] for i in range(11)]
    pad = jnp.zeros((D_MODEL, LANES - IDX_HD - GLA_GATE_RANK - IDX_HEADS), w_in.dtype)
    wp = jnp.concatenate([aq, ak, av, ar, bq, iq, bk, ik, bv, alr, iw * (IDX_HEADS ** -0.5), pad],
                         axis=1).astype(BF16)
    wa2 = jnp.zeros((LANES, GLA_HEADS * GLA_DK), F32).at[MISC_ALR:MISC_ALR + GLA_GATE_RANK].set(w_a2).astype(BF16)
    tab2 = _rope_tables(T)
    nt = T // tm
    row = lambda i: (i, 0)
    fixed = lambda i: (0, 0)
    tabm = lambda i: (i % nt, 0)
    outs = [(256, BF16), (256, BF16), (512, BF16), (512, BF16), (256, F32),
            (512, BF16), (256, BF16), (128, BF16)]
    col = lambda i: (0, i)
    return pl.pallas_call(
        _in_proj_kernel,
        grid=(N // tm,),
        in_specs=[
            pl.BlockSpec((tm, D_MODEL), row),
            pl.BlockSpec((1, D_MODEL), fixed),
            pl.BlockSpec((D_MODEL, IN_PAD), fixed),
            pl.BlockSpec((LANES, GLA_HEADS * GLA_DK), fixed),
            pl.BlockSpec((1, GLA_HEADS * GLA_DK), fixed),
            pl.BlockSpec((tm, 3 * LANES), tabm),
        ],
        out_specs=[pl.BlockSpec((tm, w), row) for w, _ in outs]
        + [pl.BlockSpec((LANES, tm), col), pl.BlockSpec((LANES, tm), col)],
        out_shape=[jax.ShapeDtypeStruct((N, w), d) for w, d in outs]
        + [jax.ShapeDtypeStruct((LANES, N), BF16), jax.ShapeDtypeStruct((LANES, N), F32)],
        compiler_params=_params("parallel"),
        name="in_proj",
    )(x2, g.reshape(1, -1), wp, wa2, b_a.reshape(1, -1), tab2)


def _gla_kernel(q_ref, k_ref, v_ref, la_ref, r_ref, hg_ref, o_ref, st_ref, *, n_chunks):
    C = GLA_CHUNK

    @pl.when(pl.program_id(1) == 0)
    def _():
        st_ref[...] = jnp.zeros_like(st_ref)

    ri = lax.broadcasted_iota(jnp.int32, (C, C), 0)
    ci = lax.broadcasted_iota(jnp.int32, (C, C), 1)
    tril = ri >= ci
    row_id = lax.broadcasted_iota(jnp.int32, (C, GLA_HEADS * GLA_DK), 0)
    hg = hg_ref[...]
    state = [st_ref[hh] for hh in range(GLA_HEADS)]

    for c in range(n_chunks):
        rows = slice(c * C, (c + 1) * C)
        la = la_ref[rows, :]
        b = la
        for sh in (1, 2, 4, 8, 16, 32):
            b = b + jnp.where(row_id >= sh, pltpu.roll(b, sh, 0), 0.0)
        b_mid = b[C // 2:C // 2 + 1, :]
        b_last = b[C - 1:C, :]
        q = q_ref[rows, :].astype(F32)
        k = k_ref[rows, :].astype(F32)
        qe = (q * jnp.exp(b - b_mid)).astype(BF16)
        ke = (k * jnp.exp(b_mid - b)).astype(BF16)
        kl = (k * jnp.exp(b_last - b)).astype(BF16)
        qb = (q * jnp.exp(b)).astype(BF16)
        dec = jnp.exp(b_last)
        for hh in range(GLA_HEADS):
            ks = slice(hh * GLA_DK, (hh + 1) * GLA_DK)
            vs = slice(hh * GLA_DV, (hh + 1) * GLA_DV)
            v = v_ref[rows, vs]
            att = jnp.where(tril, _dot_nt(qe[:, ks], ke[:, ks]), 0.0)
            o = _dot(att.astype(BF16), v) + _dot_nt(qb[:, ks], state[hh].astype(BF16))
            state[hh] = state[hh] * dec[:, ks] + _dot_tn(v, kl[:, ks])
            o = o * lax.rsqrt(jnp.mean(o * o, axis=-1, keepdims=True) + EPS) * hg
            r = r_ref[rows, vs].astype(F32)
            o_ref[rows, vs] = (o * (r * jax.nn.sigmoid(r))).astype(o_ref.dtype)
    for hh in range(GLA_HEADS):
        st_ref[hh] = state[hh]


def _gla(aq, ak, av, la, ar, head_g, B, T, tg):
    N = B * T
    nt = T // tg
    row = lambda b, i: (b * nt + i, 0)
    return pl.pallas_call(
        functools.partial(_gla_kernel, n_chunks=tg // GLA_CHUNK),
        grid=(B, nt),
        in_specs=[
            pl.BlockSpec((tg, 256), row), pl.BlockSpec((tg, 256), row), pl.BlockSpec((tg, 512), row),
            pl.BlockSpec((tg, 256), row), pl.BlockSpec((tg, 512), row),
            pl.BlockSpec((1, GLA_DV), lambda b, i: (0, 0)),
        ],
        out_specs=pl.BlockSpec((tg, 512), row),
        out_shape=jax.ShapeDtypeStruct((N, 512), BF16),
        scratch_shapes=[pltpu.VMEM((GLA_HEADS, GLA_DV, GLA_DK), F32)],
        compiler_params=_params("parallel", "arbitrary"),
        name="gla",
    )(aq, ak, av, la, ar, head_g.reshape(1, -1))


DSA_GROUP_LANES = 1024
MIN_NORMAL = 1.1754943508222875e-38
SEARCH_ROUND = 3
SEARCH_WARMUP_ROUNDS = 4
SEARCH_INTERP_ROUNDS = 12
SEARCH_BISECT_ROUNDS = 100


def _dsa_kernel(bq_ref, iq_ref, qmisct_ref, kk_ref, v1t_ref, tri_ref, o_ref,
                sc_ref, s_ref, qs_ref, acc_ref, *, tq, tk, topk):
    qi = pl.program_id(1)
    q0 = qi * tq
    nkb = (q0 + tq + tk - 1) // tk
    qpos = q0 + lax.broadcasted_iota(jnp.int32, (1, tq), 1)
    key_iota = lax.broadcasted_iota(jnp.int32, (tk, tq), 0)
    H = DSA_HEADS
    S = SUBLANES

    def krows(kb):
        return pl.ds(pl.multiple_of(kb * tk, tk), tk)

    def fold_rows(x, op):
        parts = [x[j * S:(j + 1) * S, :] for j in range(x.shape[0] // S)]
        while len(parts) > 1:
            parts = [op(parts[j], parts[j + 1]) for j in range(0, len(parts) - 1, 2)] + (
                [parts[-1]] if len(parts) % 2 else [])
        return parts[0]

    iw = [qmisct_ref[MISC_IW + h:MISC_IW + h + 1, :] for h in range(IDX_HEADS)]
    for h in range(IDX_HEADS):
        qs_ref[h * tq:(h + 1) * tq, :] = iq_ref[:, h * IDX_HD:(h + 1) * IDX_HD]

    def score_block(kb, mm):
        x = _dot_nt(kk_ref[krows(kb), IDX_HD:2 * IDX_HD], qs_ref[0:IDX_HEADS * tq, :])
        sc = iw[0] * jnp.maximum(x[:, 0:tq], 0.0)
        for h in range(1, IDX_HEADS):
            sc = sc + iw[h] * jnp.maximum(x[:, h * tq:(h + 1) * tq], 0.0)
        sc = sc + 0.0
        sc_ref[kb] = jnp.where(kb * tk + key_iota <= qpos, sc, -jnp.inf)
        return jnp.maximum(mm[0], fold_rows(sc, jnp.maximum)), jnp.minimum(mm[1], fold_rows(sc, jnp.minimum))

    mx8, mn8 = lax.fori_loop(0, nkb, score_block,
                             (jnp.full((S, tq), -jnp.inf, F32), jnp.full((S, tq), jnp.inf, F32)))
    hi_bound = jnp.max(mx8, axis=0, keepdims=True)
    lo_bound = jnp.min(mn8, axis=0, keepdims=True)

    def count_ge(cand):
        def body(kb, cnt):
            return cnt + fold_rows(jnp.where(sc_ref[kb] >= cand, 1, 0), jnp.add)
        cnt = lax.fori_loop(0, nkb, body, jnp.zeros((S, tq), jnp.int32))
        return jnp.sum(cnt, axis=0, keepdims=True)

    def midpoint(lo, hi):
        return 0.5 * lo + 0.5 * hi

    def finished(lo, hi, c_lo):
        mid = midpoint(lo, hi)
        closed = jnp.where(mid <= lo, 1, jnp.where(mid >= hi, 1, 0))
        closed = jnp.where(lo == 0.0, jnp.where(hi == MIN_NORMAL, 1, closed), closed)
        return jnp.where(c_lo == topk, 1, closed)

    def step(st, pick):
        lo, hi, c_lo, c_hi, g_lo, g_hi, last = st
        done = finished(lo, hi, c_lo)
        cand = pick(lo, hi, g_lo, g_hi)
        cand = jnp.where(cand > lo, jnp.where(cand < hi, cand, midpoint(lo, hi)), midpoint(lo, hi))
        cand = jnp.where(done > 0, lo, cand)
        c = count_ge(cand)
        g = c.astype(F32) - (topk - 0.5)
        up = jnp.where(done > 0, 0, jnp.where(c >= topk, 1, 0))
        dn = jnp.where(done > 0, 0, jnp.where(c >= topk, 0, 1))
        g_hi = jnp.where(up * last > 0, g_hi * 0.5, g_hi)
        g_lo = jnp.where(dn * last < 0, g_lo * 0.5, g_lo)
        return (jnp.where(up > 0, cand, lo), jnp.where(dn > 0, cand, hi),
                jnp.where(up > 0, c, c_lo), jnp.where(dn > 0, c, c_hi),
                jnp.where(up > 0, g, g_lo), jnp.where(dn > 0, g, g_hi),
                jnp.where(up > 0, 1, jnp.where(dn > 0, -1, last)))

    def interpolate(lo, hi, g_lo, g_hi):
        return lo + (hi - lo) * (g_lo / (g_lo - g_hi))

    def bisect(lo, hi, g_lo, g_hi):
        return midpoint(lo, hi)

    def unfinished(st):
        return jnp.max(jnp.where(finished(st[0], st[1], st[2]) > 0, 0.0, 1.0)) > 0.0

    def one_round(st, pick):
        for _ in range(SEARCH_ROUND):
            st = step(st, pick)
        return st

    def search(st, pick, max_rounds):
        return lax.while_loop(lambda c: jnp.logical_and(c[0] < max_rounds, unfinished(c[1])),
                              lambda c: (c[0] + 1, one_round(c[1], pick)), (jnp.int32(0), st))[1]

    n_causal = qpos + 1
    small = n_causal <= topk
    lo0 = jnp.where(small, -jnp.inf, lo_bound)
    hi0 = hi_bound + jnp.maximum(jnp.abs(hi_bound) * 1e-6, 1e-37)
    c_lo0 = jnp.where(small, topk, n_causal)
    st = (lo0, hi0, c_lo0, jnp.zeros((1, tq), jnp.int32), c_lo0.astype(F32) - (topk - 0.5),
          jnp.full((1, tq), 0.5 - topk, F32), jnp.zeros((1, tq), jnp.int32))
    st = step(st, lambda lo, hi, g_lo, g_hi: jnp.zeros_like(lo))
    st = step(st, lambda lo, hi, g_lo, g_hi: jnp.full_like(lo, MIN_NORMAL))
    st = lax.fori_loop(0, SEARCH_WARMUP_ROUNDS, lambda _, st: one_round(st, interpolate), st)
    st = search(st, interpolate, SEARCH_INTERP_ROUNDS)
    st = search(st, bisect, SEARCH_BISECT_ROUNDS)
    thr, c_lo, c_hi = st[0], st[2], st[3]
    need = jnp.where(small, 0, jnp.where(c_lo == topk, topk, topk - c_hi)).astype(F32)

    tri = tri_ref[...]
    HG = s_ref.shape[2] // tq
    for grp in range(H // HG):
        for j in range(HG):
            h = grp * HG + j
            qs_ref[j * tq:(j + 1) * tq, :] = bq_ref[:, h * DSA_HD:(h + 1) * DSA_HD]

        def qk_block(kb, carry, first=(grp == 0)):
            mx, base = carry
            sc = sc_ref[kb]
            if first:
                eq = jnp.where(sc == thr, 1.0, 0.0).astype(BF16)
                rank = _dot(tri, eq) + base
                bias = jnp.where(sc > thr, 0.0,
                                 jnp.where(sc == thr, jnp.where(rank <= need, 0.0, NEG_BIG), NEG_BIG))
                sc_ref[kb] = bias
                base = rank[tk - 1:tk, :]
            else:
                bias = sc
            s = _dot_nt(kk_ref[krows(kb), 0:DSA_HD], qs_ref[0:HG * tq, :])
            s = s + jnp.concatenate([bias] * HG, axis=1)
            s_ref[kb] = s
            return jnp.maximum(mx, fold_rows(s, jnp.maximum)), base

        mx, _ = lax.fori_loop(0, nkb, qk_block,
                              (jnp.full((S, HG * tq), NEG_BIG, F32), jnp.zeros((1, tq), F32)))
        m = jnp.max(mx, axis=0, keepdims=True)
        acc_ref[...] = jnp.zeros(acc_ref.shape, F32)

        def pv_block(kb, carry, m=m):
            p = jnp.exp2(s_ref[kb] - m).astype(BF16)
            acc_ref[...] += _dot(v1t_ref[:, krows(kb)], p)
            return carry

        lax.fori_loop(0, nkb, pv_block, 0)
        acc = acc_ref[...]
        out = acc[0:DSA_HD, :] / acc[DSA_HD:DSA_HD + 1, :]
        for j in range(HG):
            h = grp * HG + j
            o_ref[:, h * DSA_HD:(h + 1) * DSA_HD] = out[:, j * tq:(j + 1) * tq].T.astype(o_ref.dtype)


def _dsa(bq, iq, misct, kk, v1t, B, T, tq, tk, topk):
    N = B * T
    nq = T // tq
    qrow = lambda b, i: (b * nq + i, 0)
    r = lax.broadcasted_iota(jnp.int32, (tk, tk), 0)
    c = lax.broadcasted_iota(jnp.int32, (tk, tk), 1)
    tri = jnp.where(c <= r, 1.0, 0.0).astype(BF16)
    return pl.pallas_call(
        functools.partial(_dsa_kernel, tq=tq, tk=tk, topk=topk),
        grid=(B, nq),
        in_specs=[
            pl.BlockSpec((tq, 512), qrow), pl.BlockSpec((tq, 256), qrow),
            pl.BlockSpec((LANES, tq), lambda b, i: (0, b * nq + i)),
            pl.BlockSpec((T, LANES), lambda b, i: (b, 0)),
            pl.BlockSpec((LANES, T), lambda b, i: (0, b)),
            pl.BlockSpec((tk, tk), lambda b, i: (0, 0)),
        ],
        out_specs=pl.BlockSpec((tq, 512), qrow),
        out_shape=jax.ShapeDtypeStruct((N, 512), BF16),
        scratch_shapes=[
            pltpu.VMEM((T // tk, tk, tq), F32),
            pltpu.VMEM((T // tk, tk, DSA_GROUP_LANES), F32),
            pltpu.VMEM((max(DSA_GROUP_LANES, IDX_HEADS * tq), DSA_HD), BF16),
            pltpu.VMEM((LANES, DSA_GROUP_LANES), F32),
        ],
        compiler_params=_params("parallel", "arbitrary"),
        name="dsa",
    )(bq, iq, misct, kk, v1t, tri)


def _out_proj_kernel(x_ref, oa_ref, ob_ref, wa_ref, wb_ref, o_ref):
    o_ref[...] = x_ref[...] + _dot(oa_ref[...], wa_ref[...]) + _dot(ob_ref[...], wb_ref[...])


def _out_proj(x2, oa, ob, w_o, tm):
    N = x2.shape[0]
    wa = w_o[:512].astype(BF16)
    wb = w_o[512:].astype(BF16)
    row = lambda i: (i, 0)
    fixed = lambda i: (0, 0)
    return pl.pallas_call(
        _out_proj_kernel,
        grid=(N // tm,),
        in_specs=[pl.BlockSpec((tm, D_MODEL), row), pl.BlockSpec((tm, 512), row), pl.BlockSpec((tm, 512), row),
                  pl.BlockSpec((512, D_MODEL), fixed), pl.BlockSpec((512, D_MODEL), fixed)],
        out_specs=pl.BlockSpec((tm, D_MODEL), row),
        out_shape=jax.ShapeDtypeStruct((N, D_MODEL), F32),
        compiler_params=_params("parallel"),
        name="out_proj",
    )(x2, oa, ob, wa, wb)


FFN_CHUNKS = (768, 768, 768, 512)


def _ffn_kernel(x_ref, g_ref, wup_ref, cw_ref, cb_ref, wd_ref, fg_ref, o_ref, h_ref, act_ref, acc_ref, *ab_refs,
                tm, tiles_per_seq, final_norm, rb):
    i = pl.program_id(0)
    H = SUBLANES
    n_c = len(FFN_CHUNKS)
    ag_refs, au_refs = ab_refs[:n_c], ab_refs[n_c:]
    offs = [sum(FFN_CHUNKS[:c]) for c in range(n_c)]
    first = (i % tiles_per_seq) == 0

    @pl.when(first)
    def _():
        for ref in ab_refs:
            ref[0:H, :] = jnp.zeros((H, ref.shape[1]), F32)

    @pl.when(jnp.logical_not(first))
    def _():
        for ref in ab_refs:
            ref[0:H, :] = ref[tm:tm + H, :]

    def up_proj(c):
        o, w = offs[c], FFN_CHUNKS[c]
        ag_refs[c][H:H + tm, :] = _dot(hb, wup_ref[:, o:o + w])
        au_refs[c][H:H + tm, :] = _dot(hb, wup_ref[:, D_FF + o:D_FF + o + w])

    def gate_act(c):
        o, w = offs[c], FFN_CHUNKS[c]
        cwg, cwu = cw_ref[:, o:o + w], cw_ref[:, D_FF + o:D_FF + o + w]
        cbg, cbu = cb_ref[:, o:o + w], cb_ref[:, D_FF + o:D_FF + o + w]
        for r0 in range(0, tm, rb):
            wg = ag_refs[c][r0:r0 + rb + H, :]
            wu = au_refs[c][r0:r0 + rb + H, :]
            gate, up = cbg, cbu
            for j in range(CONV_W):
                s0 = H - (CONV_W - 1) + j
                gate = gate + cwg[j:j + 1, :] * wg[s0:s0 + rb, :]
                up = up + cwu[j:j + 1, :] * wu[s0:s0 + rb, :]
            act_ref[r0:r0 + rb, o:o + w] = (gate * jax.nn.sigmoid(gate) * up).astype(BF16)

    def down_proj(c):
        o, w = offs[c], FFN_CHUNKS[c]
        d = _dot(act_ref[:, o:o + w], wd_ref[o:o + w, :])
        if c == 0:
            acc_ref[...] = d
        else:
            acc_ref[...] += d

    h_ref[...] = _rmsnorm_rows(x_ref[...], g_ref[...]).astype(BF16)
    hb = h_ref[...]
    up_proj(0)
    for c in range(n_c):
        if c + 1 < n_c:
            up_proj(c + 1)
        gate_act(c)
        down_proj(c)
    y = x_ref[...] + acc_ref[...]
    if final_norm:
        y = _rmsnorm_rows(y, fg_ref[...])
    o_ref[...] = y


def _conv_ffn(x2, g, w_up, conv_w, conv_b, w_down, final_g, T, tm, final_norm):
    N = x2.shape[0]
    row = lambda i: (i, 0)
    fixed = lambda i: (0, 0)
    def wspec(shape):
        return pl.BlockSpec(shape, fixed, pipeline_mode=pl.Buffered(1))
    return pl.pallas_call(
        functools.partial(_ffn_kernel, tm=tm, tiles_per_seq=T // tm, final_norm=final_norm, rb=64),
        grid=(N // tm,),
        in_specs=[
            pl.BlockSpec((tm, D_MODEL), row),
            pl.BlockSpec((1, D_MODEL), fixed),
            wspec((D_MODEL, 2 * D_FF)),
            pl.BlockSpec((CONV_W, 2 * D_FF), fixed),
            pl.BlockSpec((1, 2 * D_FF), fixed),
            wspec((D_FF, D_MODEL)),
            pl.BlockSpec((1, D_MODEL), fixed),
        ],
        out_specs=pl.BlockSpec((tm, D_MODEL), row),
        out_shape=jax.ShapeDtypeStruct((N, D_MODEL), F32),
        scratch_shapes=[
            pltpu.VMEM((tm, D_MODEL), BF16),
            pltpu.VMEM((tm, D_FF), BF16),
            pltpu.VMEM((tm, D_MODEL), F32),
        ] + [pltpu.VMEM((tm + SUBLANES, w), F32) for w in FFN_CHUNKS] * 2,
        compiler_params=_params("arbitrary"),
        name="conv_ffn",
    )(x2, g.reshape(1, -1), w_up.astype(BF16), conv_w, conv_b.reshape(1, -1), w_down.astype(BF16),
      final_g.reshape(1, -1))


def _sgu_kernel(x_ref, g_ref, wuv_ref, lng_ref, lnb_ref, ws_ref, bs_ref, wo_ref, o_ref,
                u_ref, v_ref, gated_ref, *, tm):
    W = D_MODEL
    gw = W // SGU_GROUPS
    h = _rmsnorm_rows(x_ref[...], g_ref[...]).astype(BF16)
    u_ref[...] = jax.nn.gelu(_dot(h, wuv_ref[:, 0:W]))
    v = jax.nn.gelu(_dot(h, wuv_ref[:, W:2 * W]))
    mu = jnp.mean(v, axis=-1, keepdims=True)
    vc = v - mu
    v = vc * lax.rsqrt(jnp.mean(vc * vc, axis=-1, keepdims=True) + LN_EPS) * lng_ref[...] + lnb_ref[...]
    v_ref[...] = v.astype(BF16)
    ri = lax.broadcasted_iota(jnp.int32, (SGU_CHUNK, SGU_CHUNK), 0)
    ci = lax.broadcasted_iota(jnp.int32, (SGU_CHUNK, SGU_CHUNK), 1)
    tril = ri >= ci
    for gi in range(SGU_GROUPS):
        cs = slice(gi * gw, (gi + 1) * gw)
        ws = jnp.where(tril, ws_ref[gi], 0.0).astype(BF16)
        bias = bs_ref[:, gi:gi + 1]
        for n in range(tm // SGU_CHUNK):
            rs = slice(n * SGU_CHUNK, (n + 1) * SGU_CHUNK)
            mixed = _dot(ws, v_ref[rs, cs]) + bias
            gated_ref[rs, cs] = (u_ref[rs, cs] * mixed).astype(BF16)
    o_ref[...] = x_ref[...] + _dot(gated_ref[...], wo_ref[...])


def _sgu(x2, g, w_uv, ln_g, ln_b, w_s, b_s, w_out, tm):
    N = x2.shape[0]
    row = lambda i: (i, 0)
    fixed = lambda i: (0, 0)
    return pl.pallas_call(
        functools.partial(_sgu_kernel, tm=tm),
        grid=(N // tm,),
        in_specs=[
            pl.BlockSpec((tm, D_MODEL), row),
            pl.BlockSpec((1, D_MODEL), fixed),
            pl.BlockSpec((D_MODEL, 2 * D_MODEL), fixed),
            pl.BlockSpec((1, D_MODEL), fixed), pl.BlockSpec((1, D_MODEL), fixed),
            pl.BlockSpec((SGU_GROUPS, SGU_CHUNK, SGU_CHUNK), lambda i: (0, 0, 0)),
            pl.BlockSpec((SGU_CHUNK, SGU_GROUPS), fixed),
            pl.BlockSpec((D_MODEL, D_MODEL), fixed),
        ],
        out_specs=pl.BlockSpec((tm, D_MODEL), row),
        out_shape=jax.ShapeDtypeStruct((N, D_MODEL), F32),
        scratch_shapes=[pltpu.VMEM((tm, D_MODEL), F32), pltpu.VMEM((tm, D_MODEL), BF16),
                        pltpu.VMEM((tm, D_MODEL), BF16)],
        compiler_params=_params("parallel"),
        name="sgu",
    )(x2, g.reshape(1, -1), w_uv.astype(BF16), ln_g.reshape(1, -1), ln_b.reshape(1, -1),
      w_s, b_s.T, w_out.astype(BF16))


def kernel(x, attn_norm, attn_w_in, gla_w_a2, gla_b_a, gla_head_g, attn_w_o, sgu_norm, sgu_w_uv, sgu_ln_g,
           sgu_ln_b, sgu_w_s, sgu_b_s, sgu_w_out, ffn_norm, ffn_w_up, ffn_conv_w, ffn_conv_b, ffn_w_down,
           final_norm):
    B, T, D = x.shape
    assert D == D_MODEL and T % 512 == 0
    topk = min(TOPK_MAX, T // 4)
    depth = ffn_norm.shape[0]
    x2 = x.reshape(B * T, D)
    for i in range(depth):
        j = i // 2
        if i % 2 == 0:
            aq, ak, av, ar, la, bq, iq, kk, v1t, misct = _in_proj(
                x2, attn_norm[j], attn_w_in[j], gla_w_a2[j], gla_b_a[j], T, 512)
            oa = _gla(aq, ak, av, la, ar, gla_head_g[j], B, T, 512)
            ob = _dsa(bq, iq, misct, kk, v1t, B, T, 256, 512, topk)
            x2 = _out_proj(x2, oa, ob, attn_w_o[j], 512)
        else:
            x2 = _sgu(x2, sgu_norm[j], sgu_w_uv[j], sgu_ln_g[j], sgu_ln_b[j], sgu_w_s[j], sgu_b_s[j],
                      sgu_w_out[j], 512)
        x2 = _conv_ffn(x2, ffn_norm[i], ffn_w_up[i], ffn_conv_w[i], ffn_conv_b[i], ffn_w_down[i],
                       final_norm, T, 512, i == depth - 1)
    return x2.reshape(B, T, D)
```

```python
import functools

import jax
import jax.numpy as jnp
import numpy as np
from jax import lax
from jax.experimental import pallas as pl
from jax.experimental.pallas import tpu as pltpu

D_MODEL = 1024
GLA_HEADS = 4
GLA_DK = 64
GLA_DV = 128
GLA_GATE_RANK = 16
GLA_GATE_TAU = 16.0
GLA_CHUNK = 64
DSA_HEADS = 8
DSA_HD = 64
IDX_HEADS = 4
IDX_HD = 64
TOPK_MAX = 256
ROPE_THETA = 500000.0
ROPE_FRAC_DIV = 4
SGU_CHUNK = 128
SGU_GROUPS = 8
D_FF = 2816
CONV_W = 3
EPS = 1e-6
LN_EPS = 1e-5

LANES = 128
SUBLANES = 8
VMEM_LIMIT = 56 * 1024 * 1024

_SEG_AQ = (0, 256)
_SEG_AK = (256, 512)
_SEG_AV = (512, 1024)
_SEG_AR = (1024, 1536)
_SEG_BQ = (1536, 2048)
_SEG_IQ = (2048, 2304)
_SEG_KK = (2304, 2432)
_SEG_MISC = (2432, 2560)
IN_PAD = 2560
MISC_ALR = 64
MISC_IW = 80

F32 = jnp.float32
BF16 = jnp.bfloat16
NEG_BIG = -1e30
LOG2E = 1.4426950408889634


def _dot(a, b):
    return jnp.dot(a, b, preferred_element_type=F32)


def _dot_nt(a, b):
    return lax.dot_general(a, b, (((1,), (1,)), ((), ())), preferred_element_type=F32)


def _dot_tn(a, b):
    return lax.dot_general(a, b, (((0,), (0,)), ((), ())), preferred_element_type=F32)


def _rmsnorm_rows(x, g):
    ms = jnp.mean(x * x, axis=-1, keepdims=True)
    return x * lax.rsqrt(ms + EPS) * g


def _params(*sem):
    return pltpu.CompilerParams(dimension_semantics=sem, vmem_limit_bytes=VMEM_LIMIT)


def _rope_slab(x, tab):
    half = DSA_HD // ROPE_FRAC_DIV // 2
    c = tab[:, 0:LANES]
    s_up = tab[:, LANES:2 * LANES]
    s_dn = tab[:, 2 * LANES:3 * LANES]
    return x * c + pltpu.roll(x, half, 1) * s_up + pltpu.roll(x, LANES - half, 1) * s_dn


def _in_proj_kernel(x_ref, g_ref, w_ref, wa2_ref, ba_ref, tab2_ref,
                    aq_ref, ak_ref, av_ref, ar_ref, la_ref, bq_ref, iq_ref, kk_ref, v1t_ref, misct_ref):
    h = _rmsnorm_rows(x_ref[...], g_ref[...]).astype(BF16)

    def seg(s):
        return _dot(h, w_ref[:, s[0]:s[1]])

    aq_ref[...] = (seg(_SEG_AQ) * (GLA_DK ** -0.5)).astype(aq_ref.dtype)
    ak_ref[...] = seg(_SEG_AK).astype(ak_ref.dtype)
    av_ref[...] = seg(_SEG_AV).astype(av_ref.dtype)
    ar_ref[...] = seg(_SEG_AR).astype(ar_ref.dtype)

    tab2 = tab2_ref[...]
    bq = seg(_SEG_BQ)
    for j in range((_SEG_BQ[1] - _SEG_BQ[0]) // LANES):
        sl = slice(j * LANES, (j + 1) * LANES)
        bq_ref[:, sl] = (_rope_slab(bq[:, sl], tab2) * (DSA_HD ** -0.5 * LOG2E)).astype(bq_ref.dtype)
    iq = seg(_SEG_IQ)
    for j in range((_SEG_IQ[1] - _SEG_IQ[0]) // LANES):
        sl = slice(j * LANES, (j + 1) * LANES)
        iq_ref[:, sl] = (_rope_slab(iq[:, sl], tab2) * (IDX_HD ** -0.5)).astype(iq_ref.dtype)
    kk_ref[...] = _rope_slab(seg(_SEG_KK), tab2).astype(kk_ref.dtype)
    misc = seg(_SEG_MISC)
    misct = misc.T
    misct_ref[...] = misct
    row = lax.broadcasted_iota(jnp.int32, misct.shape, 0)
    v1t = jnp.where(row < DSA_HD, misct, jnp.where(row == DSA_HD, 1.0, 0.0))
    v1t_ref[...] = v1t.astype(v1t_ref.dtype)
    z = _dot(misc.astype(BF16), wa2_ref[...]) + ba_ref[...]
    la_ref[...] = (jnp.minimum(z, 0.0) - jnp.log(1.0 + jnp.exp(-jnp.abs(z)))) * (1.0 / GLA_GATE_TAU)


def _rope_tables(T):
    rd = DSA_HD // ROPE_FRAC_DIV
    half = rd // 2
    pos = jnp.arange(T, dtype=F32)
    inv = jnp.power(ROPE_THETA, -(jnp.arange(half, dtype=F32) * 2.0 / rd))
    ang = pos[:, None] * inv[None, :]
    cs = jnp.concatenate([jnp.cos(ang), jnp.sin(ang)], axis=1)
    sel = np.zeros((2 * half, 3 * LANES), np.float32)
    one = np.zeros((1, 3 * LANES), np.float32)
    for head in range(LANES // DSA_HD):
        o = head * DSA_HD
        one[0, o + rd:o + DSA_HD] = 1.0
        for j in range(half):
            sel[j, o + j] = 1.0
            sel[j, o + half + j] = 1.0
            sel[half + j, LANES + o + half + j] = 1.0
            sel[half + j, 2 * LANES + o + j] = -1.0
    return jnp.dot(cs, jnp.asarray(sel), precision=lax.Precision.HIGHEST) + jnp.asarray(one)


def _in_proj(x2, g, w_in, w_a2, b_a, T, tm):
    N = x2.shape[0]
    widths = (256, 256, 512, 512, 16, 512, 64, 64, 256, 64, 4)
    offs = [0]
    for w in widths:
        offs.append(offs[-1] + w)
    w_in = w_in.astype(BF16)
    aq, ak, av, ar, alr, bq, bk, bv, iq, ik, iw = [w_in[:, offs[i]:offs[i + 1]] for i in range(11)]
    pad = jnp.zeros((D_MODEL, LANES - IDX_HD - GLA_GATE_RANK - IDX_HEADS), w_in.dtype)
    wp = jnp.concatenate([aq, ak, av, ar, bq, iq, bk, ik, bv, alr, iw * (IDX_HEADS ** -0.5), pad],
                         axis=1)
    wa2 = jnp.zeros((LANES, GLA_HEADS * GLA_DK), F32).at[MISC_ALR:MISC_ALR + GLA_GATE_RANK].set(w_a2).astype(BF16)
    tab2 = _rope_tables(T)
    nt = T // tm
    row = lambda i: (i, 0)
    fixed = lambda i: (0, 0)
    tabm = lambda i: (i % nt, 0)
    outs = [(256, BF16), (256, BF16), (512, BF16), (512, BF16), (256, F32),
            (512, BF16), (256, BF16), (128, BF16)]
    col = lambda i: (0, i)
    return pl.pallas_call(
        _in_proj_kernel,
        grid=(N // tm,),
        in_specs=[
            pl.BlockSpec((tm, D_MODEL), row),
            pl.BlockSpec((1, D_MODEL), fixed),
            pl.BlockSpec((D_MODEL, IN_PAD), fixed),
            pl.BlockSpec((LANES, GLA_HEADS * GLA_DK), fixed),
            pl.BlockSpec((1, GLA_HEADS * GLA_DK), fixed),
            pl.BlockSpec((tm, 3 * LANES), tabm),
        ],
        out_specs=[pl.BlockSpec((tm, w), row) for w, _ in outs]
        + [pl.BlockSpec((LANES, tm), col), pl.BlockSpec((LANES, tm), col)],
        out_shape=[jax.ShapeDtypeStruct((N, w), d) for w, d in outs]
        + [jax.ShapeDtypeStruct((LANES, N), BF16), jax.ShapeDtypeStruct((LANES, N), F32)],
        compiler_params=_params("parallel"),
        name="in_proj",
    )(x2, g.reshape(1, -1), wp, wa2, b_a.reshape(1, -1), tab2)


def _gla_kernel(q_ref, k_ref, v_ref, la_ref, r_ref, hg_ref, o_ref, st_ref, *, n_chunks):
    C = GLA_CHUNK

    @pl.when(pl.program_id(1) == 0)
    def _():
        st_ref[...] = jnp.zeros_like(st_ref)

    ri = lax.broadcasted_iota(jnp.int32, (C, C), 0)
    ci = lax.broadcasted_iota(jnp.int32, (C, C), 1)
    tril = ri >= ci
    row_id = lax.broadcasted_iota(jnp.int32, (C, GLA_HEADS * GLA_DK), 0)
    hg = hg_ref[...]
    state = [st_ref[hh] for hh in range(GLA_HEADS)]

    for c in range(n_chunks):
        rows = slice(c * C, (c + 1) * C)
        la = la_ref[rows, :]
        b = la
        for sh in (1, 2, 4, 8, 16, 32):
            b = b + jnp.where(row_id >= sh, pltpu.roll(b, sh, 0), 0.0)
        b_mid = b[C // 2:C // 2 + 1, :]
        b_last = b[C - 1:C, :]
        q = q_ref[rows, :].astype(F32)
        k = k_ref[rows, :].astype(F32)
        qe = (q * jnp.exp(b - b_mid)).astype(BF16)
        ke = (k * jnp.exp(b_mid - b)).astype(BF16)
        kl = (k * jnp.exp(b_last - b)).astype(BF16)
        qb = (q * jnp.exp(b)).astype(BF16)
        dec = jnp.exp(b_last)
        for hh in range(GLA_HEADS):
            ks = slice(hh * GLA_DK, (hh + 1) * GLA_DK)
            vs = slice(hh * GLA_DV, (hh + 1) * GLA_DV)
            v = v_ref[rows, vs]
            att = jnp.where(tril, _dot_nt(qe[:, ks], ke[:, ks]), 0.0)
            o = _dot(att.astype(BF16), v) + _dot_nt(qb[:, ks], state[hh].astype(BF16))
            state[hh] = state[hh] * dec[:, ks] + _dot_tn(v, kl[:, ks])
            o = o * lax.rsqrt(jnp.mean(o * o, axis=-1, keepdims=True) + EPS) * hg
            r = r_ref[rows, vs].astype(F32)
            o_ref[rows, vs] = (o * (r * jax.nn.sigmoid(r))).astype(o_ref.dtype)
    for hh in range(GLA_HEADS):
        st_ref[hh] = state[hh]


def _gla(aq, ak, av, la, ar, head_g, B, T, tg):
    N = B * T
    nt = T // tg
    row = lambda b, i: (b * nt + i, 0)
    return pl.pallas_call(
        functools.partial(_gla_kernel, n_chunks=tg // GLA_CHUNK),
        grid=(B, nt),
        in_specs=[
            pl.BlockSpec((tg, 256), row), pl.BlockSpec((tg, 256), row), pl.BlockSpec((tg, 512), row),
            pl.BlockSpec((tg, 256), row), pl.BlockSpec((tg, 512), row),
            pl.BlockSpec((1, GLA_DV), lambda b, i: (0, 0)),
        ],
        out_specs=pl.BlockSpec((tg, 512), row),
        out_shape=jax.ShapeDtypeStruct((N, 512), BF16),
        scratch_shapes=[pltpu.VMEM((GLA_HEADS, GLA_DV, GLA_DK), F32)],
        compiler_params=_params("parallel", "arbitrary"),
        name="gla",
    )(aq, ak, av, la, ar, head_g.reshape(1, -1))


DSA_GROUP_LANES = 1024
MIN_NORMAL = 1.1754943508222875e-38
SEARCH_ROUND = 3
SEARCH_WARMUP_ROUNDS = 4
SEARCH_INTERP_ROUNDS = 12
SEARCH_BISECT_ROUNDS = 100


def _dsa_kernel(bq_ref, iq_ref, qmisct_ref, kk_ref, v1t_ref, tri_ref, o_ref,
                sc_ref, s_ref, qs_ref, acc_ref, *, tq, tk, topk):
    qi = pl.program_id(1)
    q0 = qi * tq
    nkb = (q0 + tq + tk - 1) // tk
    qpos = q0 + lax.broadcasted_iota(jnp.int32, (1, tq), 1)
    key_iota = lax.broadcasted_iota(jnp.int32, (tk, tq), 0)
    H = DSA_HEADS
    S = SUBLANES

    def krows(kb):
        return pl.ds(pl.multiple_of(kb * tk, tk), tk)

    def fold_rows(x, op):
        parts = [x[j * S:(j + 1) * S, :] for j in range(x.shape[0] // S)]
        while len(parts) > 1:
            parts = [op(parts[j], parts[j + 1]) for j in range(0, len(parts) - 1, 2)] + (
                [parts[-1]] if len(parts) % 2 else [])
        return parts[0]

    iw = [qmisct_ref[MISC_IW + h:MISC_IW + h + 1, :] for h in range(IDX_HEADS)]
    for h in range(IDX_HEADS):
        qs_ref[h * tq:(h + 1) * tq, :] = iq_ref[:, h * IDX_HD:(h + 1) * IDX_HD]

    def score_block(kb, mm):
        x = _dot_nt(kk_ref[krows(kb), IDX_HD:2 * IDX_HD], qs_ref[0:IDX_HEADS * tq, :])
        sc = iw[0] * jnp.maximum(x[:, 0:tq], 0.0)
        for h in range(1, IDX_HEADS):
            sc = sc + iw[h] * jnp.maximum(x[:, h * tq:(h + 1) * tq], 0.0)
        sc = sc + 0.0
        sc_ref[kb] = jnp.where(kb * tk + key_iota <= qpos, sc, -jnp.inf)
        return jnp.maximum(mm[0], fold_rows(sc, jnp.maximum)), jnp.minimum(mm[1], fold_rows(sc, jnp.minimum))

    mx8, mn8 = lax.fori_loop(0, nkb, score_block,
                             (jnp.full((S, tq), -jnp.inf, F32), jnp.full((S, tq), jnp.inf, F32)))
    hi_bound = jnp.max(mx8, axis=0, keepdims=True)
    lo_bound = jnp.min(mn8, axis=0, keepdims=True)

    def count_ge(cand):
        def body(kb, cnt):
            return cnt + fold_rows(jnp.where(sc_ref[kb] >= cand, 1, 0), jnp.add)
        cnt = lax.fori_loop(0, nkb, body, jnp.zeros((S, tq), jnp.int32))
        return jnp.sum(cnt, axis=0, keepdims=True)

    def midpoint(lo, hi):
        return 0.5 * lo + 0.5 * hi

    def finished(lo, hi, c_lo):
        mid = midpoint(lo, hi)
        closed = jnp.where(mid <= lo, 1, jnp.where(mid >= hi, 1, 0))
        closed = jnp.where(lo == 0.0, jnp.where(hi == MIN_NORMAL, 1, closed), closed)
        return jnp.where(c_lo == topk, 1, closed)

    def step(st, pick):
        lo, hi, c_lo, c_hi, g_lo, g_hi, last = st
        done = finished(lo, hi, c_lo)
        cand = pick(lo, hi, g_lo, g_hi)
        cand = jnp.where(cand > lo, jnp.where(cand < hi, cand, midpoint(lo, hi)), midpoint(lo, hi))
        cand = jnp.where(done > 0, lo, cand)
        c = count_ge(cand)
        g = c.astype(F32) - (topk - 0.5)
        up = jnp.where(done > 0, 0, jnp.where(c >= topk, 1, 0))
        dn = jnp.where(done > 0, 0, jnp.where(c >= topk, 0, 1))
        g_hi = jnp.where(up * last > 0, g_hi * 0.5, g_hi)
        g_lo = jnp.where(dn * last < 0, g_lo * 0.5, g_lo)
        return (jnp.where(up > 0, cand, lo), jnp.where(dn > 0, cand, hi),
                jnp.where(up > 0, c, c_lo), jnp.where(dn > 0, c, c_hi),
                jnp.where(up > 0, g, g_lo), jnp.where(dn > 0, g, g_hi),
                jnp.where(up > 0, 1, jnp.where(dn > 0, -1, last)))

    def interpolate(lo, hi, g_lo, g_hi):
        return lo + (hi - lo) * (g_lo / (g_lo - g_hi))

    def bisect(lo, hi, g_lo, g_hi):
        return midpoint(lo, hi)

    def unfinished(st):
        return jnp.max(jnp.where(finished(st[0], st[1], st[2]) > 0, 0.0, 1.0)) > 0.0

    def one_round(st, pick):
        for _ in range(SEARCH_ROUND):
            st = step(st, pick)
        return st

    def search(st, pick, max_rounds):
        return lax.while_loop(lambda c: jnp.logical_and(c[0] < max_rounds, unfinished(c[1])),
                              lambda c: (c[0] + 1, one_round(c[1], pick)), (jnp.int32(0), st))[1]

    n_causal = qpos + 1
    small = n_causal <= topk
    lo0 = jnp.where(small, -jnp.inf, lo_bound)
    hi0 = hi_bound + jnp.maximum(jnp.abs(hi_bound) * 1e-6, 1e-37)
    c_lo0 = jnp.where(small, topk, n_causal)
    st = (lo0, hi0, c_lo0, jnp.zeros((1, tq), jnp.int32), c_lo0.astype(F32) - (topk - 0.5),
          jnp.full((1, tq), 0.5 - topk, F32), jnp.zeros((1, tq), jnp.int32))
    st = step(st, lambda lo, hi, g_lo, g_hi: jnp.zeros_like(lo))
    st = step(st, lambda lo, hi, g_lo, g_hi: jnp.full_like(lo, MIN_NORMAL))
    st = lax.fori_loop(0, SEARCH_WARMUP_ROUNDS, lambda _, st: one_round(st, interpolate), st)
    st = search(st, interpolate, SEARCH_INTERP_ROUNDS)
    st = search(st, bisect, SEARCH_BISECT_ROUNDS)
    thr, c_lo, c_hi = st[0], st[2], st[3]
    need = jnp.where(small, 0, jnp.where(c_lo == topk, topk, topk - c_hi)).astype(F32)

    tri = tri_ref[...]
    GL = s_ref.shape[3]
    HG = GL // tq
    NG = H // HG
    for h in range(H):
        qs_ref[h * tq:(h + 1) * tq, :] = bq_ref[:, h * DSA_HD:(h + 1) * DSA_HD]

    def qk_part(grp, kb, mx, base):
        sc = sc_ref[kb]
        if grp == 0:
            eq = jnp.where(sc == thr, 1.0, 0.0).astype(BF16)
            rank = _dot(tri, eq) + base
            bias = jnp.where(sc > thr, 0.0,
                             jnp.where(sc == thr, jnp.where(rank <= need, 0.0, NEG_BIG), NEG_BIG))
            sc_ref[kb] = bias
            base = rank[tk - 1:tk, :]
        else:
            bias = sc
        s = _dot_nt(kk_ref[krows(kb), 0:DSA_HD], qs_ref[grp * GL:(grp + 1) * GL, :])
        s = s + jnp.concatenate([bias] * HG, axis=1)
        s_ref[grp % 2, kb] = s
        return jnp.maximum(mx, fold_rows(s, jnp.maximum)), base

    def pv_part(grp, kb, m):
        p = jnp.exp2(s_ref[grp % 2, kb] - m).astype(BF16)
        acc_ref[grp % 2] += _dot(v1t_ref[:, krows(kb)], p)

    def finish(grp):
        acc = acc_ref[grp % 2]
        out = acc[0:DSA_HD, :] / acc[DSA_HD:DSA_HD + 1, :]
        for j in range(HG):
            h = grp * HG + j
            o_ref[:, h * DSA_HD:(h + 1) * DSA_HD] = out[:, j * tq:(j + 1) * tq].T.astype(o_ref.dtype)

    mx0 = jnp.full((S, GL), NEG_BIG, F32)
    m_prev = None
    for grp in range(NG + 1):
        if grp > 0:
            acc_ref[(grp - 1) % 2] = jnp.zeros(acc_ref.shape[1:], F32)

        def body(kb, carry, grp=grp, m_prev=m_prev):
            mx, base = carry
            if grp > 0:
                pv_part(grp - 1, kb, m_prev)
            if grp < NG:
                mx, base = qk_part(grp, kb, mx, base)
            return mx, base

        mx, _ = lax.fori_loop(0, nkb, body, (mx0, jnp.zeros((1, tq), F32)))
        if grp > 0:
            finish(grp - 1)
        m_prev = jnp.max(mx, axis=0, keepdims=True)


def _dsa(bq, iq, misct, kk, v1t, B, T, tq, tk, topk):
    N = B * T
    nq = T // tq
    qrow = lambda b, i: (b * nq + i, 0)
    r = lax.broadcasted_iota(jnp.int32, (tk, tk), 0)
    c = lax.broadcasted_iota(jnp.int32, (tk, tk), 1)
    tri = jnp.where(c <= r, 1.0, 0.0).astype(BF16)
    return pl.pallas_call(
        functools.partial(_dsa_kernel, tq=tq, tk=tk, topk=topk),
        grid=(B, nq),
        in_specs=[
            pl.BlockSpec((tq, 512), qrow), pl.BlockSpec((tq, 256), qrow),
            pl.BlockSpec((LANES, tq), lambda b, i: (0, b * nq + i)),
            pl.BlockSpec((T, LANES), lambda b, i: (b, 0)),
            pl.BlockSpec((LANES, T), lambda b, i: (0, b)),
            pl.BlockSpec((tk, tk), lambda b, i: (0, 0)),
        ],
        out_specs=pl.BlockSpec((tq, 512), qrow),
        out_shape=jax.ShapeDtypeStruct((N, 512), BF16),
        scratch_shapes=[
            pltpu.VMEM((T // tk, tk, tq), F32),
            pltpu.VMEM((2, T // tk, tk, DSA_GROUP_LANES), F32),
            pltpu.VMEM((DSA_HEADS * tq, DSA_HD), BF16),
            pltpu.VMEM((2, LANES, DSA_GROUP_LANES), F32),
        ],
        compiler_params=_params("parallel", "arbitrary"),
        name="dsa",
    )(bq, iq, misct, kk, v1t, tri)


FFN_CHUNKS = (768, 768, 768, 512)


def _ffn_kernel(*refs, tm, tiles_per_seq, final_norm, attn_out, rb):
    if attn_out:
        oa_ref, ob_ref, wa_ref, wb_ref = refs[:4]
        refs = refs[4:]
    x_ref, g_ref, wup_ref, cw_ref, cb_ref, wd_ref, fg_ref, o_ref, h_ref, act_ref, acc_ref = refs[:11]
    ab_refs = refs[11:]
    i = pl.program_id(0)
    H = SUBLANES
    n_c = len(FFN_CHUNKS)
    ag_refs, au_refs = ab_refs[:n_c], ab_refs[n_c:]
    offs = [sum(FFN_CHUNKS[:c]) for c in range(n_c)]
    first = (i % tiles_per_seq) == 0

    @pl.when(first)
    def _():
        for ref in ab_refs:
            ref[0:H, :] = jnp.zeros((H, ref.shape[1]), F32)

    @pl.when(jnp.logical_not(first))
    def _():
        for ref in ab_refs:
            ref[0:H, :] = ref[tm:tm + H, :]

    def up_proj(c):
        o, w = offs[c], FFN_CHUNKS[c]
        ag_refs[c][H:H + tm, :] = _dot(hb, wup_ref[:, o:o + w])
        au_refs[c][H:H + tm, :] = _dot(hb, wup_ref[:, D_FF + o:D_FF + o + w])

    def gate_act(c):
        o, w = offs[c], FFN_CHUNKS[c]
        cwg, cwu = cw_ref[:, o:o + w], cw_ref[:, D_FF + o:D_FF + o + w]
        cbg, cbu = cb_ref[:, o:o + w], cb_ref[:, D_FF + o:D_FF + o + w]
        for r0 in range(0, tm, rb):
            wg = ag_refs[c][r0:r0 + rb + H, :]
            wu = au_refs[c][r0:r0 + rb + H, :]
            gate, up = cbg, cbu
            for j in range(CONV_W):
                s0 = H - (CONV_W - 1) + j
                gate = gate + cwg[j:j + 1, :] * wg[s0:s0 + rb, :]
                up = up + cwu[j:j + 1, :] * wu[s0:s0 + rb, :]
            act_ref[r0:r0 + rb, o:o + w] = (gate * jax.nn.sigmoid(gate) * up).astype(BF16)

    def down_proj(c):
        o, w = offs[c], FFN_CHUNKS[c]
        d = _dot(act_ref[:, o:o + w], wd_ref[o:o + w, :])
        if c == 0:
            acc_ref[...] = d
        else:
            acc_ref[...] += d

    if attn_out:
        o_ref[...] = x_ref[...] + _dot(oa_ref[...], wa_ref[...]) + _dot(ob_ref[...], wb_ref[...])
        xin_ref = o_ref
    else:
        xin_ref = x_ref
    h_ref[...] = _rmsnorm_rows(xin_ref[...], g_ref[...]).astype(BF16)
    hb = h_ref[...]
    up_proj(0)
    for c in range(n_c):
        if c + 1 < n_c:
            up_proj(c + 1)
        gate_act(c)
        down_proj(c)
    y = xin_ref[...] + acc_ref[...]
    if final_norm:
        y = _rmsnorm_rows(y, fg_ref[...])
    o_ref[...] = y


def _conv_ffn(x2, g, w_up, conv_w, conv_b, w_down, final_g, layer, T, tm, final_norm, attn=None):
    N = x2.shape[0]
    row = lambda i: (i, 0)
    fixed = lambda i: (0, 0)
    def wspec(shape):
        return pl.BlockSpec((None,) + shape, lambda i: (layer, 0, 0), pipeline_mode=pl.Buffered(1))
    attn_args, attn_specs = (), []
    if attn is not None:
        oa, ob, w_o = attn
        half = w_o.shape[0] // 2
        w_o = w_o.astype(BF16)
        attn_args = (oa, ob, w_o[:half], w_o[half:])
        attn_specs = [pl.BlockSpec((tm, half), row), pl.BlockSpec((tm, half), row),
                      pl.BlockSpec((half, D_MODEL), fixed), pl.BlockSpec((half, D_MODEL), fixed)]
    return pl.pallas_call(
        functools.partial(_ffn_kernel, tm=tm, tiles_per_seq=T // tm, final_norm=final_norm,
                          attn_out=attn is not None, rb=64),
        grid=(N // tm,),
        in_specs=attn_specs + [
            pl.BlockSpec((tm, D_MODEL), row),
            pl.BlockSpec((1, D_MODEL), fixed),
            wspec((D_MODEL, 2 * D_FF)),
            pl.BlockSpec((CONV_W, 2 * D_FF), fixed),
            pl.BlockSpec((1, 2 * D_FF), fixed),
            wspec((D_FF, D_MODEL)),
            pl.BlockSpec((1, D_MODEL), fixed),
        ],
        out_specs=pl.BlockSpec((tm, D_MODEL), row),
        out_shape=jax.ShapeDtypeStruct((N, D_MODEL), F32),
        scratch_shapes=[
            pltpu.VMEM((tm, D_MODEL), BF16),
            pltpu.VMEM((tm, D_FF), BF16),
            pltpu.VMEM((tm, D_MODEL), F32),
        ] + [pltpu.VMEM((tm + SUBLANES, w), F32) for w in FFN_CHUNKS] * 2,
        compiler_params=_params("arbitrary"),
        name="conv_ffn",
    )(*attn_args, x2, g.reshape(1, -1), w_up, conv_w, conv_b.reshape(1, -1), w_down, final_g.reshape(1, -1))


def _sgu_kernel(x_ref, g_ref, wuv_ref, lng_ref, lnb_ref, ws_ref, bs_ref, wo_ref, o_ref,
                u_ref, v_ref, gated_ref, *, tm):
    W = D_MODEL
    gw = W // SGU_GROUPS
    h = _rmsnorm_rows(x_ref[...], g_ref[...]).astype(BF16)
    u_ref[...] = jax.nn.gelu(_dot(h, wuv_ref[:, 0:W]))
    v = jax.nn.gelu(_dot(h, wuv_ref[:, W:2 * W]))
    mu = jnp.mean(v, axis=-1, keepdims=True)
    vc = v - mu
    v = vc * lax.rsqrt(jnp.mean(vc * vc, axis=-1, keepdims=True) + LN_EPS) * lng_ref[...] + lnb_ref[...]
    v_ref[...] = v.astype(BF16)
    ri = lax.broadcasted_iota(jnp.int32, (SGU_CHUNK, SGU_CHUNK), 0)
    ci = lax.broadcasted_iota(jnp.int32, (SGU_CHUNK, SGU_CHUNK), 1)
    tril = ri >= ci
    for gi in range(SGU_GROUPS):
        cs = slice(gi * gw, (gi + 1) * gw)
        ws = jnp.where(tril, ws_ref[gi], 0.0).astype(BF16)
        bias = bs_ref[:, gi:gi + 1]
        for n in range(tm // SGU_CHUNK):
            rs = slice(n * SGU_CHUNK, (n + 1) * SGU_CHUNK)
            mixed = _dot(ws, v_ref[rs, cs]) + bias
            gated_ref[rs, cs] = (u_ref[rs, cs] * mixed).astype(BF16)
    o_ref[...] = x_ref[...] + _dot(gated_ref[...], wo_ref[...])


def _sgu(x2, g, w_uv, ln_g, ln_b, w_s, b_s, w_out, tm):
    N = x2.shape[0]
    row = lambda i: (i, 0)
    fixed = lambda i: (0, 0)
    return pl.pallas_call(
        functools.partial(_sgu_kernel, tm=tm),
        grid=(N // tm,),
        in_specs=[
            pl.BlockSpec((tm, D_MODEL), row),
            pl.BlockSpec((1, D_MODEL), fixed),
            pl.BlockSpec((D_MODEL, 2 * D_MODEL), fixed),
            pl.BlockSpec((1, D_MODEL), fixed), pl.BlockSpec((1, D_MODEL), fixed),
            pl.BlockSpec((SGU_GROUPS, SGU_CHUNK, SGU_CHUNK), lambda i: (0, 0, 0)),
            pl.BlockSpec((SGU_CHUNK, SGU_GROUPS), fixed),
            pl.BlockSpec((D_MODEL, D_MODEL), fixed),
        ],
        out_specs=pl.BlockSpec((tm, D_MODEL), row),
        out_shape=jax.ShapeDtypeStruct((N, D_MODEL), F32),
        scratch_shapes=[pltpu.VMEM((tm, D_MODEL), F32), pltpu.VMEM((tm, D_MODEL), BF16),
                        pltpu.VMEM((tm, D_MODEL), BF16)],
        compiler_params=_params("parallel"),
        name="sgu",
    )(x2, g.reshape(1, -1), w_uv.astype(BF16), ln_g.reshape(1, -1), ln_b.reshape(1, -1),
      w_s, b_s.T, w_out.astype(BF16))


def kernel(x, attn_norm, attn_w_in, gla_w_a2, gla_b_a, gla_head_g, attn_w_o, sgu_norm, sgu_w_uv, sgu_ln_g,
           sgu_ln_b, sgu_w_s, sgu_b_s, sgu_w_out, ffn_norm, ffn_w_up, ffn_conv_w, ffn_conv_b, ffn_w_down,
           final_norm):
    B, T, D = x.shape
    assert D == D_MODEL and T % 512 == 0
    topk = min(TOPK_MAX, T // 4)
    depth = ffn_norm.shape[0]
    x2 = x.reshape(B * T, D)
    w_up_bf, w_down_bf = ffn_w_up.astype(BF16), ffn_w_down.astype(BF16)
    for i in range(depth):
        j = i // 2
        if i % 2 == 0:
            aq, ak, av, ar, la, bq, iq, kk, v1t, misct = _in_proj(
                x2, attn_norm[j], attn_w_in[j], gla_w_a2[j], gla_b_a[j], T, 512)
            oa = _gla(aq, ak, av, la, ar, gla_head_g[j], B, T, 512)
            ob = _dsa(bq, iq, misct, kk, v1t, B, T, 256, 512, topk)
            attn = (oa, ob, attn_w_o[j])
        else:
            attn = None
            x2 = _sgu(x2, sgu_norm[j], sgu_w_uv[j], sgu_ln_g[j], sgu_ln_b[j], sgu_w_s[j], sgu_b_s[j],
                      sgu_w_out[j], 512)
        x2 = _conv_ffn(x2, ffn_norm[i], w_up_bf, ffn_conv_w[i], ffn_conv_b[i], w_down_bf,
                       final_norm, i, T, 512, i == depth - 1, attn)
    return x2.reshape(B, T, D)
```

```python
import functools

import jax
import jax.numpy as jnp
import numpy as np
from jax import lax
from jax.experimental import pallas as pl
from jax.experimental.pallas import tpu as pltpu

D_MODEL = 1024
GLA_HEADS = 4
GLA_DK = 64
GLA_DV = 128
GLA_GATE_RANK = 16
GLA_GATE_TAU = 16.0
GLA_CHUNK = 64
DSA_HEADS = 8
DSA_HD = 64
IDX_HEADS = 4
IDX_HD = 64
TOPK_MAX = 256
ROPE_THETA = 500000.0
ROPE_FRAC_DIV = 4
SGU_CHUNK = 128
SGU_GROUPS = 8
D_FF = 2816
CONV_W = 3
EPS = 1e-6
LN_EPS = 1e-5

LANES = 128
SUBLANES = 8
VMEM_LIMIT = 56 * 1024 * 1024

TOKEN_TILE = 512
DSA_Q_TILE = 256
DSA_K_TILE = 512

GLA_QK = GLA_HEADS * GLA_DK
GLA_V = GLA_HEADS * GLA_DV
DSA_Q = DSA_HEADS * DSA_HD
IDX_Q = IDX_HEADS * IDX_HD

def _segments(*widths):
    edges = [0]
    for w in widths:
        edges.append(edges[-1] + w)
    return [(edges[i], edges[i + 1]) for i in range(len(widths))], edges[-1]


(_SEG_AQ, _SEG_AK, _SEG_AV, _SEG_AR, _SEG_BQ, _SEG_IQ, _SEG_KK, _SEG_MISC), IN_PAD = _segments(
    GLA_QK, GLA_QK, GLA_V, GLA_V, DSA_Q, IDX_Q, LANES, LANES)
MISC_ALR = 64
MISC_IW = 80

F32 = jnp.float32
BF16 = jnp.bfloat16
NEG_BIG = -1e30
LOG2E = 1.4426950408889634


def _dot(a, b):
    return jnp.dot(a, b, preferred_element_type=F32)


def _dot_nt(a, b):
    return lax.dot_general(a, b, (((1,), (1,)), ((), ())), preferred_element_type=F32)


def _dot_tn(a, b):
    return lax.dot_general(a, b, (((0,), (0,)), ((), ())), preferred_element_type=F32)


def _rmsnorm_rows(x, g):
    ms = jnp.mean(x * x, axis=-1, keepdims=True)
    return x * lax.rsqrt(ms + EPS) * g


def _params(*sem):
    return pltpu.CompilerParams(dimension_semantics=sem, vmem_limit_bytes=VMEM_LIMIT)


def _rope_slab(x, tab):
    half = DSA_HD // ROPE_FRAC_DIV // 2
    c = tab[:, 0:LANES]
    s_up = tab[:, LANES:2 * LANES]
    s_dn = tab[:, 2 * LANES:3 * LANES]
    return x * c + pltpu.roll(x, half, 1) * s_up + pltpu.roll(x, LANES - half, 1) * s_dn


def _in_proj_kernel(x_ref, g_ref, w_ref, wa2_ref, ba_ref, tab2_ref,
                    aq_ref, ak_ref, av_ref, ar_ref, la_ref, bq_ref, iq_ref, kk_ref, v1t_ref, misct_ref):
    h = _rmsnorm_rows(x_ref[...], g_ref[...]).astype(BF16)

    def seg(s):
        return _dot(h, w_ref[:, s[0]:s[1]])

    aq_ref[...] = (seg(_SEG_AQ) * (GLA_DK ** -0.5)).astype(aq_ref.dtype)
    ak_ref[...] = seg(_SEG_AK).astype(ak_ref.dtype)
    av_ref[...] = seg(_SEG_AV).astype(av_ref.dtype)
    ar_ref[...] = seg(_SEG_AR).astype(ar_ref.dtype)

    tab2 = tab2_ref[...]
    bq = seg(_SEG_BQ)
    for j in range((_SEG_BQ[1] - _SEG_BQ[0]) // LANES):
        sl = slice(j * LANES, (j + 1) * LANES)
        bq_ref[:, sl] = (_rope_slab(bq[:, sl], tab2) * (DSA_HD ** -0.5 * LOG2E)).astype(bq_ref.dtype)
    iq = seg(_SEG_IQ)
    for j in range((_SEG_IQ[1] - _SEG_IQ[0]) // LANES):
        sl = slice(j * LANES, (j + 1) * LANES)
        iq_ref[:, sl] = (_rope_slab(iq[:, sl], tab2) * (IDX_HD ** -0.5)).astype(iq_ref.dtype)
    kk_ref[...] = _rope_slab(seg(_SEG_KK), tab2).astype(kk_ref.dtype)
    misc = seg(_SEG_MISC)
    misct = misc.T
    misct_ref[...] = misct
    row = lax.broadcasted_iota(jnp.int32, misct.shape, 0)
    v1t = jnp.where(row < DSA_HD, misct, jnp.where(row == DSA_HD, 1.0, 0.0))
    v1t_ref[...] = v1t.astype(v1t_ref.dtype)
    z = _dot(misc.astype(BF16), wa2_ref[...]) + ba_ref[...]
    la_ref[...] = (jnp.minimum(z, 0.0) - jnp.log(1.0 + jnp.exp(-jnp.abs(z)))) * (1.0 / GLA_GATE_TAU)


def _rope_tables(T):
    rd = DSA_HD // ROPE_FRAC_DIV
    half = rd // 2
    pos = jnp.arange(T, dtype=F32)
    inv = jnp.power(ROPE_THETA, -(jnp.arange(half, dtype=F32) * 2.0 / rd))
    ang = pos[:, None] * inv[None, :]
    cs = jnp.concatenate([jnp.cos(ang), jnp.sin(ang)], axis=1)
    sel = np.zeros((2 * half, 3 * LANES), np.float32)
    one = np.zeros((1, 3 * LANES), np.float32)
    for head in range(LANES // DSA_HD):
        o = head * DSA_HD
        one[0, o + rd:o + DSA_HD] = 1.0
        for j in range(half):
            sel[j, o + j] = 1.0
            sel[j, o + half + j] = 1.0
            sel[half + j, LANES + o + half + j] = 1.0
            sel[half + j, 2 * LANES + o + j] = -1.0
    return jnp.dot(cs, jnp.asarray(sel), precision=lax.Precision.HIGHEST) + jnp.asarray(one)


def _in_proj(x2, g, w_in, w_a2, b_a, T, tm):
    N = x2.shape[0]
    widths = (GLA_QK, GLA_QK, GLA_V, GLA_V, GLA_GATE_RANK, DSA_Q, DSA_HD, DSA_HD, IDX_Q, IDX_HD, IDX_HEADS)
    offs = [0]
    for w in widths:
        offs.append(offs[-1] + w)
    w_in = w_in.astype(BF16)
    aq, ak, av, ar, alr, bq, bk, bv, iq, ik, iw = [w_in[:, offs[i]:offs[i + 1]] for i in range(11)]
    pad = jnp.zeros((D_MODEL, LANES - IDX_HD - GLA_GATE_RANK - IDX_HEADS), w_in.dtype)
    wp = jnp.concatenate([aq, ak, av, ar, bq, iq, bk, ik, bv, alr, iw * (IDX_HEADS ** -0.5), pad],
                         axis=1)
    wa2 = jnp.zeros((LANES, GLA_HEADS * GLA_DK), F32).at[MISC_ALR:MISC_ALR + GLA_GATE_RANK].set(w_a2).astype(BF16)
    tab2 = _rope_tables(T)
    nt = T // tm
    row = lambda i: (i, 0)
    fixed = lambda i: (0, 0)
    tabm = lambda i: (i % nt, 0)
    outs = [(GLA_QK, BF16), (GLA_QK, BF16), (GLA_V, BF16), (GLA_V, BF16), (GLA_QK, F32),
            (DSA_Q, BF16), (IDX_Q, BF16), (LANES, BF16)]
    col = lambda i: (0, i)
    return pl.pallas_call(
        _in_proj_kernel,
        grid=(N // tm,),
        in_specs=[
            pl.BlockSpec((tm, D_MODEL), row),
            pl.BlockSpec((1, D_MODEL), fixed),
            pl.BlockSpec((D_MODEL, IN_PAD), fixed),
            pl.BlockSpec((LANES, GLA_HEADS * GLA_DK), fixed),
            pl.BlockSpec((1, GLA_HEADS * GLA_DK), fixed),
            pl.BlockSpec((tm, 3 * LANES), tabm),
        ],
        out_specs=[pl.BlockSpec((tm, w), row) for w, _ in outs]
        + [pl.BlockSpec((LANES, tm), col), pl.BlockSpec((LANES, tm), col)],
        out_shape=[jax.ShapeDtypeStruct((N, w), d) for w, d in outs]
        + [jax.ShapeDtypeStruct((LANES, N), BF16), jax.ShapeDtypeStruct((LANES, N), F32)],
        compiler_params=_params("parallel"),
        name="in_proj",
    )(x2, g.reshape(1, -1), wp, wa2, b_a.reshape(1, -1), tab2)


def _gla_kernel(q_ref, k_ref, v_ref, la_ref, r_ref, hg_ref, o_ref, st_ref, *, n_chunks):
    C = GLA_CHUNK

    @pl.when(pl.program_id(1) == 0)
    def _():
        st_ref[...] = jnp.zeros_like(st_ref)

    ri = lax.broadcasted_iota(jnp.int32, (C, C), 0)
    ci = lax.broadcasted_iota(jnp.int32, (C, C), 1)
    tril = ri >= ci
    row_id = lax.broadcasted_iota(jnp.int32, (C, GLA_HEADS * GLA_DK), 0)
    hg = hg_ref[...]
    state = [st_ref[hh] for hh in range(GLA_HEADS)]

    for c in range(n_chunks):
        rows = slice(c * C, (c + 1) * C)
        la = la_ref[rows, :]
        b = la
        for sh in (1, 2, 4, 8, 16, 32):
            b = b + jnp.where(row_id >= sh, pltpu.roll(b, sh, 0), 0.0)
        b_mid = b[C // 2:C // 2 + 1, :]
        b_last = b[C - 1:C, :]
        q = q_ref[rows, :].astype(F32)
        k = k_ref[rows, :].astype(F32)
        qe = (q * jnp.exp(b - b_mid)).astype(BF16)
        ke = (k * jnp.exp(b_mid - b)).astype(BF16)
        kl = (k * jnp.exp(b_last - b)).astype(BF16)
        qb = (q * jnp.exp(b)).astype(BF16)
        dec = jnp.exp(b_last)
        for hh in range(GLA_HEADS):
            ks = slice(hh * GLA_DK, (hh + 1) * GLA_DK)
            vs = slice(hh * GLA_DV, (hh + 1) * GLA_DV)
            v = v_ref[rows, vs]
            att = jnp.where(tril, _dot_nt(qe[:, ks], ke[:, ks]), 0.0)
            o = _dot(att.astype(BF16), v) + _dot_nt(qb[:, ks], state[hh].astype(BF16))
            state[hh] = state[hh] * dec[:, ks] + _dot_tn(v, kl[:, ks])
            o = o * lax.rsqrt(jnp.mean(o * o, axis=-1, keepdims=True) + EPS) * hg
            r = r_ref[rows, vs].astype(F32)
            o_ref[rows, vs] = (o * (r * jax.nn.sigmoid(r))).astype(o_ref.dtype)
    for hh in range(GLA_HEADS):
        st_ref[hh] = state[hh]


def _gla(aq, ak, av, la, ar, head_g, B, T, tg):
    N = B * T
    nt = T // tg
    row = lambda b, i: (b * nt + i, 0)
    return pl.pallas_call(
        functools.partial(_gla_kernel, n_chunks=tg // GLA_CHUNK),
        grid=(B, nt),
        in_specs=[
            pl.BlockSpec((tg, GLA_QK), row), pl.BlockSpec((tg, GLA_QK), row), pl.BlockSpec((tg, GLA_V), row),
            pl.BlockSpec((tg, GLA_QK), row), pl.BlockSpec((tg, GLA_V), row),
            pl.BlockSpec((1, GLA_DV), lambda b, i: (0, 0)),
        ],
        out_specs=pl.BlockSpec((tg, GLA_V), row),
        out_shape=jax.ShapeDtypeStruct((N, GLA_V), BF16),
        scratch_shapes=[pltpu.VMEM((GLA_HEADS, GLA_DV, GLA_DK), F32)],
        compiler_params=_params("parallel", "arbitrary"),
        name="gla",
    )(aq, ak, av, la, ar, head_g.reshape(1, -1))


DSA_GROUP_LANES = 1024
MIN_NORMAL = 1.1754943508222875e-38
SEARCH_ROUND = 2
SEARCH_WARMUP_ROUNDS = 7
SEARCH_INTERP_ROUNDS = 18
SEARCH_BISECT_ROUNDS = 150


def _dsa_kernel(bq_ref, iq_ref, qmisct_ref, kk_ref, v1t_ref, tri_ref, o_ref,
                sc_ref, s_ref, qs_ref, acc_ref, *, tq, tk, topk):
    qi = pl.program_id(1)
    q0 = qi * tq
    nkb = (q0 + tq + tk - 1) // tk
    qpos = q0 + lax.broadcasted_iota(jnp.int32, (1, tq), 1)
    key_iota = lax.broadcasted_iota(jnp.int32, (tk, tq), 0)
    H = DSA_HEADS
    S = SUBLANES

    def krows(kb):
        return pl.ds(pl.multiple_of(kb * tk, tk), tk)

    def fold_rows(x, op):
        parts = [x[j * S:(j + 1) * S, :] for j in range(x.shape[0] // S)]
        while len(parts) > 1:
            parts = [op(parts[j], parts[j + 1]) for j in range(0, len(parts) - 1, 2)] + (
                [parts[-1]] if len(parts) % 2 else [])
        return parts[0]

    iw = [qmisct_ref[MISC_IW + h:MISC_IW + h + 1, :] for h in range(IDX_HEADS)]
    for h in range(IDX_HEADS):
        qs_ref[h * tq:(h + 1) * tq, :] = iq_ref[:, h * IDX_HD:(h + 1) * IDX_HD]

    def score_block(kb, mm):
        x = _dot_nt(kk_ref[krows(kb), IDX_HD:2 * IDX_HD], qs_ref[0:IDX_HEADS * tq, :])
        sc = iw[0] * jnp.maximum(x[:, 0:tq], 0.0)
        for h in range(1, IDX_HEADS):
            sc = sc + iw[h] * jnp.maximum(x[:, h * tq:(h + 1) * tq], 0.0)
        sc = sc + 0.0
        sc_ref[kb] = jnp.where(kb * tk + key_iota <= qpos, sc, -jnp.inf)
        return jnp.maximum(mm[0], fold_rows(sc, jnp.maximum)), jnp.minimum(mm[1], fold_rows(sc, jnp.minimum))

    mx8, mn8 = lax.fori_loop(0, nkb, score_block,
                             (jnp.full((S, tq), -jnp.inf, F32), jnp.full((S, tq), jnp.inf, F32)))
    hi_bound = jnp.max(mx8, axis=0, keepdims=True)
    lo_bound = jnp.min(mn8, axis=0, keepdims=True)

    def count_ge(cand):
        def body(kb, cnt):
            return cnt + fold_rows(jnp.where(sc_ref[kb] >= cand, 1, 0), jnp.add)
        cnt = lax.fori_loop(0, nkb, body, jnp.zeros((S, tq), jnp.int32))
        return jnp.sum(cnt, axis=0, keepdims=True)

    def midpoint(lo, hi):
        return 0.5 * lo + 0.5 * hi

    def finished(lo, hi, c_lo):
        mid = midpoint(lo, hi)
        closed = jnp.where(mid <= lo, 1, jnp.where(mid >= hi, 1, 0))
        closed = jnp.where(lo == 0.0, jnp.where(hi == MIN_NORMAL, 1, closed), closed)
        return jnp.where(c_lo == topk, 1, closed)

    def step(st, pick):
        lo, hi, c_lo, c_hi, g_lo, g_hi, last = st
        done = finished(lo, hi, c_lo)
        cand = pick(lo, hi, g_lo, g_hi)
        cand = jnp.where(cand > lo, jnp.where(cand < hi, cand, midpoint(lo, hi)), midpoint(lo, hi))
        cand = jnp.where(done > 0, lo, cand)
        c = count_ge(cand)
        g = c.astype(F32) - (topk - 0.5)
        up = jnp.where(done > 0, 0, jnp.where(c >= topk, 1, 0))
        dn = jnp.where(done > 0, 0, jnp.where(c >= topk, 0, 1))
        g_hi = jnp.where(up * last > 0, g_hi * 0.5, g_hi)
        g_lo = jnp.where(dn * last < 0, g_lo * 0.5, g_lo)
        return (jnp.where(up > 0, cand, lo), jnp.where(dn > 0, cand, hi),
                jnp.where(up > 0, c, c_lo), jnp.where(dn > 0, c, c_hi),
                jnp.where(up > 0, g, g_lo), jnp.where(dn > 0, g, g_hi),
                jnp.where(up > 0, 1, jnp.where(dn > 0, -1, last)))

    def interpolate(lo, hi, g_lo, g_hi):
        return lo + (hi - lo) * (g_lo / (g_lo - g_hi))

    def bisect(lo, hi, g_lo, g_hi):
        return midpoint(lo, hi)

    def unfinished(st):
        return jnp.max(jnp.where(finished(st[0], st[1], st[2]) > 0, 0.0, 1.0)) > 0.0

    def one_round(st, pick):
        for _ in range(SEARCH_ROUND):
            st = step(st, pick)
        return st

    def search(st, pick, max_rounds):
        return lax.while_loop(lambda c: jnp.logical_and(c[0] < max_rounds, unfinished(c[1])),
                              lambda c: (c[0] + 1, one_round(c[1], pick)), (jnp.int32(0), st))[1]

    n_causal = qpos + 1
    small = n_causal <= topk
    lo0 = jnp.where(small, -jnp.inf, lo_bound)
    hi0 = hi_bound + jnp.maximum(jnp.abs(hi_bound) * 1e-6, 1e-37)
    c_lo0 = jnp.where(small, topk, n_causal)
    st = (lo0, hi0, c_lo0, jnp.zeros((1, tq), jnp.int32), c_lo0.astype(F32) - (topk - 0.5),
          jnp.full((1, tq), 0.5 - topk, F32), jnp.zeros((1, tq), jnp.int32))
    st = step(st, lambda lo, hi, g_lo, g_hi: jnp.zeros_like(lo))
    st = step(st, lambda lo, hi, g_lo, g_hi: jnp.full_like(lo, MIN_NORMAL))
    st = lax.fori_loop(0, SEARCH_WARMUP_ROUNDS, lambda _, st: one_round(st, interpolate), st)
    st = search(st, interpolate, SEARCH_INTERP_ROUNDS)
    st = search(st, bisect, SEARCH_BISECT_ROUNDS)
    thr, c_lo, c_hi = st[0], st[2], st[3]
    need = jnp.where(small, 0, jnp.where(c_lo == topk, topk, topk - c_hi)).astype(F32)

    tri = tri_ref[...]
    GL = s_ref.shape[3]
    HG = GL // tq
    NG = H // HG
    for h in range(H):
        qs_ref[h * tq:(h + 1) * tq, :] = bq_ref[:, h * DSA_HD:(h + 1) * DSA_HD]

    def qk_part(grp, kb, mx, base):
        sc = sc_ref[kb]
        if grp == 0:
            eq = jnp.where(sc == thr, 1.0, 0.0).astype(BF16)
            rank = _dot(tri, eq) + base
            bias = jnp.where(sc > thr, 0.0,
                             jnp.where(sc == thr, jnp.where(rank <= need, 0.0, NEG_BIG), NEG_BIG))
            sc_ref[kb] = bias
            base = rank[tk - 1:tk, :]
        else:
            bias = sc
        s = _dot_nt(kk_ref[krows(kb), 0:DSA_HD], qs_ref[grp * GL:(grp + 1) * GL, :])
        s = s + jnp.concatenate([bias] * HG, axis=1)
        s_ref[grp % 2, kb] = s
        return jnp.maximum(mx, fold_rows(s, jnp.maximum)), base

    def pv_part(grp, kb, m):
        p = jnp.exp2(s_ref[grp % 2, kb] - m).astype(BF16)
        acc_ref[grp % 2] += _dot(v1t_ref[:, krows(kb)], p)

    def finish(grp):
        acc = acc_ref[grp % 2]
        out = acc[0:DSA_HD, :] / acc[DSA_HD:DSA_HD + 1, :]
        for j in range(HG):
            h = grp * HG + j
            o_ref[:, h * DSA_HD:(h + 1) * DSA_HD] = out[:, j * tq:(j + 1) * tq].T.astype(o_ref.dtype)

    mx0 = jnp.full((S, GL), NEG_BIG, F32)
    m_prev = None
    for grp in range(NG + 1):
        if grp > 0:
            acc_ref[(grp - 1) % 2] = jnp.zeros(acc_ref.shape[1:], F32)

        def body(kb, carry, grp=grp, m_prev=m_prev):
            mx, base = carry
            if grp > 0:
                pv_part(grp - 1, kb, m_prev)
            if grp < NG:
                mx, base = qk_part(grp, kb, mx, base)
            return mx, base

        mx, _ = lax.fori_loop(0, nkb, body, (mx0, jnp.zeros((1, tq), F32)))
        if grp > 0:
            finish(grp - 1)
        m_prev = jnp.max(mx, axis=0, keepdims=True)


def _dsa(bq, iq, misct, kk, v1t, B, T, tq, tk, topk):
    N = B * T
    nq = T // tq
    qrow = lambda b, i: (b * nq + i, 0)
    r = lax.broadcasted_iota(jnp.int32, (tk, tk), 0)
    c = lax.broadcasted_iota(jnp.int32, (tk, tk), 1)
    tri = jnp.where(c <= r, 1.0, 0.0).astype(BF16)
    return pl.pallas_call(
        functools.partial(_dsa_kernel, tq=tq, tk=tk, topk=topk),
        grid=(B, nq),
        in_specs=[
            pl.BlockSpec((tq, DSA_Q), qrow), pl.BlockSpec((tq, IDX_Q), qrow),
            pl.BlockSpec((LANES, tq), lambda b, i: (0, b * nq + i)),
            pl.BlockSpec((T, LANES), lambda b, i: (b, 0)),
            pl.BlockSpec((LANES, T), lambda b, i: (0, b)),
            pl.BlockSpec((tk, tk), lambda b, i: (0, 0)),
        ],
        out_specs=pl.BlockSpec((tq, DSA_Q), qrow),
        out_shape=jax.ShapeDtypeStruct((N, DSA_Q), BF16),
        scratch_shapes=[
            pltpu.VMEM((T // tk, tk, tq), F32),
            pltpu.VMEM((2, T // tk, tk, DSA_GROUP_LANES), F32),
            pltpu.VMEM((DSA_HEADS * tq, DSA_HD), BF16),
            pltpu.VMEM((2, LANES, DSA_GROUP_LANES), F32),
        ],
        compiler_params=_params("parallel", "arbitrary"),
        name="dsa",
    )(bq, iq, misct, kk, v1t, tri)


FFN_CHUNKS = (768, 768, 768, 512)


def _ffn_kernel(*refs, tm, tiles_per_seq, final_norm, attn_out, rb):
    if attn_out:
        oa_ref, ob_ref, wa_ref, wb_ref = refs[:4]
        refs = refs[4:]
    x_ref, g_ref, wup_ref, cw_ref, cb_ref, wd_ref, fg_ref, o_ref, h_ref, act_ref, acc_ref = refs[:11]
    ab_refs = refs[11:]
    i = pl.program_id(0)
    H = SUBLANES
    n_c = len(FFN_CHUNKS)
    ag_refs, au_refs = ab_refs[:n_c], ab_refs[n_c:]
    offs = [sum(FFN_CHUNKS[:c]) for c in range(n_c)]
    first = (i % tiles_per_seq) == 0

    @pl.when(first)
    def _():
        for ref in ab_refs:
            ref[0:H, :] = jnp.zeros((H, ref.shape[1]), F32)

    @pl.when(jnp.logical_not(first))
    def _():
        for ref in ab_refs:
            ref[0:H, :] = ref[tm:tm + H, :]

    def up_proj(c):
        o, w = offs[c], FFN_CHUNKS[c]
        ag_refs[c][H:H + tm, :] = _dot(hb, wup_ref[:, o:o + w])
        au_refs[c][H:H + tm, :] = _dot(hb, wup_ref[:, D_FF + o:D_FF + o + w])

    def gate_act(c):
        o, w = offs[c], FFN_CHUNKS[c]
        cwg, cwu = cw_ref[:, o:o + w], cw_ref[:, D_FF + o:D_FF + o + w]
        cbg, cbu = cb_ref[:, o:o + w], cb_ref[:, D_FF + o:D_FF + o + w]
        for r0 in range(0, tm, rb):
            wg = ag_refs[c][r0:r0 + rb + H, :]
            wu = au_refs[c][r0:r0 + rb + H, :]
            gate, up = cbg, cbu
            for j in range(CONV_W):
                s0 = H - (CONV_W - 1) + j
                gate = gate + cwg[j:j + 1, :] * wg[s0:s0 + rb, :]
                up = up + cwu[j:j + 1, :] * wu[s0:s0 + rb, :]
            act_ref[r0:r0 + rb, o:o + w] = (gate * jax.nn.sigmoid(gate) * up).astype(BF16)

    def down_proj(c):
        o, w = offs[c], FFN_CHUNKS[c]
        d = _dot(act_ref[:, o:o + w], wd_ref[o:o + w, :])
        if c == 0:
            acc_ref[...] = d
        else:
            acc_ref[...] += d

    if attn_out:
        o_ref[...] = x_ref[...] + _dot(oa_ref[...], wa_ref[...]) + _dot(ob_ref[...], wb_ref[...])
        xin_ref = o_ref
    else:
        xin_ref = x_ref
    h_ref[...] = _rmsnorm_rows(xin_ref[...], g_ref[...]).astype(BF16)
    hb = h_ref[...]
    up_proj(0)
    for c in range(n_c):
        if c + 1 < n_c:
            up_proj(c + 1)
        gate_act(c)
        down_proj(c)
    y = xin_ref[...] + acc_ref[...]
    if final_norm:
        y = _rmsnorm_rows(y, fg_ref[...])
    o_ref[...] = y


def _conv_ffn(x2, g, w_up, conv_w, conv_b, w_down, final_g, layer, T, tm, final_norm, attn=None):
    N = x2.shape[0]
    row = lambda i: (i, 0)
    fixed = lambda i: (0, 0)
    def wspec(shape):
        return pl.BlockSpec((None,) + shape, lambda i: (layer, 0, 0), pipeline_mode=pl.Buffered(1))
    attn_args, attn_specs = (), []
    if attn is not None:
        oa, ob, w_o = attn
        half = w_o.shape[0] // 2
        w_o = w_o.astype(BF16)
        attn_args = (oa, ob, w_o[:half], w_o[half:])
        attn_specs = [pl.BlockSpec((tm, half), row), pl.BlockSpec((tm, half), row),
                      pl.BlockSpec((half, D_MODEL), fixed), pl.BlockSpec((half, D_MODEL), fixed)]
    return pl.pallas_call(
        functools.partial(_ffn_kernel, tm=tm, tiles_per_seq=T // tm, final_norm=final_norm,
                          attn_out=attn is not None, rb=64),
        grid=(N // tm,),
        in_specs=attn_specs + [
            pl.BlockSpec((tm, D_MODEL), row),
            pl.BlockSpec((1, D_MODEL), fixed),
            wspec((D_MODEL, 2 * D_FF)),
            pl.BlockSpec((CONV_W, 2 * D_FF), fixed),
            pl.BlockSpec((1, 2 * D_FF), fixed),
            wspec((D_FF, D_MODEL)),
            pl.BlockSpec((1, D_MODEL), fixed),
        ],
        out_specs=pl.BlockSpec((tm, D_MODEL), row),
        out_shape=jax.ShapeDtypeStruct((N, D_MODEL), F32),
        scratch_shapes=[
            pltpu.VMEM((tm, D_MODEL), BF16),
            pltpu.VMEM((tm, D_FF), BF16),
            pltpu.VMEM((tm, D_MODEL), F32),
        ] + [pltpu.VMEM((tm + SUBLANES, w), F32) for w in FFN_CHUNKS] * 2,
        compiler_params=_params("arbitrary"),
        name="conv_ffn",
    )(*attn_args, x2, g.reshape(1, -1), w_up, conv_w, conv_b.reshape(1, -1), w_down, final_g.reshape(1, -1))


def _sgu_kernel(x_ref, g_ref, wuv_ref, lng_ref, lnb_ref, ws_ref, bs_ref, wo_ref, o_ref,
                u_ref, v_ref, gated_ref, *, tm):
    W = D_MODEL
    gw = W // SGU_GROUPS
    h = _rmsnorm_rows(x_ref[...], g_ref[...]).astype(BF16)
    u_ref[...] = jax.nn.gelu(_dot(h, wuv_ref[:, 0:W]))
    v = jax.nn.gelu(_dot(h, wuv_ref[:, W:2 * W]))
    mu = jnp.mean(v, axis=-1, keepdims=True)
    vc = v - mu
    v = vc * lax.rsqrt(jnp.mean(vc * vc, axis=-1, keepdims=True) + LN_EPS) * lng_ref[...] + lnb_ref[...]
    v_ref[...] = v.astype(BF16)
    ri = lax.broadcasted_iota(jnp.int32, (SGU_CHUNK, SGU_CHUNK), 0)
    ci = lax.broadcasted_iota(jnp.int32, (SGU_CHUNK, SGU_CHUNK), 1)
    tril = ri >= ci
    for gi in range(SGU_GROUPS):
        cs = slice(gi * gw, (gi + 1) * gw)
        ws = jnp.where(tril, ws_ref[gi], 0.0).astype(BF16)
        bias = bs_ref[:, gi:gi + 1]
        for n in range(tm // SGU_CHUNK):
            rs = slice(n * SGU_CHUNK, (n + 1) * SGU_CHUNK)
            mixed = _dot(ws, v_ref[rs, cs]) + bias
            gated_ref[rs, cs] = (u_ref[rs, cs] * mixed).astype(BF16)
    o_ref[...] = x_ref[...] + _dot(gated_ref[...], wo_ref[...])


def _sgu(x2, g, w_uv, ln_g, ln_b, w_s, b_s, w_out, tm):
    N = x2.shape[0]
    row = lambda i: (i, 0)
    fixed = lambda i: (0, 0)
    return pl.pallas_call(
        functools.partial(_sgu_kernel, tm=tm),
        grid=(N // tm,),
        in_specs=[
            pl.BlockSpec((tm, D_MODEL), row),
            pl.BlockSpec((1, D_MODEL), fixed),
            pl.BlockSpec((D_MODEL, 2 * D_MODEL), fixed),
            pl.BlockSpec((1, D_MODEL), fixed), pl.BlockSpec((1, D_MODEL), fixed),
            pl.BlockSpec((SGU_GROUPS, SGU_CHUNK, SGU_CHUNK), lambda i: (0, 0, 0)),
            pl.BlockSpec((SGU_CHUNK, SGU_GROUPS), fixed),
            pl.BlockSpec((D_MODEL, D_MODEL), fixed),
        ],
        out_specs=pl.BlockSpec((tm, D_MODEL), row),
        out_shape=jax.ShapeDtypeStruct((N, D_MODEL), F32),
        scratch_shapes=[pltpu.VMEM((tm, D_MODEL), F32), pltpu.VMEM((tm, D_MODEL), BF16),
                        pltpu.VMEM((tm, D_MODEL), BF16)],
        compiler_params=_params("parallel"),
        name="sgu",
    )(x2, g.reshape(1, -1), w_uv.astype(BF16), ln_g.reshape(1, -1), ln_b.reshape(1, -1),
      w_s, b_s.T, w_out.astype(BF16))


def kernel(x, attn_norm, attn_w_in, gla_w_a2, gla_b_a, gla_head_g, attn_w_o, sgu_norm, sgu_w_uv, sgu_ln_g,
           sgu_ln_b, sgu_w_s, sgu_b_s, sgu_w_out, ffn_norm, ffn_w_up, ffn_conv_w, ffn_conv_b, ffn_w_down,
           final_norm):
    B, T, D = x.shape
    assert D == D_MODEL and T % TOKEN_TILE == 0 and T % DSA_K_TILE == 0
    topk = min(TOPK_MAX, T // 4)
    depth = ffn_norm.shape[0]
    x2 = x.reshape(B * T, D)
    w_up_bf, w_down_bf = ffn_w_up.astype(BF16), ffn_w_down.astype(BF16)
    for i in range(depth):
        j = i // 2
        if i % 2 == 0:
            aq, ak, av, ar, la, bq, iq, kk, v1t, misct = _in_proj(
                x2, attn_norm[j], attn_w_in[j], gla_w_a2[j], gla_b_a[j], T, TOKEN_TILE)
            oa = _gla(aq, ak, av, la, ar, gla_head_g[j], B, T, TOKEN_TILE)
            ob = _dsa(bq, iq, misct, kk, v1t, B, T, DSA_Q_TILE, DSA_K_TILE, topk)
            attn = (oa, ob, attn_w_o[j])
        else:
            attn = None
            x2 = _sgu(x2, sgu_norm[j], sgu_w_uv[j], sgu_ln_g[j], sgu_ln_b[j], sgu_w_s[j], sgu_b_s[j],
                      sgu_w_out[j], TOKEN_TILE)
        x2 = _conv_ffn(x2, ffn_norm[i], w_up_bf, ffn_conv_w[i], ffn_conv_b[i], w_down_bf,
                       final_norm, i, T, TOKEN_TILE, i == depth - 1, attn)
    return x2.reshape(B, T, D)
```

```python
import functools

import jax
import jax.numpy as jnp
import numpy as np
from jax import lax
from jax.experimental import pallas as pl
from jax.experimental.pallas import tpu as pltpu

D_MODEL = 1024
GLA_HEADS = 4
GLA_DK = 64
GLA_DV = 128
GLA_GATE_RANK = 16
GLA_GATE_TAU = 16.0
GLA_CHUNK = 64
DSA_HEADS = 8
DSA_HD = 64
IDX_HEADS = 4
IDX_HD = 64
TOPK_MAX = 256
ROPE_THETA = 500000.0
ROPE_FRAC_DIV = 4
SGU_CHUNK = 128
SGU_GROUPS = 8
D_FF = 2816
CONV_W = 3
EPS = 1e-6
LN_EPS = 1e-5

LANES = 128
SUBLANES = 8
VMEM_LIMIT = 56 * 1024 * 1024

TOKEN_TILE = 512
DSA_Q_TILE = 256
DSA_K_TILE = 512

GLA_QK = GLA_HEADS * GLA_DK
GLA_V = GLA_HEADS * GLA_DV
DSA_Q = DSA_HEADS * DSA_HD
IDX_Q = IDX_HEADS * IDX_HD

def _segments(*widths):
    edges = [0]
    for w in widths:
        edges.append(edges[-1] + w)
    return [(edges[i], edges[i + 1]) for i in range(len(widths))], edges[-1]


(_SEG_AQ, _SEG_AK, _SEG_AV, _SEG_AR, _SEG_BQ, _SEG_IQ, _SEG_KK, _SEG_MISC), IN_PAD = _segments(
    GLA_QK, GLA_QK, GLA_V, GLA_V, DSA_Q, IDX_Q, LANES, LANES)
MISC_ALR = 64
MISC_IW = 80

F32 = jnp.float32
BF16 = jnp.bfloat16
NEG_BIG = -1e30
LOG2E = 1.4426950408889634


def _dot(a, b):
    return jnp.dot(a, b, preferred_element_type=F32)


def _dot_nt(a, b):
    return lax.dot_general(a, b, (((1,), (1,)), ((), ())), preferred_element_type=F32)


def _dot_tn(a, b):
    return lax.dot_general(a, b, (((0,), (0,)), ((), ())), preferred_element_type=F32)


def _rmsnorm_rows(x, g):
    ms = jnp.mean(x * x, axis=-1, keepdims=True)
    return x * lax.rsqrt(ms + EPS) * g


def _params(*sem):
    return pltpu.CompilerParams(dimension_semantics=sem, vmem_limit_bytes=VMEM_LIMIT)


def _rope_slab(x, tab):
    half = DSA_HD // ROPE_FRAC_DIV // 2
    c = tab[:, 0:LANES]
    s_up = tab[:, LANES:2 * LANES]
    s_dn = tab[:, 2 * LANES:3 * LANES]
    return x * c + pltpu.roll(x, half, 1) * s_up + pltpu.roll(x, LANES - half, 1) * s_dn


def _in_proj_kernel(x_ref, g_ref, w_ref, wa2_ref, ba_ref, tab2_ref,
                    aq_ref, ak_ref, av_ref, ar_ref, la_ref, bq_ref, iq_ref, kk_ref, v1t_ref, misct_ref):
    h = _rmsnorm_rows(x_ref[...], g_ref[...]).astype(BF16)

    def seg(s):
        return _dot(h, w_ref[:, s[0]:s[1]])

    aq_ref[...] = (seg(_SEG_AQ) * (GLA_DK ** -0.5)).astype(aq_ref.dtype)
    ak_ref[...] = seg(_SEG_AK).astype(ak_ref.dtype)
    av_ref[...] = seg(_SEG_AV).astype(av_ref.dtype)
    ar_ref[...] = seg(_SEG_AR).astype(ar_ref.dtype)

    tab2 = tab2_ref[...]
    bq = seg(_SEG_BQ)
    for j in range((_SEG_BQ[1] - _SEG_BQ[0]) // LANES):
        sl = slice(j * LANES, (j + 1) * LANES)
        bq_ref[:, sl] = (_rope_slab(bq[:, sl], tab2) * (DSA_HD ** -0.5 * LOG2E)).astype(bq_ref.dtype)
    iq = seg(_SEG_IQ)
    for j in range((_SEG_IQ[1] - _SEG_IQ[0]) // LANES):
        sl = slice(j * LANES, (j + 1) * LANES)
        iq_ref[:, sl] = (_rope_slab(iq[:, sl], tab2) * (IDX_HD ** -0.5)).astype(iq_ref.dtype)
    kk_ref[...] = _rope_slab(seg(_SEG_KK), tab2).astype(kk_ref.dtype)
    misc = seg(_SEG_MISC)
    misct = misc.T
    misct_ref[...] = misct
    row = lax.broadcasted_iota(jnp.int32, misct.shape, 0)
    v1t = jnp.where(row < DSA_HD, misct, jnp.where(row == DSA_HD, 1.0, 0.0))
    v1t_ref[...] = v1t.astype(v1t_ref.dtype)
    z = _dot(misc.astype(BF16), wa2_ref[...]) + ba_ref[...]
    la_ref[...] = (jnp.minimum(z, 0.0) - jnp.log(1.0 + jnp.exp(-jnp.abs(z)))) * (1.0 / GLA_GATE_TAU)


def _rope_tables(T):
    rd = DSA_HD // ROPE_FRAC_DIV
    half = rd // 2
    pos = jnp.arange(T, dtype=F32)
    inv = jnp.power(ROPE_THETA, -(jnp.arange(half, dtype=F32) * 2.0 / rd))
    ang = pos[:, None] * inv[None, :]
    cs = jnp.concatenate([jnp.cos(ang), jnp.sin(ang)], axis=1)
    sel = np.zeros((2 * half, 3 * LANES), np.float32)
    one = np.zeros((1, 3 * LANES), np.float32)
    for head in range(LANES // DSA_HD):
        o = head * DSA_HD
        one[0, o + rd:o + DSA_HD] = 1.0
        for j in range(half):
            sel[j, o + j] = 1.0
            sel[j, o + half + j] = 1.0
            sel[half + j, LANES + o + half + j] = 1.0
            sel[half + j, 2 * LANES + o + j] = -1.0
    return jnp.dot(cs, jnp.asarray(sel), precision=lax.Precision.HIGHEST) + jnp.asarray(one)


def _in_proj(x2, g, w_in, w_a2, b_a, T, tm):
    N = x2.shape[0]
    widths = (GLA_QK, GLA_QK, GLA_V, GLA_V, GLA_GATE_RANK, DSA_Q, DSA_HD, DSA_HD, IDX_Q, IDX_HD, IDX_HEADS)
    offs = [0]
    for w in widths:
        offs.append(offs[-1] + w)
    w_in = w_in.astype(BF16)
    aq, ak, av, ar, alr, bq, bk, bv, iq, ik, iw = [w_in[:, offs[i]:offs[i + 1]] for i in range(11)]
    pad = jnp.zeros((D_MODEL, LANES - IDX_HD - GLA_GATE_RANK - IDX_HEADS), w_in.dtype)
    wp = jnp.concatenate([aq, ak, av, ar, bq, iq, bk, ik, bv, alr, iw * (IDX_HEADS ** -0.5), pad],
                         axis=1)
    wa2 = jnp.zeros((LANES, GLA_HEADS * GLA_DK), F32).at[MISC_ALR:MISC_ALR + GLA_GATE_RANK].set(w_a2).astype(BF16)
    tab2 = _rope_tables(T)
    nt = T // tm
    row = lambda i: (i, 0)
    fixed = lambda i: (0, 0)
    tabm = lambda i: (i % nt, 0)
    outs = [(GLA_QK, BF16), (GLA_QK, BF16), (GLA_V, BF16), (GLA_V, BF16), (GLA_QK, F32),
            (DSA_Q, BF16), (IDX_Q, BF16), (LANES, BF16)]
    col = lambda i: (0, i)
    return pl.pallas_call(
        _in_proj_kernel,
        grid=(N // tm,),
        in_specs=[
            pl.BlockSpec((tm, D_MODEL), row),
            pl.BlockSpec((1, D_MODEL), fixed),
            pl.BlockSpec((D_MODEL, IN_PAD), fixed),
            pl.BlockSpec((LANES, GLA_HEADS * GLA_DK), fixed),
            pl.BlockSpec((1, GLA_HEADS * GLA_DK), fixed),
            pl.BlockSpec((tm, 3 * LANES), tabm),
        ],
        out_specs=[pl.BlockSpec((tm, w), row) for w, _ in outs]
        + [pl.BlockSpec((LANES, tm), col), pl.BlockSpec((LANES, tm), col)],
        out_shape=[jax.ShapeDtypeStruct((N, w), d) for w, d in outs]
        + [jax.ShapeDtypeStruct((LANES, N), BF16), jax.ShapeDtypeStruct((LANES, N), F32)],
        compiler_params=_params("parallel"),
        name="in_proj",
    )(x2, g.reshape(1, -1), wp, wa2, b_a.reshape(1, -1), tab2)


def _gla_kernel(q_ref, k_ref, v_ref, la_ref, r_ref, hg_ref, o_ref, st_ref, *, n_chunks):
    C = GLA_CHUNK

    @pl.when(pl.program_id(1) == 0)
    def _():
        st_ref[...] = jnp.zeros_like(st_ref)

    ri = lax.broadcasted_iota(jnp.int32, (C, C), 0)
    ci = lax.broadcasted_iota(jnp.int32, (C, C), 1)
    tril = ri >= ci
    row_id = lax.broadcasted_iota(jnp.int32, (C, GLA_HEADS * GLA_DK), 0)
    hg = hg_ref[...]
    state = [st_ref[hh] for hh in range(GLA_HEADS)]

    for c in range(n_chunks):
        rows = slice(c * C, (c + 1) * C)
        la = la_ref[rows, :]
        b = la
        for sh in (1, 2, 4, 8, 16, 32):
            b = b + jnp.where(row_id >= sh, pltpu.roll(b, sh, 0), 0.0)
        b_mid = b[C // 2:C // 2 + 1, :]
        b_last = b[C - 1:C, :]
        q = q_ref[rows, :].astype(F32)
        k = k_ref[rows, :].astype(F32)
        qe = (q * jnp.exp(b - b_mid)).astype(BF16)
        ke = (k * jnp.exp(b_mid - b)).astype(BF16)
        kl = (k * jnp.exp(b_last - b)).astype(BF16)
        qb = (q * jnp.exp(b)).astype(BF16)
        dec = jnp.exp(b_last)
        for hh in range(GLA_HEADS):
            ks = slice(hh * GLA_DK, (hh + 1) * GLA_DK)
            vs = slice(hh * GLA_DV, (hh + 1) * GLA_DV)
            v = v_ref[rows, vs]
            att = jnp.where(tril, _dot_nt(qe[:, ks], ke[:, ks]), 0.0)
            o = _dot(att.astype(BF16), v) + _dot_nt(qb[:, ks], state[hh].astype(BF16))
            state[hh] = state[hh] * dec[:, ks] + _dot_tn(v, kl[:, ks])
            o = o * lax.rsqrt(jnp.mean(o * o, axis=-1, keepdims=True) + EPS) * hg
            r = r_ref[rows, vs].astype(F32)
            o_ref[rows, vs] = (o * (r * jax.nn.sigmoid(r))).astype(o_ref.dtype)
    for hh in range(GLA_HEADS):
        st_ref[hh] = state[hh]


def _gla(aq, ak, av, la, ar, head_g, B, T, tg):
    N = B * T
    nt = T // tg
    row = lambda b, i: (b * nt + i, 0)
    return pl.pallas_call(
        functools.partial(_gla_kernel, n_chunks=tg // GLA_CHUNK),
        grid=(B, nt),
        in_specs=[
            pl.BlockSpec((tg, GLA_QK), row), pl.BlockSpec((tg, GLA_QK), row), pl.BlockSpec((tg, GLA_V), row),
            pl.BlockSpec((tg, GLA_QK), row), pl.BlockSpec((tg, GLA_V), row),
            pl.BlockSpec((1, GLA_DV), lambda b, i: (0, 0)),
        ],
        out_specs=pl.BlockSpec((tg, GLA_V), row),
        out_shape=jax.ShapeDtypeStruct((N, GLA_V), BF16),
        scratch_shapes=[pltpu.VMEM((GLA_HEADS, GLA_DV, GLA_DK), F32)],
        compiler_params=_params("parallel", "arbitrary"),
        name="gla",
    )(aq, ak, av, la, ar, head_g.reshape(1, -1))


DSA_GROUP_LANES = 1024
MIN_NORMAL = 1.1754943508222875e-38
TINY_BRACKET = 2.0 ** -60
TINY_SCALE = 2.0 ** 64
SEARCH_ROUND = 2
SEARCH_WARMUP_ROUNDS = 7
SEARCH_INTERP_ROUNDS = 18
SEARCH_BISECT_ROUNDS = 150


def _dsa_kernel(bq_ref, iq_ref, qmisct_ref, kk_ref, v1t_ref, tri_ref, o_ref,
                sc_ref, s_ref, qs_ref, acc_ref, *, tq, tk, topk):
    qi = pl.program_id(1)
    q0 = qi * tq
    nkb = (q0 + tq + tk - 1) // tk
    qpos = q0 + lax.broadcasted_iota(jnp.int32, (1, tq), 1)
    key_iota = lax.broadcasted_iota(jnp.int32, (tk, tq), 0)
    H = DSA_HEADS
    S = SUBLANES

    def krows(kb):
        return pl.ds(pl.multiple_of(kb * tk, tk), tk)

    def fold_rows(x, op):
        parts = [x[j * S:(j + 1) * S, :] for j in range(x.shape[0] // S)]
        while len(parts) > 1:
            parts = [op(parts[j], parts[j + 1]) for j in range(0, len(parts) - 1, 2)] + (
                [parts[-1]] if len(parts) % 2 else [])
        return parts[0]

    iw = [qmisct_ref[MISC_IW + h:MISC_IW + h + 1, :] for h in range(IDX_HEADS)]
    for h in range(IDX_HEADS):
        qs_ref[h * tq:(h + 1) * tq, :] = iq_ref[:, h * IDX_HD:(h + 1) * IDX_HD]

    def score_block(kb, mm):
        x = _dot_nt(kk_ref[krows(kb), IDX_HD:2 * IDX_HD], qs_ref[0:IDX_HEADS * tq, :])
        sc = iw[0] * jnp.maximum(x[:, 0:tq], 0.0)
        for h in range(1, IDX_HEADS):
            sc = sc + iw[h] * jnp.maximum(x[:, h * tq:(h + 1) * tq], 0.0)
        sc = sc + 0.0
        sc_ref[kb] = jnp.where(kb * tk + key_iota <= qpos, sc, -jnp.inf)
        return jnp.maximum(mm[0], fold_rows(sc, jnp.maximum)), jnp.minimum(mm[1], fold_rows(sc, jnp.minimum))

    mx8, mn8 = lax.fori_loop(0, nkb, score_block,
                             (jnp.full((S, tq), -jnp.inf, F32), jnp.full((S, tq), jnp.inf, F32)))
    hi_bound = jnp.max(mx8, axis=0, keepdims=True)
    lo_bound = jnp.min(mn8, axis=0, keepdims=True)

    def count_ge(cand):
        def body(kb, cnt):
            return cnt + fold_rows(jnp.where(sc_ref[kb] >= cand, 1, 0), jnp.add)
        cnt = lax.fori_loop(0, nkb, body, jnp.zeros((S, tq), jnp.int32))
        return jnp.sum(cnt, axis=0, keepdims=True)

    def midpoint(lo, hi):
        tiny = jnp.maximum(jnp.abs(lo), jnp.abs(hi)) < TINY_BRACKET
        up = jnp.where(tiny, TINY_SCALE, 1.0)
        return (0.5 * (lo * up) + 0.5 * (hi * up)) * jnp.where(tiny, 1.0 / TINY_SCALE, 1.0)

    def finished(lo, hi, c_lo):
        mid = midpoint(lo, hi)
        closed = jnp.where(mid <= lo, 1, jnp.where(mid >= hi, 1, 0))
        closed = jnp.where(lo == 0.0, jnp.where(hi == MIN_NORMAL, 1, closed), closed)
        return jnp.where(c_lo == topk, 1, closed)

    def step(st, pick):
        lo, hi, c_lo, c_hi, g_lo, g_hi, last = st
        done = finished(lo, hi, c_lo)
        cand = pick(lo, hi, g_lo, g_hi)
        cand = jnp.where(cand > lo, jnp.where(cand < hi, cand, midpoint(lo, hi)), midpoint(lo, hi))
        cand = jnp.where(done > 0, lo, cand)
        c = count_ge(cand)
        g = c.astype(F32) - (topk - 0.5)
        up = jnp.where(done > 0, 0, jnp.where(c >= topk, 1, 0))
        dn = jnp.where(done > 0, 0, jnp.where(c >= topk, 0, 1))
        g_hi = jnp.where(up * last > 0, g_hi * 0.5, g_hi)
        g_lo = jnp.where(dn * last < 0, g_lo * 0.5, g_lo)
        return (jnp.where(up > 0, cand, lo), jnp.where(dn > 0, cand, hi),
                jnp.where(up > 0, c, c_lo), jnp.where(dn > 0, c, c_hi),
                jnp.where(up > 0, g, g_lo), jnp.where(dn > 0, g, g_hi),
                jnp.where(up > 0, 1, jnp.where(dn > 0, -1, last)))

    def interpolate(lo, hi, g_lo, g_hi):
        return lo + (hi - lo) * (g_lo / (g_lo - g_hi))

    def bisect(lo, hi, g_lo, g_hi):
        return midpoint(lo, hi)

    def unfinished(st):
        return jnp.max(jnp.where(finished(st[0], st[1], st[2]) > 0, 0.0, 1.0)) > 0.0

    def one_round(st, pick):
        for _ in range(SEARCH_ROUND):
            st = step(st, pick)
        return st

    def search(st, pick, max_rounds):
        return lax.while_loop(lambda c: jnp.logical_and(c[0] < max_rounds, unfinished(c[1])),
                              lambda c: (c[0] + 1, one_round(c[1], pick)), (jnp.int32(0), st))[1]

    n_causal = qpos + 1
    small = n_causal <= topk
    lo0 = jnp.where(small, -jnp.inf, lo_bound)
    hi0 = hi_bound + jnp.maximum(jnp.abs(hi_bound) * 1e-6, 1e-37)
    c_lo0 = jnp.where(small, topk, n_causal)
    st = (lo0, hi0, c_lo0, jnp.zeros((1, tq), jnp.int32), c_lo0.astype(F32) - (topk - 0.5),
          jnp.full((1, tq), 0.5 - topk, F32), jnp.zeros((1, tq), jnp.int32))
    st = step(st, lambda lo, hi, g_lo, g_hi: jnp.zeros_like(lo))
    st = step(st, lambda lo, hi, g_lo, g_hi: jnp.full_like(lo, MIN_NORMAL))
    st = lax.fori_loop(0, SEARCH_WARMUP_ROUNDS, lambda _, st: one_round(st, interpolate), st)
    st = search(st, interpolate, SEARCH_INTERP_ROUNDS)
    st = search(st, bisect, SEARCH_BISECT_ROUNDS)
    thr, c_lo, c_hi = st[0], st[2], st[3]
    need = jnp.where(small, 0, jnp.where(c_lo == topk, topk, topk - c_hi)).astype(F32)

    tri = tri_ref[...]
    GL = s_ref.shape[3]
    HG = GL // tq
    NG = H // HG
    for h in range(H):
        qs_ref[h * tq:(h + 1) * tq, :] = bq_ref[:, h * DSA_HD:(h + 1) * DSA_HD]

    def qk_part(grp, kb, mx, base):
        sc = sc_ref[kb]
        if grp == 0:
            eq = jnp.where(sc == thr, 1.0, 0.0).astype(BF16)
            rank = _dot(tri, eq) + base
            bias = jnp.where(sc > thr, 0.0,
                             jnp.where(sc == thr, jnp.where(rank <= need, 0.0, NEG_BIG), NEG_BIG))
            sc_ref[kb] = bias
            base = rank[tk - 1:tk, :]
        else:
            bias = sc
        s = _dot_nt(kk_ref[krows(kb), 0:DSA_HD], qs_ref[grp * GL:(grp + 1) * GL, :])
        s = s + jnp.concatenate([bias] * HG, axis=1)
        s_ref[grp % 2, kb] = s
        return jnp.maximum(mx, fold_rows(s, jnp.maximum)), base

    def pv_part(grp, kb, m):
        p = jnp.exp2(s_ref[grp % 2, kb] - m).astype(BF16)
        acc_ref[grp % 2] += _dot(v1t_ref[:, krows(kb)], p)

    def finish(grp):
        acc = acc_ref[grp % 2]
        out = acc[0:DSA_HD, :] / acc[DSA_HD:DSA_HD + 1, :]
        for j in range(HG):
            h = grp * HG + j
            o_ref[:, h * DSA_HD:(h + 1) * DSA_HD] = out[:, j * tq:(j + 1) * tq].T.astype(o_ref.dtype)

    mx0 = jnp.full((S, GL), NEG_BIG, F32)
    m_prev = None
    for grp in range(NG + 1):
        if grp > 0:
            acc_ref[(grp - 1) % 2] = jnp.zeros(acc_ref.shape[1:], F32)

        def body(kb, carry, grp=grp, m_prev=m_prev):
            mx, base = carry
            if grp > 0:
                pv_part(grp - 1, kb, m_prev)
            if grp < NG:
                mx, base = qk_part(grp, kb, mx, base)
            return mx, base

        mx, _ = lax.fori_loop(0, nkb, body, (mx0, jnp.zeros((1, tq), F32)))
        if grp > 0:
            finish(grp - 1)
        m_prev = jnp.max(mx, axis=0, keepdims=True)


def _dsa(bq, iq, misct, kk, v1t, B, T, tq, tk, topk):
    N = B * T
    nq = T // tq
    qrow = lambda b, i: (b * nq + i, 0)
    r = lax.broadcasted_iota(jnp.int32, (tk, tk), 0)
    c = lax.broadcasted_iota(jnp.int32, (tk, tk), 1)
    tri = jnp.where(c <= r, 1.0, 0.0).astype(BF16)
    return pl.pallas_call(
        functools.partial(_dsa_kernel, tq=tq, tk=tk, topk=topk),
        grid=(B, nq),
        in_specs=[
            pl.BlockSpec((tq, DSA_Q), qrow), pl.BlockSpec((tq, IDX_Q), qrow),
            pl.BlockSpec((LANES, tq), lambda b, i: (0, b * nq + i)),
            pl.BlockSpec((T, LANES), lambda b, i: (b, 0)),
            pl.BlockSpec((LANES, T), lambda b, i: (0, b)),
            pl.BlockSpec((tk, tk), lambda b, i: (0, 0)),
        ],
        out_specs=pl.BlockSpec((tq, DSA_Q), qrow),
        out_shape=jax.ShapeDtypeStruct((N, DSA_Q), BF16),
        scratch_shapes=[
            pltpu.VMEM((T // tk, tk, tq), F32),
            pltpu.VMEM((2, T // tk, tk, DSA_GROUP_LANES), F32),
            pltpu.VMEM((DSA_HEADS * tq, DSA_HD), BF16),
            pltpu.VMEM((2, LANES, DSA_GROUP_LANES), F32),
        ],
        compiler_params=_params("parallel", "arbitrary"),
        name="dsa",
    )(bq, iq, misct, kk, v1t, tri)


FFN_CHUNKS = (768, 768, 768, 512)


def _ffn_kernel(*refs, tm, tiles_per_seq, final_norm, attn_out, rb):
    if attn_out:
        oa_ref, ob_ref, wa_ref, wb_ref = refs[:4]
        refs = refs[4:]
    x_ref, g_ref, wup_ref, cw_ref, cb_ref, wd_ref, fg_ref, o_ref, h_ref, act_ref, acc_ref = refs[:11]
    ab_refs = refs[11:]
    i = pl.program_id(0)
    H = SUBLANES
    n_c = len(FFN_CHUNKS)
    ag_refs, au_refs = ab_refs[:n_c], ab_refs[n_c:]
    offs = [sum(FFN_CHUNKS[:c]) for c in range(n_c)]
    first = (i % tiles_per_seq) == 0

    @pl.when(first)
    def _():
        for ref in ab_refs:
            ref[0:H, :] = jnp.zeros((H, ref.shape[1]), F32)

    @pl.when(jnp.logical_not(first))
    def _():
        for ref in ab_refs:
            ref[0:H, :] = ref[tm:tm + H, :]

    def up_proj(c):
        o, w = offs[c], FFN_CHUNKS[c]
        ag_refs[c][H:H + tm, :] = _dot(hb, wup_ref[:, o:o + w])
        au_refs[c][H:H + tm, :] = _dot(hb, wup_ref[:, D_FF + o:D_FF + o + w])

    def gate_act(c):
        o, w = offs[c], FFN_CHUNKS[c]
        cwg, cwu = cw_ref[:, o:o + w], cw_ref[:, D_FF + o:D_FF + o + w]
        cbg, cbu = cb_ref[:, o:o + w], cb_ref[:, D_FF + o:D_FF + o + w]
        for r0 in range(0, tm, rb):
            wg = ag_refs[c][r0:r0 + rb + H, :]
            wu = au_refs[c][r0:r0 + rb + H, :]
            gate, up = cbg, cbu
            for j in range(CONV_W):
                s0 = H - (CONV_W - 1) + j
                gate = gate + cwg[j:j + 1, :] * wg[s0:s0 + rb, :]
                up = up + cwu[j:j + 1, :] * wu[s0:s0 + rb, :]
            act_ref[r0:r0 + rb, o:o + w] = (gate * jax.nn.sigmoid(gate) * up).astype(BF16)

    def down_proj(c):
        o, w = offs[c], FFN_CHUNKS[c]
        d = _dot(act_ref[:, o:o + w], wd_ref[o:o + w, :])
        if c == 0:
            acc_ref[...] = d
        else:
            acc_ref[...] += d

    if attn_out:
        o_ref[...] = x_ref[...] + _dot(oa_ref[...], wa_ref[...]) + _dot(ob_ref[...], wb_ref[...])
        xin_ref = o_ref
    else:
        xin_ref = x_ref
    h_ref[...] = _rmsnorm_rows(xin_ref[...], g_ref[...]).astype(BF16)
    hb = h_ref[...]
    up_proj(0)
    for c in range(n_c):
        if c + 1 < n_c:
            up_proj(c + 1)
        gate_act(c)
        down_proj(c)
    y = xin_ref[...] + acc_ref[...]
    if final_norm:
        y = _rmsnorm_rows(y, fg_ref[...])
    o_ref[...] = y


def _conv_ffn(x2, g, w_up, conv_w, conv_b, w_down, final_g, layer, T, tm, final_norm, attn=None):
    N = x2.shape[0]
    row = lambda i: (i, 0)
    fixed = lambda i: (0, 0)
    def wspec(shape):
        return pl.BlockSpec((None,) + shape, lambda i: (layer, 0, 0), pipeline_mode=pl.Buffered(1))
    attn_args, attn_specs = (), []
    if attn is not None:
        oa, ob, w_o = attn
        half = w_o.shape[0] // 2
        w_o = w_o.astype(BF16)
        attn_args = (oa, ob, w_o[:half], w_o[half:])
        attn_specs = [pl.BlockSpec((tm, half), row), pl.BlockSpec((tm, half), row),
                      pl.BlockSpec((half, D_MODEL), fixed), pl.BlockSpec((half, D_MODEL), fixed)]
    return pl.pallas_call(
        functools.partial(_ffn_kernel, tm=tm, tiles_per_seq=T // tm, final_norm=final_norm,
                          attn_out=attn is not None, rb=64),
        grid=(N // tm,),
        in_specs=attn_specs + [
            pl.BlockSpec((tm, D_MODEL), row),
            pl.BlockSpec((1, D_MODEL), fixed),
            wspec((D_MODEL, 2 * D_FF)),
            pl.BlockSpec((CONV_W, 2 * D_FF), fixed),
            pl.BlockSpec((1, 2 * D_FF), fixed),
            wspec((D_FF, D_MODEL)),
            pl.BlockSpec((1, D_MODEL), fixed),
        ],
        out_specs=pl.BlockSpec((tm, D_MODEL), row),
        out_shape=jax.ShapeDtypeStruct((N, D_MODEL), F32),
        scratch_shapes=[
            pltpu.VMEM((tm, D_MODEL), BF16),
            pltpu.VMEM((tm, D_FF), BF16),
            pltpu.VMEM((tm, D_MODEL), F32),
        ] + [pltpu.VMEM((tm + SUBLANES, w), F32) for w in FFN_CHUNKS] * 2,
        compiler_params=_params("arbitrary"),
        name="conv_ffn",
    )(*attn_args, x2, g.reshape(1, -1), w_up, conv_w, conv_b.reshape(1, -1), w_down, final_g.reshape(1, -1))


def _sgu_kernel(x_ref, g_ref, wuv_ref, lng_ref, lnb_ref, ws_ref, bs_ref, wo_ref, o_ref,
                u_ref, v_ref, gated_ref, *, tm):
    W = D_MODEL
    gw = W // SGU_GROUPS
    h = _rmsnorm_rows(x_ref[...], g_ref[...]).astype(BF16)
    u_ref[...] = jax.nn.gelu(_dot(h, wuv_ref[:, 0:W]))
    v = jax.nn.gelu(_dot(h, wuv_ref[:, W:2 * W]))
    mu = jnp.mean(v, axis=-1, keepdims=True)
    vc = v - mu
    v = vc * lax.rsqrt(jnp.mean(vc * vc, axis=-1, keepdims=True) + LN_EPS) * lng_ref[...] + lnb_ref[...]
    v_ref[...] = v.astype(BF16)
    ri = lax.broadcasted_iota(jnp.int32, (SGU_CHUNK, SGU_CHUNK), 0)
    ci = lax.broadcasted_iota(jnp.int32, (SGU_CHUNK, SGU_CHUNK), 1)
    tril = ri >= ci
    for gi in range(SGU_GROUPS):
        cs = slice(gi * gw, (gi + 1) * gw)
        ws = jnp.where(tril, ws_ref[gi], 0.0).astype(BF16)
        bias = bs_ref[:, gi:gi + 1]
        for n in range(tm // SGU_CHUNK):
            rs = slice(n * SGU_CHUNK, (n + 1) * SGU_CHUNK)
            mixed = _dot(ws, v_ref[rs, cs]) + bias
            gated_ref[rs, cs] = (u_ref[rs, cs] * mixed).astype(BF16)
    o_ref[...] = x_ref[...] + _dot(gated_ref[...], wo_ref[...])


def _sgu(x2, g, w_uv, ln_g, ln_b, w_s, b_s, w_out, tm):
    N = x2.shape[0]
    row = lambda i: (i, 0)
    fixed = lambda i: (0, 0)
    return pl.pallas_call(
        functools.partial(_sgu_kernel, tm=tm),
        grid=(N // tm,),
        in_specs=[
            pl.BlockSpec((tm, D_MODEL), row),
            pl.BlockSpec((1, D_MODEL), fixed),
            pl.BlockSpec((D_MODEL, 2 * D_MODEL), fixed),
            pl.BlockSpec((1, D_MODEL), fixed), pl.BlockSpec((1, D_MODEL), fixed),
            pl.BlockSpec((SGU_GROUPS, SGU_CHUNK, SGU_CHUNK), lambda i: (0, 0, 0)),
            pl.BlockSpec((SGU_CHUNK, SGU_GROUPS), fixed),
            pl.BlockSpec((D_MODEL, D_MODEL), fixed),
        ],
        out_specs=pl.BlockSpec((tm, D_MODEL), row),
        out_shape=jax.ShapeDtypeStruct((N, D_MODEL), F32),
        scratch_shapes=[pltpu.VMEM((tm, D_MODEL), F32), pltpu.VMEM((tm, D_MODEL), BF16),
                        pltpu.VMEM((tm, D_MODEL), BF16)],
        compiler_params=_params("parallel"),
        name="sgu",
    )(x2, g.reshape(1, -1), w_uv.astype(BF16), ln_g.reshape(1, -1), ln_b.reshape(1, -1),
      w_s, b_s.T, w_out.astype(BF16))


def kernel(x, attn_norm, attn_w_in, gla_w_a2, gla_b_a, gla_head_g, attn_w_o, sgu_norm, sgu_w_uv, sgu_ln_g,
           sgu_ln_b, sgu_w_s, sgu_b_s, sgu_w_out, ffn_norm, ffn_w_up, ffn_conv_w, ffn_conv_b, ffn_w_down,
           final_norm):
    B, T, D = x.shape
    assert D == D_MODEL and T % TOKEN_TILE == 0 and T % DSA_K_TILE == 0
    topk = min(TOPK_MAX, T // 4)
    depth = ffn_norm.shape[0]
    x2 = x.reshape(B * T, D)
    w_up_bf, w_down_bf = ffn_w_up.astype(BF16), ffn_w_down.astype(BF16)
    for i in range(depth):
        j = i // 2
        if i % 2 == 0:
            aq, ak, av, ar, la, bq, iq, kk, v1t, misct = _in_proj(
                x2, attn_norm[j], attn_w_in[j], gla_w_a2[j], gla_b_a[j], T, TOKEN_TILE)
            oa = _gla(aq, ak, av, la, ar, gla_head_g[j], B, T, TOKEN_TILE)
            ob = _dsa(bq, iq, misct, kk, v1t, B, T, DSA_Q_TILE, DSA_K_TILE, topk)
            attn = (oa, ob, attn_w_o[j])
        else:
            attn = None
            x2 = _sgu(x2, sgu_norm[j], sgu_w_uv[j], sgu_ln_g[j], sgu_ln_b[j], sgu_w_s[j], sgu_b_s[j],
                      sgu_w_out[j], TOKEN_TILE)
        x2 = _conv_ffn(x2, ffn_norm[i], w_up_bf, ffn_conv_w[i], ffn_conv_b[i], w_down_bf,
                       final_norm, i, T, TOKEN_TILE, i == depth - 1, attn)
    return x2.reshape(B, T, D)
```

```python
import functools

import jax
import jax.numpy as jnp
import numpy as np
from jax import lax
from jax.experimental import pallas as pl
from jax.experimental.pallas import tpu as pltpu

D_MODEL = 1024
GLA_HEADS = 4
GLA_DK = 64
GLA_DV = 128
GLA_GATE_RANK = 16
GLA_GATE_TAU = 16.0
GLA_CHUNK = 64
DSA_HEADS = 8
DSA_HD = 64
IDX_HEADS = 4
IDX_HD = 64
TOPK_MAX = 256
ROPE_THETA = 500000.0
ROPE_FRAC_DIV = 4
SGU_CHUNK = 128
SGU_GROUPS = 8
D_FF = 2816
CONV_W = 3
EPS = 1e-6
LN_EPS = 1e-5

LANES = 128
SUBLANES = 8
VMEM_LIMIT = 56 * 1024 * 1024

TOKEN_TILE = 512
DSA_Q_TILE = 256
DSA_K_TILE = 512

GLA_QK = GLA_HEADS * GLA_DK
GLA_V = GLA_HEADS * GLA_DV
DSA_Q = DSA_HEADS * DSA_HD
IDX_Q = IDX_HEADS * IDX_HD

def _segments(*widths):
    edges = [0]
    for w in widths:
        edges.append(edges[-1] + w)
    return [(edges[i], edges[i + 1]) for i in range(len(widths))], edges[-1]


(_SEG_AQ, _SEG_AK, _SEG_AV, _SEG_AR, _SEG_BQ, _SEG_IQ, _SEG_KK, _SEG_MISC), IN_PAD = _segments(
    GLA_QK, GLA_QK, GLA_V, GLA_V, DSA_Q, IDX_Q, LANES, LANES)
MISC_ALR = 64
MISC_IW = 80

F32 = jnp.float32
BF16 = jnp.bfloat16
NEG_BIG = -1e30
LOG2E = 1.4426950408889634


def _dot(a, b):
    return jnp.dot(a, b, preferred_element_type=F32)


def _dot_nt(a, b):
    return lax.dot_general(a, b, (((1,), (1,)), ((), ())), preferred_element_type=F32)


def _dot_tn(a, b):
    return lax.dot_general(a, b, (((0,), (0,)), ((), ())), preferred_element_type=F32)


def _rmsnorm_rows(x, g):
    ms = jnp.mean(x * x, axis=-1, keepdims=True)
    return x * lax.rsqrt(ms + EPS) * g


def _params(*sem):
    return pltpu.CompilerParams(dimension_semantics=sem, vmem_limit_bytes=VMEM_LIMIT)


def _rope_slab(x, tab):
    half = DSA_HD // ROPE_FRAC_DIV // 2
    c = tab[:, 0:LANES]
    s_up = tab[:, LANES:2 * LANES]
    s_dn = tab[:, 2 * LANES:3 * LANES]
    return x * c + pltpu.roll(x, half, 1) * s_up + pltpu.roll(x, LANES - half, 1) * s_dn


def _in_proj_kernel(x_ref, g_ref, w_ref, wa2_ref, ba_ref, tab2_ref,
                    aq_ref, ak_ref, av_ref, ar_ref, la_ref, bq_ref, iq_ref, kk_ref, v1t_ref, misct_ref):
    h = _rmsnorm_rows(x_ref[...], g_ref[...]).astype(BF16)

    def seg(s):
        return _dot(h, w_ref[:, s[0]:s[1]])

    aq_ref[...] = (seg(_SEG_AQ) * (GLA_DK ** -0.5)).astype(aq_ref.dtype)
    ak_ref[...] = seg(_SEG_AK).astype(ak_ref.dtype)
    av_ref[...] = seg(_SEG_AV).astype(av_ref.dtype)
    ar_ref[...] = seg(_SEG_AR).astype(ar_ref.dtype)

    tab2 = tab2_ref[...]
    bq = seg(_SEG_BQ)
    for j in range((_SEG_BQ[1] - _SEG_BQ[0]) // LANES):
        sl = slice(j * LANES, (j + 1) * LANES)
        bq_ref[:, sl] = (_rope_slab(bq[:, sl], tab2) * (DSA_HD ** -0.5 * LOG2E)).astype(bq_ref.dtype)
    iq = seg(_SEG_IQ)
    for j in range((_SEG_IQ[1] - _SEG_IQ[0]) // LANES):
        sl = slice(j * LANES, (j + 1) * LANES)
        iq_ref[:, sl] = (_rope_slab(iq[:, sl], tab2) * (IDX_HD ** -0.5)).astype(iq_ref.dtype)
    kk_ref[...] = _rope_slab(seg(_SEG_KK), tab2).astype(kk_ref.dtype)
    misc = seg(_SEG_MISC)
    misct = misc.T
    misct_ref[...] = misct
    row = lax.broadcasted_iota(jnp.int32, misct.shape, 0)
    v1t = jnp.where(row < DSA_HD, misct, jnp.where(row == DSA_HD, 1.0, 0.0))
    v1t_ref[...] = v1t.astype(v1t_ref.dtype)
    z = _dot(misc.astype(BF16), wa2_ref[...]) + ba_ref[...]
    la_ref[...] = (jnp.minimum(z, 0.0) - jnp.log(1.0 + jnp.exp(-jnp.abs(z)))) * (1.0 / GLA_GATE_TAU)


def _rope_tables(T):
    rd = DSA_HD // ROPE_FRAC_DIV
    half = rd // 2
    pos = jnp.arange(T, dtype=F32)
    inv = jnp.power(ROPE_THETA, -(jnp.arange(half, dtype=F32) * 2.0 / rd))
    ang = pos[:, None] * inv[None, :]
    cs = jnp.concatenate([jnp.cos(ang), jnp.sin(ang)], axis=1)
    sel = np.zeros((2 * half, 3 * LANES), np.float32)
    one = np.zeros((1, 3 * LANES), np.float32)
    for head in range(LANES // DSA_HD):
        o = head * DSA_HD
        one[0, o + rd:o + DSA_HD] = 1.0
        for j in range(half):
            sel[j, o + j] = 1.0
            sel[j, o + half + j] = 1.0
            sel[half + j, LANES + o + half + j] = 1.0
            sel[half + j, 2 * LANES + o + j] = -1.0
    return jnp.dot(cs, jnp.asarray(sel), precision=lax.Precision.HIGHEST) + jnp.asarray(one)


def _in_proj(x2, g, w_in, w_a2, b_a, T, tm):
    N = x2.shape[0]
    widths = (GLA_QK, GLA_QK, GLA_V, GLA_V, GLA_GATE_RANK, DSA_Q, DSA_HD, DSA_HD, IDX_Q, IDX_HD, IDX_HEADS)
    offs = [0]
    for w in widths:
        offs.append(offs[-1] + w)
    w_in = w_in.astype(BF16)
    aq, ak, av, ar, alr, bq, bk, bv, iq, ik, iw = [w_in[:, offs[i]:offs[i + 1]] for i in range(11)]
    pad = jnp.zeros((D_MODEL, LANES - IDX_HD - GLA_GATE_RANK - IDX_HEADS), w_in.dtype)
    wp = jnp.concatenate([aq, ak, av, ar, bq, iq, bk, ik, bv, alr, iw * (IDX_HEADS ** -0.5), pad],
                         axis=1)
    wa2 = jnp.zeros((LANES, GLA_HEADS * GLA_DK), F32).at[MISC_ALR:MISC_ALR + GLA_GATE_RANK].set(w_a2).astype(BF16)
    tab2 = _rope_tables(T)
    nt = T // tm
    row = lambda i: (i, 0)
    fixed = lambda i: (0, 0)
    tabm = lambda i: (i % nt, 0)
    outs = [(GLA_QK, BF16), (GLA_QK, BF16), (GLA_V, BF16), (GLA_V, BF16), (GLA_QK, F32),
            (DSA_Q, BF16), (IDX_Q, BF16), (LANES, BF16)]
    col = lambda i: (0, i)
    return pl.pallas_call(
        _in_proj_kernel,
        grid=(N // tm,),
        in_specs=[
            pl.BlockSpec((tm, D_MODEL), row),
            pl.BlockSpec((1, D_MODEL), fixed),
            pl.BlockSpec((D_MODEL, IN_PAD), fixed),
            pl.BlockSpec((LANES, GLA_HEADS * GLA_DK), fixed),
            pl.BlockSpec((1, GLA_HEADS * GLA_DK), fixed),
            pl.BlockSpec((tm, 3 * LANES), tabm),
        ],
        out_specs=[pl.BlockSpec((tm, w), row) for w, _ in outs]
        + [pl.BlockSpec((LANES, tm), col), pl.BlockSpec((LANES, tm), col)],
        out_shape=[jax.ShapeDtypeStruct((N, w), d) for w, d in outs]
        + [jax.ShapeDtypeStruct((LANES, N), BF16), jax.ShapeDtypeStruct((LANES, N), F32)],
        compiler_params=_params("parallel"),
        name="in_proj",
    )(x2, g.reshape(1, -1), wp, wa2, b_a.reshape(1, -1), tab2)


def _gla_kernel(q_ref, k_ref, v_ref, la_ref, r_ref, hg_ref, o_ref, st_ref, *, n_chunks):
    C = GLA_CHUNK

    @pl.when(pl.program_id(1) == 0)
    def _():
        st_ref[...] = jnp.zeros_like(st_ref)

    ri = lax.broadcasted_iota(jnp.int32, (C, C), 0)
    ci = lax.broadcasted_iota(jnp.int32, (C, C), 1)
    tril = ri >= ci
    row_id = lax.broadcasted_iota(jnp.int32, (C, GLA_HEADS * GLA_DK), 0)
    hg = hg_ref[...]
    state = [st_ref[hh] for hh in range(GLA_HEADS)]

    for c in range(n_chunks):
        rows = slice(c * C, (c + 1) * C)
        la = la_ref[rows, :]
        b = la
        for sh in (1, 2, 4, 8, 16, 32):
            b = b + jnp.where(row_id >= sh, pltpu.roll(b, sh, 0), 0.0)
        b_mid = b[C // 2:C // 2 + 1, :]
        b_last = b[C - 1:C, :]
        q = q_ref[rows, :].astype(F32)
        k = k_ref[rows, :].astype(F32)
        qe = (q * jnp.exp(b - b_mid)).astype(BF16)
        ke = (k * jnp.exp(b_mid - b)).astype(BF16)
        kl = (k * jnp.exp(b_last - b)).astype(BF16)
        qb = (q * jnp.exp(b)).astype(BF16)
        dec = jnp.exp(b_last)
        for hh in range(GLA_HEADS):
            ks = slice(hh * GLA_DK, (hh + 1) * GLA_DK)
            vs = slice(hh * GLA_DV, (hh + 1) * GLA_DV)
            v = v_ref[rows, vs]
            att = jnp.where(tril, _dot_nt(qe[:, ks], ke[:, ks]), 0.0)
            o = _dot(att.astype(BF16), v) + _dot_nt(qb[:, ks], state[hh].astype(BF16))
            state[hh] = state[hh] * dec[:, ks] + _dot_tn(v, kl[:, ks])
            o = o * lax.rsqrt(jnp.mean(o * o, axis=-1, keepdims=True) + EPS) * hg
            r = r_ref[rows, vs].astype(F32)
            o_ref[rows, vs] = (o * (r * jax.nn.sigmoid(r))).astype(o_ref.dtype)
    for hh in range(GLA_HEADS):
        st_ref[hh] = state[hh]


def _gla(aq, ak, av, la, ar, head_g, B, T, tg):
    N = B * T
    nt = T // tg
    row = lambda b, i: (b * nt + i, 0)
    return pl.pallas_call(
        functools.partial(_gla_kernel, n_chunks=tg // GLA_CHUNK),
        grid=(B, nt),
        in_specs=[
            pl.BlockSpec((tg, GLA_QK), row), pl.BlockSpec((tg, GLA_QK), row), pl.BlockSpec((tg, GLA_V), row),
            pl.BlockSpec((tg, GLA_QK), row), pl.BlockSpec((tg, GLA_V), row),
            pl.BlockSpec((1, GLA_DV), lambda b, i: (0, 0)),
        ],
        out_specs=pl.BlockSpec((tg, GLA_V), row),
        out_shape=jax.ShapeDtypeStruct((N, GLA_V), BF16),
        scratch_shapes=[pltpu.VMEM((GLA_HEADS, GLA_DV, GLA_DK), F32)],
        compiler_params=_params("parallel", "arbitrary"),
        name="gla",
    )(aq, ak, av, la, ar, head_g.reshape(1, -1))


DSA_GROUP_LANES = 1024
MIN_NORMAL = 1.1754943508222875e-38
TINY_BRACKET = 2.0 ** -60
TINY_SCALE = 2.0 ** 64
SEARCH_ROUND = 2
SEARCH_WARMUP_ROUNDS = 7
SEARCH_INTERP_ROUNDS = 18
SEARCH_BISECT_ROUNDS = 150


def _dsa_kernel(bq_ref, iq_ref, qmisct_ref, kk_ref, v1t_ref, tri_ref, o_ref,
                sc_ref, s_ref, qs_ref, acc_ref, *, tq, tk, topk):
    qi = pl.program_id(1)
    q0 = qi * tq
    nkb = (q0 + tq + tk - 1) // tk
    qpos = q0 + lax.broadcasted_iota(jnp.int32, (1, tq), 1)
    key_iota = lax.broadcasted_iota(jnp.int32, (tk, tq), 0)
    H = DSA_HEADS
    S = SUBLANES

    def krows(kb):
        return pl.ds(pl.multiple_of(kb * tk, tk), tk)

    def over_blocks(body, carry):
        def pair(j, c):
            return body(2 * j + 1, body(2 * j, c))
        carry = lax.fori_loop(0, nkb // 2, pair, carry)
        return lax.fori_loop(2 * (nkb // 2), nkb, body, carry)

    def fold_rows(x, op):
        parts = [x[j * S:(j + 1) * S, :] for j in range(x.shape[0] // S)]
        while len(parts) > 1:
            parts = [op(parts[j], parts[j + 1]) for j in range(0, len(parts) - 1, 2)] + (
                [parts[-1]] if len(parts) % 2 else [])
        return parts[0]

    iw = [qmisct_ref[MISC_IW + h:MISC_IW + h + 1, :] for h in range(IDX_HEADS)]
    for h in range(IDX_HEADS):
        qs_ref[h * tq:(h + 1) * tq, :] = iq_ref[:, h * IDX_HD:(h + 1) * IDX_HD]

    def score_block(kb, mm):
        x = _dot_nt(kk_ref[krows(kb), IDX_HD:2 * IDX_HD], qs_ref[0:IDX_HEADS * tq, :])
        sc = iw[0] * jnp.maximum(x[:, 0:tq], 0.0)
        for h in range(1, IDX_HEADS):
            sc = sc + iw[h] * jnp.maximum(x[:, h * tq:(h + 1) * tq], 0.0)
        sc = sc + 0.0
        sc_ref[kb] = jnp.where(kb * tk + key_iota <= qpos, sc, -jnp.inf)
        return jnp.maximum(mm[0], fold_rows(sc, jnp.maximum)), jnp.minimum(mm[1], fold_rows(sc, jnp.minimum))

    mx8, mn8 = over_blocks(score_block, (jnp.full((S, tq), -jnp.inf, F32), jnp.full((S, tq), jnp.inf, F32)))
    hi_bound = jnp.max(mx8, axis=0, keepdims=True)
    lo_bound = jnp.min(mn8, axis=0, keepdims=True)

    def count_ge(cand):
        def body(kb, cnt):
            return cnt + fold_rows(jnp.where(sc_ref[kb] >= cand, 1, 0), jnp.add)
        cnt = over_blocks(body, jnp.zeros((S, tq), jnp.int32))
        return jnp.sum(cnt, axis=0, keepdims=True)

    def midpoint(lo, hi):
        tiny = jnp.maximum(jnp.abs(lo), jnp.abs(hi)) < TINY_BRACKET
        up = jnp.where(tiny, TINY_SCALE, 1.0)
        return (0.5 * (lo * up) + 0.5 * (hi * up)) * jnp.where(tiny, 1.0 / TINY_SCALE, 1.0)

    def finished(lo, hi, c_lo):
        mid = midpoint(lo, hi)
        closed = jnp.where(mid <= lo, 1, jnp.where(mid >= hi, 1, 0))
        closed = jnp.where(lo == 0.0, jnp.where(hi == MIN_NORMAL, 1, closed), closed)
        return jnp.where(c_lo == topk, 1, closed)

    def step(st, pick):
        lo, hi, c_lo, c_hi, g_lo, g_hi, last = st
        done = finished(lo, hi, c_lo)
        cand = pick(lo, hi, g_lo, g_hi)
        cand = jnp.where(cand > lo, jnp.where(cand < hi, cand, midpoint(lo, hi)), midpoint(lo, hi))
        cand = jnp.where(done > 0, lo, cand)
        c = count_ge(cand)
        g = c.astype(F32) - (topk - 0.5)
        up = jnp.where(done > 0, 0, jnp.where(c >= topk, 1, 0))
        dn = jnp.where(done > 0, 0, jnp.where(c >= topk, 0, 1))
        g_hi = jnp.where(up * last > 0, g_hi * 0.5, g_hi)
        g_lo = jnp.where(dn * last < 0, g_lo * 0.5, g_lo)
        return (jnp.where(up > 0, cand, lo), jnp.where(dn > 0, cand, hi),
                jnp.where(up > 0, c, c_lo), jnp.where(dn > 0, c, c_hi),
                jnp.where(up > 0, g, g_lo), jnp.where(dn > 0, g, g_hi),
                jnp.where(up > 0, 1, jnp.where(dn > 0, -1, last)))

    def interpolate(lo, hi, g_lo, g_hi):
        return lo + (hi - lo) * (g_lo / (g_lo - g_hi))

    def bisect(lo, hi, g_lo, g_hi):
        return midpoint(lo, hi)

    def unfinished(st):
        return jnp.max(jnp.where(finished(st[0], st[1], st[2]) > 0, 0.0, 1.0)) > 0.0

    def one_round(st, pick):
        for _ in range(SEARCH_ROUND):
            st = step(st, pick)
        return st

    def search(st, pick, max_rounds):
        return lax.while_loop(lambda c: jnp.logical_and(c[0] < max_rounds, unfinished(c[1])),
                              lambda c: (c[0] + 1, one_round(c[1], pick)), (jnp.int32(0), st))[1]

    n_causal = qpos + 1
    small = n_causal <= topk
    lo0 = jnp.where(small, -jnp.inf, lo_bound)
    hi0 = hi_bound + jnp.maximum(jnp.abs(hi_bound) * 1e-6, 1e-37)
    c_lo0 = jnp.where(small, topk, n_causal)
    st = (lo0, hi0, c_lo0, jnp.zeros((1, tq), jnp.int32), c_lo0.astype(F32) - (topk - 0.5),
          jnp.full((1, tq), 0.5 - topk, F32), jnp.zeros((1, tq), jnp.int32))
    st = step(st, lambda lo, hi, g_lo, g_hi: jnp.zeros_like(lo))
    st = step(st, lambda lo, hi, g_lo, g_hi: jnp.full_like(lo, MIN_NORMAL))
    st = lax.fori_loop(0, SEARCH_WARMUP_ROUNDS, lambda _, st: one_round(st, interpolate), st)
    st = search(st, interpolate, SEARCH_INTERP_ROUNDS)
    st = search(st, bisect, SEARCH_BISECT_ROUNDS)
    thr, c_lo, c_hi = st[0], st[2], st[3]
    need = jnp.where(small, 0, jnp.where(c_lo == topk, topk, topk - c_hi)).astype(F32)

    tri = tri_ref[...]
    GL = s_ref.shape[3]
    HG = GL // tq
    NG = H // HG
    for h in range(H):
        qs_ref[h * tq:(h + 1) * tq, :] = bq_ref[:, h * DSA_HD:(h + 1) * DSA_HD]

    def qk_part(grp, kb, mx, base):
        sc = sc_ref[kb]
        if grp == 0:
            eq = jnp.where(sc == thr, 1.0, 0.0).astype(BF16)
            rank = _dot(tri, eq) + base
            bias = jnp.where(sc > thr, 0.0,
                             jnp.where(sc == thr, jnp.where(rank <= need, 0.0, NEG_BIG), NEG_BIG))
            sc_ref[kb] = bias
            base = rank[tk - 1:tk, :]
        else:
            bias = sc
        s = _dot_nt(kk_ref[krows(kb), 0:DSA_HD], qs_ref[grp * GL:(grp + 1) * GL, :])
        s = s + jnp.concatenate([bias] * HG, axis=1)
        s_ref[grp % 2, kb] = s
        return jnp.maximum(mx, fold_rows(s, jnp.maximum)), base

    def pv_part(grp, kb, m):
        p = jnp.exp2(s_ref[grp % 2, kb] - m).astype(BF16)
        acc_ref[grp % 2] += _dot(v1t_ref[:, krows(kb)], p)

    def finish(grp):
        acc = acc_ref[grp % 2]
        out = acc[0:DSA_HD, :] / acc[DSA_HD:DSA_HD + 1, :]
        for j in range(HG):
            h = grp * HG + j
            o_ref[:, h * DSA_HD:(h + 1) * DSA_HD] = out[:, j * tq:(j + 1) * tq].T.astype(o_ref.dtype)

    mx0 = jnp.full((S, GL), NEG_BIG, F32)
    m_prev = None
    for grp in range(NG + 1):
        if grp > 0:
            acc_ref[(grp - 1) % 2] = jnp.zeros(acc_ref.shape[1:], F32)

        def body(kb, carry, grp=grp, m_prev=m_prev):
            mx, base = carry
            if grp > 0:
                pv_part(grp - 1, kb, m_prev)
            if grp < NG:
                mx, base = qk_part(grp, kb, mx, base)
            return mx, base

        mx, _ = over_blocks(body, (mx0, jnp.zeros((1, tq), F32)))
        if grp > 0:
            finish(grp - 1)
        m_prev = jnp.max(mx, axis=0, keepdims=True)


def _dsa(bq, iq, misct, kk, v1t, B, T, tq, tk, topk):
    N = B * T
    nq = T // tq
    qrow = lambda b, i: (b * nq + i, 0)
    r = lax.broadcasted_iota(jnp.int32, (tk, tk), 0)
    c = lax.broadcasted_iota(jnp.int32, (tk, tk), 1)
    tri = jnp.where(c <= r, 1.0, 0.0).astype(BF16)
    return pl.pallas_call(
        functools.partial(_dsa_kernel, tq=tq, tk=tk, topk=topk),
        grid=(B, nq),
        in_specs=[
            pl.BlockSpec((tq, DSA_Q), qrow), pl.BlockSpec((tq, IDX_Q), qrow),
            pl.BlockSpec((LANES, tq), lambda b, i: (0, b * nq + i)),
            pl.BlockSpec((T, LANES), lambda b, i: (b, 0)),
            pl.BlockSpec((LANES, T), lambda b, i: (0, b)),
            pl.BlockSpec((tk, tk), lambda b, i: (0, 0)),
        ],
        out_specs=pl.BlockSpec((tq, DSA_Q), qrow),
        out_shape=jax.ShapeDtypeStruct((N, DSA_Q), BF16),
        scratch_shapes=[
            pltpu.VMEM((T // tk, tk, tq), F32),
            pltpu.VMEM((2, T // tk, tk, DSA_GROUP_LANES), F32),
            pltpu.VMEM((DSA_HEADS * tq, DSA_HD), BF16),
            pltpu.VMEM((2, LANES, DSA_GROUP_LANES), F32),
        ],
        compiler_params=_params("parallel", "arbitrary"),
        name="dsa",
    )(bq, iq, misct, kk, v1t, tri)


FFN_CHUNKS = (768, 768, 768, 512)


def _ffn_kernel(*refs, tm, tiles_per_seq, final_norm, attn_out, rb):
    if attn_out:
        oa_ref, ob_ref, wa_ref, wb_ref = refs[:4]
        refs = refs[4:]
    x_ref, g_ref, wup_ref, cw_ref, cb_ref, wd_ref, fg_ref, o_ref, h_ref, act_ref, acc_ref = refs[:11]
    ab_refs = refs[11:]
    i = pl.program_id(0)
    H = SUBLANES
    n_c = len(FFN_CHUNKS)
    ag_refs, au_refs = ab_refs[:n_c], ab_refs[n_c:]
    offs = [sum(FFN_CHUNKS[:c]) for c in range(n_c)]
    first = (i % tiles_per_seq) == 0

    @pl.when(first)
    def _():
        for ref in ab_refs:
            ref[0:H, :] = jnp.zeros((H, ref.shape[1]), F32)

    @pl.when(jnp.logical_not(first))
    def _():
        for ref in ab_refs:
            ref[0:H, :] = ref[tm:tm + H, :]

    def up_proj(c):
        o, w = offs[c], FFN_CHUNKS[c]
        ag_refs[c][H:H + tm, :] = _dot(hb, wup_ref[:, o:o + w])
        au_refs[c][H:H + tm, :] = _dot(hb, wup_ref[:, D_FF + o:D_FF + o + w])

    def gate_act(c):
        o, w = offs[c], FFN_CHUNKS[c]
        cwg, cwu = cw_ref[:, o:o + w], cw_ref[:, D_FF + o:D_FF + o + w]
        cbg, cbu = cb_ref[:, o:o + w], cb_ref[:, D_FF + o:D_FF + o + w]
        for r0 in range(0, tm, rb):
            wg = ag_refs[c][r0:r0 + rb + H, :]
            wu = au_refs[c][r0:r0 + rb + H, :]
            gate, up = cbg, cbu
            for j in range(CONV_W):
                s0 = H - (CONV_W - 1) + j
                gate = gate + cwg[j:j + 1, :] * wg[s0:s0 + rb, :]
                up = up + cwu[j:j + 1, :] * wu[s0:s0 + rb, :]
            act_ref[r0:r0 + rb, o:o + w] = (gate * jax.nn.sigmoid(gate) * up).astype(BF16)

    def down_proj(c):
        o, w = offs[c], FFN_CHUNKS[c]
        d = _dot(act_ref[:, o:o + w], wd_ref[o:o + w, :])
        if c == 0:
            acc_ref[...] = d
        else:
            acc_ref[...] += d

    if attn_out:
        o_ref[...] = x_ref[...] + _dot(oa_ref[...], wa_ref[...]) + _dot(ob_ref[...], wb_ref[...])
        xin_ref = o_ref
    else:
        xin_ref = x_ref
    h_ref[...] = _rmsnorm_rows(xin_ref[...], g_ref[...]).astype(BF16)
    hb = h_ref[...]
    up_proj(0)
    for c in range(n_c):
        if c + 1 < n_c:
            up_proj(c + 1)
        gate_act(c)
        down_proj(c)
    y = xin_ref[...] + acc_ref[...]
    if final_norm:
        y = _rmsnorm_rows(y, fg_ref[...])
    o_ref[...] = y


def _conv_ffn(x2, g, w_up, conv_w, conv_b, w_down, final_g, layer, T, tm, final_norm, attn=None):
    N = x2.shape[0]
    row = lambda i: (i, 0)
    fixed = lambda i: (0, 0)
    def wspec(shape):
        return pl.BlockSpec((None,) + shape, lambda i: (layer, 0, 0), pipeline_mode=pl.Buffered(1))
    attn_args, attn_specs = (), []
    if attn is not None:
        oa, ob, w_o = attn
        half = w_o.shape[0] // 2
        w_o = w_o.astype(BF16)
        attn_args = (oa, ob, w_o[:half], w_o[half:])
        attn_specs = [pl.BlockSpec((tm, half), row), pl.BlockSpec((tm, half), row),
                      pl.BlockSpec((half, D_MODEL), fixed), pl.BlockSpec((half, D_MODEL), fixed)]
    return pl.pallas_call(
        functools.partial(_ffn_kernel, tm=tm, tiles_per_seq=T // tm, final_norm=final_norm,
                          attn_out=attn is not None, rb=64),
        grid=(N // tm,),
        in_specs=attn_specs + [
            pl.BlockSpec((tm, D_MODEL), row),
            pl.BlockSpec((1, D_MODEL), fixed),
            wspec((D_MODEL, 2 * D_FF)),
            pl.BlockSpec((CONV_W, 2 * D_FF), fixed),
            pl.BlockSpec((1, 2 * D_FF), fixed),
            wspec((D_FF, D_MODEL)),
            pl.BlockSpec((1, D_MODEL), fixed),
        ],
        out_specs=pl.BlockSpec((tm, D_MODEL), row),
        out_shape=jax.ShapeDtypeStruct((N, D_MODEL), F32),
        scratch_shapes=[
            pltpu.VMEM((tm, D_MODEL), BF16),
            pltpu.VMEM((tm, D_FF), BF16),
            pltpu.VMEM((tm, D_MODEL), F32),
        ] + [pltpu.VMEM((tm + SUBLANES, w), F32) for w in FFN_CHUNKS] * 2,
        compiler_params=_params("arbitrary"),
        name="conv_ffn",
    )(*attn_args, x2, g.reshape(1, -1), w_up, conv_w, conv_b.reshape(1, -1), w_down, final_g.reshape(1, -1))


def _sgu_kernel(x_ref, g_ref, wuv_ref, lng_ref, lnb_ref, ws_ref, bs_ref, wo_ref, o_ref,
                u_ref, v_ref, gated_ref, *, tm):
    W = D_MODEL
    gw = W // SGU_GROUPS
    h = _rmsnorm_rows(x_ref[...], g_ref[...]).astype(BF16)
    u_ref[...] = jax.nn.gelu(_dot(h, wuv_ref[:, 0:W]))
    v = jax.nn.gelu(_dot(h, wuv_ref[:, W:2 * W]))
    mu = jnp.mean(v, axis=-1, keepdims=True)
    vc = v - mu
    v = vc * lax.rsqrt(jnp.mean(vc * vc, axis=-1, keepdims=True) + LN_EPS) * lng_ref[...] + lnb_ref[...]
    v_ref[...] = v.astype(BF16)
    ri = lax.broadcasted_iota(jnp.int32, (SGU_CHUNK, SGU_CHUNK), 0)
    ci = lax.broadcasted_iota(jnp.int32, (SGU_CHUNK, SGU_CHUNK), 1)
    tril = ri >= ci
    for gi in range(SGU_GROUPS):
        cs = slice(gi * gw, (gi + 1) * gw)
        ws = jnp.where(tril, ws_ref[gi], 0.0).astype(BF16)
        bias = bs_ref[:, gi:gi + 1]
        for n in range(tm // SGU_CHUNK):
            rs = slice(n * SGU_CHUNK, (n + 1) * SGU_CHUNK)
            mixed = _dot(ws, v_ref[rs, cs]) + bias
            gated_ref[rs, cs] = (u_ref[rs, cs] * mixed).astype(BF16)
    o_ref[...] = x_ref[...] + _dot(gated_ref[...], wo_ref[...])


def _sgu(x2, g, w_uv, ln_g, ln_b, w_s, b_s, w_out, tm):
    N = x2.shape[0]
    row = lambda i: (i, 0)
    fixed = lambda i: (0, 0)
    return pl.pallas_call(
        functools.partial(_sgu_kernel, tm=tm),
        grid=(N // tm,),
        in_specs=[
            pl.BlockSpec((tm, D_MODEL), row),
            pl.BlockSpec((1, D_MODEL), fixed),
            pl.BlockSpec((D_MODEL, 2 * D_MODEL), fixed),
            pl.BlockSpec((1, D_MODEL), fixed), pl.BlockSpec((1, D_MODEL), fixed),
            pl.BlockSpec((SGU_GROUPS, SGU_CHUNK, SGU_CHUNK), lambda i: (0, 0, 0)),
            pl.BlockSpec((SGU_CHUNK, SGU_GROUPS), fixed),
            pl.BlockSpec((D_MODEL, D_MODEL), fixed),
        ],
        out_specs=pl.BlockSpec((tm, D_MODEL), row),
        out_shape=jax.ShapeDtypeStruct((N, D_MODEL), F32),
        scratch_shapes=[pltpu.VMEM((tm, D_MODEL), F32), pltpu.VMEM((tm, D_MODEL), BF16),
                        pltpu.VMEM((tm, D_MODEL), BF16)],
        compiler_params=_params("parallel"),
        name="sgu",
    )(x2, g.reshape(1, -1), w_uv.astype(BF16), ln_g.reshape(1, -1), ln_b.reshape(1, -1),
      w_s, b_s.T, w_out.astype(BF16))


def kernel(x, attn_norm, attn_w_in, gla_w_a2, gla_b_a, gla_head_g, attn_w_o, sgu_norm, sgu_w_uv, sgu_ln_g,
           sgu_ln_b, sgu_w_s, sgu_b_s, sgu_w_out, ffn_norm, ffn_w_up, ffn_conv_w, ffn_conv_b, ffn_w_down,
           final_norm):
    B, T, D = x.shape
    assert D == D_MODEL and T % TOKEN_TILE == 0 and T % DSA_K_TILE == 0
    topk = min(TOPK_MAX, T // 4)
    depth = ffn_norm.shape[0]
    x2 = x.reshape(B * T, D)
    w_up_bf, w_down_bf = ffn_w_up.astype(BF16), ffn_w_down.astype(BF16)
    for i in range(depth):
        j = i // 2
        if i % 2 == 0:
            aq, ak, av, ar, la, bq, iq, kk, v1t, misct = _in_proj(
                x2, attn_norm[j], attn_w_in[j], gla_w_a2[j], gla_b_a[j], T, TOKEN_TILE)
            oa = _gla(aq, ak, av, la, ar, gla_head_g[j], B, T, TOKEN_TILE)
            ob = _dsa(bq, iq, misct, kk, v1t, B, T, DSA_Q_TILE, DSA_K_TILE, topk)
            attn = (oa, ob, attn_w_o[j])
        else:
            attn = None
            x2 = _sgu(x2, sgu_norm[j], sgu_w_uv[j], sgu_ln_g[j], sgu_ln_b[j], sgu_w_s[j], sgu_b_s[j],
                      sgu_w_out[j], TOKEN_TILE)
        x2 = _conv_ffn(x2, ffn_norm[i], w_up_bf, ffn_conv_w[i], ffn_conv_b[i], w_down_bf,
                       final_norm, i, T, TOKEN_TILE, i == depth - 1, attn)
    return x2.reshape(B, T, D)
```

```python
import functools

import jax
import jax.numpy as jnp
import numpy as np
from jax import lax
from jax.experimental import pallas as pl
from jax.experimental.pallas import tpu as pltpu

D_MODEL = 1024
GLA_HEADS = 4
GLA_DK = 64
GLA_DV = 128
GLA_GATE_RANK = 16
GLA_GATE_TAU = 16.0
GLA_CHUNK = 64
DSA_HEADS = 8
DSA_HD = 64
IDX_HEADS = 4
IDX_HD = 64
TOPK_MAX = 256
ROPE_THETA = 500000.0
ROPE_FRAC_DIV = 4
SGU_CHUNK = 128
SGU_GROUPS = 8
D_FF = 2816
CONV_W = 3
EPS = 1e-6
LN_EPS = 1e-5

LANES = 128
SUBLANES = 8
VMEM_LIMIT = 56 * 1024 * 1024

TOKEN_TILE = 512
DSA_Q_TILE = 256
DSA_K_TILE = 512

GLA_QK = GLA_HEADS * GLA_DK
GLA_V = GLA_HEADS * GLA_DV
DSA_Q = DSA_HEADS * DSA_HD
IDX_Q = IDX_HEADS * IDX_HD

def _segments(*widths):
    edges = [0]
    for w in widths:
        edges.append(edges[-1] + w)
    return [(edges[i], edges[i + 1]) for i in range(len(widths))], edges[-1]


(_SEG_AQ, _SEG_AK, _SEG_AV, _SEG_AR, _SEG_BQ, _SEG_IQ, _SEG_KK, _SEG_MISC), IN_PAD = _segments(
    GLA_QK, GLA_QK, GLA_V, GLA_V, DSA_Q, IDX_Q, LANES, LANES)
MISC_ALR = 64
MISC_IW = 80

F32 = jnp.float32
BF16 = jnp.bfloat16
NEG_BIG = -1e30
LOG2E = 1.4426950408889634


def _dot(a, b):
    return jnp.dot(a, b, preferred_element_type=F32)


def _dot_nt(a, b):
    return lax.dot_general(a, b, (((1,), (1,)), ((), ())), preferred_element_type=F32)


def _dot_tn(a, b):
    return lax.dot_general(a, b, (((0,), (0,)), ((), ())), preferred_element_type=F32)


def _rmsnorm_rows(x, g):
    ms = jnp.mean(x * x, axis=-1, keepdims=True)
    return x * lax.rsqrt(ms + EPS) * g


def _params(*sem):
    return pltpu.CompilerParams(dimension_semantics=sem, vmem_limit_bytes=VMEM_LIMIT)


def _rope_slab(x, tab):
    half = DSA_HD // ROPE_FRAC_DIV // 2
    c = tab[:, 0:LANES]
    s_up = tab[:, LANES:2 * LANES]
    s_dn = tab[:, 2 * LANES:3 * LANES]
    return x * c + pltpu.roll(x, half, 1) * s_up + pltpu.roll(x, LANES - half, 1) * s_dn


def _in_proj_kernel(x_ref, g_ref, w_ref, wa2_ref, ba_ref, tab2_ref,
                    aq_ref, ak_ref, av_ref, ar_ref, la_ref, bq_ref, iq_ref, kk_ref, v1t_ref, misct_ref):
    h = _rmsnorm_rows(x_ref[...], g_ref[...]).astype(BF16)

    def seg(s):
        return _dot(h, w_ref[:, s[0]:s[1]])

    aq_ref[...] = (seg(_SEG_AQ) * (GLA_DK ** -0.5)).astype(aq_ref.dtype)
    ak_ref[...] = seg(_SEG_AK).astype(ak_ref.dtype)
    av_ref[...] = seg(_SEG_AV).astype(av_ref.dtype)
    ar_ref[...] = seg(_SEG_AR).astype(ar_ref.dtype)

    tab2 = tab2_ref[...]
    bq = seg(_SEG_BQ)
    for j in range((_SEG_BQ[1] - _SEG_BQ[0]) // LANES):
        sl = slice(j * LANES, (j + 1) * LANES)
        bq_ref[:, sl] = (_rope_slab(bq[:, sl], tab2) * (DSA_HD ** -0.5 * LOG2E)).astype(bq_ref.dtype)
    iq = seg(_SEG_IQ)
    for j in range((_SEG_IQ[1] - _SEG_IQ[0]) // LANES):
        sl = slice(j * LANES, (j + 1) * LANES)
        iq_ref[:, sl] = (_rope_slab(iq[:, sl], tab2) * (IDX_HD ** -0.5)).astype(iq_ref.dtype)
    kk_ref[...] = _rope_slab(seg(_SEG_KK), tab2).astype(kk_ref.dtype)
    misc = seg(_SEG_MISC)
    misct = misc.T
    misct_ref[...] = misct
    row = lax.broadcasted_iota(jnp.int32, misct.shape, 0)
    v1t = jnp.where(row < DSA_HD, misct, jnp.where(row == DSA_HD, 1.0, 0.0))
    v1t_ref[...] = v1t.astype(v1t_ref.dtype)
    z = _dot(misc.astype(BF16), wa2_ref[...]) + ba_ref[...]
    la_ref[...] = (jnp.minimum(z, 0.0) - jnp.log(1.0 + jnp.exp(-jnp.abs(z)))) * (1.0 / GLA_GATE_TAU)


def _rope_tables(T):
    rd = DSA_HD // ROPE_FRAC_DIV
    half = rd // 2
    pos = jnp.arange(T, dtype=F32)
    inv = jnp.power(ROPE_THETA, -(jnp.arange(half, dtype=F32) * 2.0 / rd))
    ang = pos[:, None] * inv[None, :]
    cs = jnp.concatenate([jnp.cos(ang), jnp.sin(ang)], axis=1)
    sel = np.zeros((2 * half, 3 * LANES), np.float32)
    one = np.zeros((1, 3 * LANES), np.float32)
    for head in range(LANES // DSA_HD):
        o = head * DSA_HD
        one[0, o + rd:o + DSA_HD] = 1.0
        for j in range(half):
            sel[j, o + j] = 1.0
            sel[j, o + half + j] = 1.0
            sel[half + j, LANES + o + half + j] = 1.0
            sel[half + j, 2 * LANES + o + j] = -1.0
    return jnp.dot(cs, jnp.asarray(sel), precision=lax.Precision.HIGHEST) + jnp.asarray(one)


def _in_proj(x2, g, w_in, w_a2, b_a, T, tm):
    N = x2.shape[0]
    widths = (GLA_QK, GLA_QK, GLA_V, GLA_V, GLA_GATE_RANK, DSA_Q, DSA_HD, DSA_HD, IDX_Q, IDX_HD, IDX_HEADS)
    offs = [0]
    for w in widths:
        offs.append(offs[-1] + w)
    w_in = w_in.astype(BF16)
    aq, ak, av, ar, alr, bq, bk, bv, iq, ik, iw = [w_in[:, offs[i]:offs[i + 1]] for i in range(11)]
    pad = jnp.zeros((D_MODEL, LANES - IDX_HD - GLA_GATE_RANK - IDX_HEADS), w_in.dtype)
    wp = jnp.concatenate([aq, ak, av, ar, bq, iq, bk, ik, bv, alr, iw * (IDX_HEADS ** -0.5), pad],
                         axis=1)
    wa2 = jnp.zeros((LANES, GLA_HEADS * GLA_DK), F32).at[MISC_ALR:MISC_ALR + GLA_GATE_RANK].set(w_a2).astype(BF16)
    tab2 = _rope_tables(T)
    nt = T // tm
    row = lambda i: (i, 0)
    fixed = lambda i: (0, 0)
    tabm = lambda i: (i % nt, 0)
    outs = [(GLA_QK, BF16), (GLA_QK, BF16), (GLA_V, BF16), (GLA_V, BF16), (GLA_QK, F32),
            (DSA_Q, BF16), (IDX_Q, BF16), (LANES, BF16)]
    col = lambda i: (0, i)
    return pl.pallas_call(
        _in_proj_kernel,
        grid=(N // tm,),
        in_specs=[
            pl.BlockSpec((tm, D_MODEL), row),
            pl.BlockSpec((1, D_MODEL), fixed),
            pl.BlockSpec((D_MODEL, IN_PAD), fixed),
            pl.BlockSpec((LANES, GLA_HEADS * GLA_DK), fixed),
            pl.BlockSpec((1, GLA_HEADS * GLA_DK), fixed),
            pl.BlockSpec((tm, 3 * LANES), tabm),
        ],
        out_specs=[pl.BlockSpec((tm, w), row) for w, _ in outs]
        + [pl.BlockSpec((LANES, tm), col), pl.BlockSpec((LANES, tm), col)],
        out_shape=[jax.ShapeDtypeStruct((N, w), d) for w, d in outs]
        + [jax.ShapeDtypeStruct((LANES, N), BF16), jax.ShapeDtypeStruct((LANES, N), F32)],
        compiler_params=_params("parallel"),
        name="in_proj",
    )(x2, g.reshape(1, -1), wp, wa2, b_a.reshape(1, -1), tab2)


def _gla_kernel(q_ref, k_ref, v_ref, la_ref, r_ref, hg_ref, o_ref, st_ref, *, n_chunks):
    C = GLA_CHUNK

    @pl.when(pl.program_id(1) == 0)
    def _():
        st_ref[...] = jnp.zeros_like(st_ref)

    ri = lax.broadcasted_iota(jnp.int32, (C, C), 0)
    ci = lax.broadcasted_iota(jnp.int32, (C, C), 1)
    tril = ri >= ci
    row_id = lax.broadcasted_iota(jnp.int32, (C, GLA_HEADS * GLA_DK), 0)
    hg = hg_ref[...]
    state = [st_ref[hh] for hh in range(GLA_HEADS)]

    for c in range(n_chunks):
        rows = slice(c * C, (c + 1) * C)
        la = la_ref[rows, :]
        b = la
        for sh in (1, 2, 4, 8, 16, 32):
            b = b + jnp.where(row_id >= sh, pltpu.roll(b, sh, 0), 0.0)
        b_mid = b[C // 2:C // 2 + 1, :]
        b_last = b[C - 1:C, :]
        q = q_ref[rows, :].astype(F32)
        k = k_ref[rows, :].astype(F32)
        qe = (q * jnp.exp(b - b_mid)).astype(BF16)
        ke = (k * jnp.exp(b_mid - b)).astype(BF16)
        kl = (k * jnp.exp(b_last - b)).astype(BF16)
        qb = (q * jnp.exp(b)).astype(BF16)
        dec = jnp.exp(b_last)
        for hh in range(GLA_HEADS):
            ks = slice(hh * GLA_DK, (hh + 1) * GLA_DK)
            vs = slice(hh * GLA_DV, (hh + 1) * GLA_DV)
            v = v_ref[rows, vs]
            att = jnp.where(tril, _dot_nt(qe[:, ks], ke[:, ks]), 0.0)
            o = _dot(att.astype(BF16), v) + _dot_nt(qb[:, ks], state[hh].astype(BF16))
            state[hh] = state[hh] * dec[:, ks] + _dot_tn(v, kl[:, ks])
            o = o * lax.rsqrt(jnp.mean(o * o, axis=-1, keepdims=True) + EPS) * hg
            r = r_ref[rows, vs].astype(F32)
            o_ref[rows, vs] = (o * (r * jax.nn.sigmoid(r))).astype(o_ref.dtype)
    for hh in range(GLA_HEADS):
        st_ref[hh] = state[hh]


def _gla(aq, ak, av, la, ar, head_g, B, T, tg):
    N = B * T
    nt = T // tg
    row = lambda b, i: (b * nt + i, 0)
    return pl.pallas_call(
        functools.partial(_gla_kernel, n_chunks=tg // GLA_CHUNK),
        grid=(B, nt),
        in_specs=[
            pl.BlockSpec((tg, GLA_QK), row), pl.BlockSpec((tg, GLA_QK), row), pl.BlockSpec((tg, GLA_V), row),
            pl.BlockSpec((tg, GLA_QK), row), pl.BlockSpec((tg, GLA_V), row),
            pl.BlockSpec((1, GLA_DV), lambda b, i: (0, 0)),
        ],
        out_specs=pl.BlockSpec((tg, GLA_V), row),
        out_shape=jax.ShapeDtypeStruct((N, GLA_V), BF16),
        scratch_shapes=[pltpu.VMEM((GLA_HEADS, GLA_DV, GLA_DK), F32)],
        compiler_params=_params("parallel", "arbitrary"),
        name="gla",
    )(aq, ak, av, la, ar, head_g.reshape(1, -1))


DSA_GROUP_LANES = 1024
MIN_NORMAL = 1.1754943508222875e-38
TINY_BRACKET = 2.0 ** -60
TINY_SCALE = 2.0 ** 64
SEARCH_ROUND = 2
SEARCH_WARMUP_ROUNDS = 7
SEARCH_INTERP_ROUNDS = 18
SEARCH_BISECT_ROUNDS = 150


def _dsa_kernel(bq_ref, iq_ref, qmisct_ref, kk_ref, v1t_ref, tri_ref, o_ref,
                sc_ref, s_ref, qs_ref, acc_ref, *, tq, tk, topk):
    qi = pl.program_id(1)
    q0 = qi * tq
    nkb = (q0 + tq + tk - 1) // tk
    qpos = q0 + lax.broadcasted_iota(jnp.int32, (1, tq), 1)
    key_iota = lax.broadcasted_iota(jnp.int32, (tk, tq), 0)
    H = DSA_HEADS
    S = SUBLANES

    def krows(kb):
        return pl.ds(pl.multiple_of(kb * tk, tk), tk)

    def over_blocks(body, carry, widths=(4, 2, 1)):
        start = 0
        for w in widths:
            def group(j, c, w=w):
                for i in range(w):
                    c = body(w * j + i, c)
                return c
            carry = lax.fori_loop(start // w, nkb // w, group, carry)
            start = (nkb // w) * w
        return carry

    def fold_rows(x, op):
        parts = [x[j * S:(j + 1) * S, :] for j in range(x.shape[0] // S)]
        while len(parts) > 1:
            parts = [op(parts[j], parts[j + 1]) for j in range(0, len(parts) - 1, 2)] + (
                [parts[-1]] if len(parts) % 2 else [])
        return parts[0]

    iw = [qmisct_ref[MISC_IW + h:MISC_IW + h + 1, :] for h in range(IDX_HEADS)]
    for h in range(IDX_HEADS):
        qs_ref[h * tq:(h + 1) * tq, :] = iq_ref[:, h * IDX_HD:(h + 1) * IDX_HD]

    def score_block(kb, mm):
        x = _dot_nt(kk_ref[krows(kb), IDX_HD:2 * IDX_HD], qs_ref[0:IDX_HEADS * tq, :])
        sc = iw[0] * jnp.maximum(x[:, 0:tq], 0.0)
        for h in range(1, IDX_HEADS):
            sc = sc + iw[h] * jnp.maximum(x[:, h * tq:(h + 1) * tq], 0.0)
        sc = sc + 0.0
        sc_ref[kb] = jnp.where(kb * tk + key_iota <= qpos, sc, -jnp.inf)
        return jnp.maximum(mm[0], fold_rows(sc, jnp.maximum)), jnp.minimum(mm[1], fold_rows(sc, jnp.minimum))

    mx8, mn8 = over_blocks(score_block, (jnp.full((S, tq), -jnp.inf, F32), jnp.full((S, tq), jnp.inf, F32)))
    hi_bound = jnp.max(mx8, axis=0, keepdims=True)
    lo_bound = jnp.min(mn8, axis=0, keepdims=True)

    def count_ge(cand):
        def body(kb, cnt):
            return cnt + fold_rows(jnp.where(sc_ref[kb] >= cand, 1, 0), jnp.add)
        cnt = over_blocks(body, jnp.zeros((S, tq), jnp.int32), widths=(2, 1))
        return jnp.sum(cnt, axis=0, keepdims=True)

    def midpoint(lo, hi):
        tiny = jnp.maximum(jnp.abs(lo), jnp.abs(hi)) < TINY_BRACKET
        up = jnp.where(tiny, TINY_SCALE, 1.0)
        return (0.5 * (lo * up) + 0.5 * (hi * up)) * jnp.where(tiny, 1.0 / TINY_SCALE, 1.0)

    def finished(lo, hi, c_lo):
        mid = midpoint(lo, hi)
        closed = jnp.where(mid <= lo, 1, jnp.where(mid >= hi, 1, 0))
        closed = jnp.where(lo == 0.0, jnp.where(hi == MIN_NORMAL, 1, closed), closed)
        return jnp.where(c_lo == topk, 1, closed)

    def step(st, pick):
        lo, hi, c_lo, c_hi, g_lo, g_hi, last = st
        done = finished(lo, hi, c_lo)
        cand = pick(lo, hi, g_lo, g_hi)
        cand = jnp.where(cand > lo, jnp.where(cand < hi, cand, midpoint(lo, hi)), midpoint(lo, hi))
        cand = jnp.where(done > 0, lo, cand)
        c = count_ge(cand)
        g = c.astype(F32) - (topk - 0.5)
        up = jnp.where(done > 0, 0, jnp.where(c >= topk, 1, 0))
        dn = jnp.where(done > 0, 0, jnp.where(c >= topk, 0, 1))
        g_hi = jnp.where(up * last > 0, g_hi * 0.5, g_hi)
        g_lo = jnp.where(dn * last < 0, g_lo * 0.5, g_lo)
        return (jnp.where(up > 0, cand, lo), jnp.where(dn > 0, cand, hi),
                jnp.where(up > 0, c, c_lo), jnp.where(dn > 0, c, c_hi),
                jnp.where(up > 0, g, g_lo), jnp.where(dn > 0, g, g_hi),
                jnp.where(up > 0, 1, jnp.where(dn > 0, -1, last)))

    def interpolate(lo, hi, g_lo, g_hi):
        return lo + (hi - lo) * (g_lo / (g_lo - g_hi))

    def bisect(lo, hi, g_lo, g_hi):
        return midpoint(lo, hi)

    def unfinished(st):
        return jnp.max(jnp.where(finished(st[0], st[1], st[2]) > 0, 0.0, 1.0)) > 0.0

    def one_round(st, pick):
        for _ in range(SEARCH_ROUND):
            st = step(st, pick)
        return st

    def search(st, pick, max_rounds):
        return lax.while_loop(lambda c: jnp.logical_and(c[0] < max_rounds, unfinished(c[1])),
                              lambda c: (c[0] + 1, one_round(c[1], pick)), (jnp.int32(0), st))[1]

    n_causal = qpos + 1
    small = n_causal <= topk
    lo0 = jnp.where(small, -jnp.inf, lo_bound)
    hi0 = hi_bound + jnp.maximum(jnp.abs(hi_bound) * 1e-6, 1e-37)
    c_lo0 = jnp.where(small, topk, n_causal)
    st = (lo0, hi0, c_lo0, jnp.zeros((1, tq), jnp.int32), c_lo0.astype(F32) - (topk - 0.5),
          jnp.full((1, tq), 0.5 - topk, F32), jnp.zeros((1, tq), jnp.int32))
    st = step(st, lambda lo, hi, g_lo, g_hi: jnp.zeros_like(lo))
    st = step(st, lambda lo, hi, g_lo, g_hi: jnp.full_like(lo, MIN_NORMAL))
    st = lax.fori_loop(0, SEARCH_WARMUP_ROUNDS, lambda _, st: one_round(st, interpolate), st)
    st = search(st, interpolate, SEARCH_INTERP_ROUNDS)
    st = search(st, bisect, SEARCH_BISECT_ROUNDS)
    thr, c_lo, c_hi = st[0], st[2], st[3]
    need = jnp.where(small, 0, jnp.where(c_lo == topk, topk, topk - c_hi)).astype(F32)

    tri = tri_ref[...]
    GL = s_ref.shape[3]
    HG = GL // tq
    NG = H // HG
    for h in range(H):
        qs_ref[h * tq:(h + 1) * tq, :] = bq_ref[:, h * DSA_HD:(h + 1) * DSA_HD]

    def qk_part(grp, kb, mx, base):
        sc = sc_ref[kb]
        if grp == 0:
            eq = jnp.where(sc == thr, 1.0, 0.0).astype(BF16)
            rank = _dot(tri, eq) + base
            bias = jnp.where(sc > thr, 0.0,
                             jnp.where(sc == thr, jnp.where(rank <= need, 0.0, NEG_BIG), NEG_BIG))
            sc_ref[kb] = bias
            base = rank[tk - 1:tk, :]
        else:
            bias = sc
        s = _dot_nt(kk_ref[krows(kb), 0:DSA_HD], qs_ref[grp * GL:(grp + 1) * GL, :])
        s = s + jnp.concatenate([bias] * HG, axis=1)
        s_ref[grp % 2, kb] = s
        return jnp.maximum(mx, fold_rows(s, jnp.maximum)), base

    def pv_part(grp, kb, m):
        p = jnp.exp2(s_ref[grp % 2, kb] - m).astype(BF16)
        acc_ref[grp % 2] += _dot(v1t_ref[:, krows(kb)], p)

    def finish(grp):
        acc = acc_ref[grp % 2]
        out = acc[0:DSA_HD, :] / acc[DSA_HD:DSA_HD + 1, :]
        for j in range(HG):
            h = grp * HG + j
            o_ref[:, h * DSA_HD:(h + 1) * DSA_HD] = out[:, j * tq:(j + 1) * tq].T.astype(o_ref.dtype)

    mx0 = jnp.full((S, GL), NEG_BIG, F32)
    m_prev = None
    for grp in range(NG + 1):
        if grp > 0:
            acc_ref[(grp - 1) % 2] = jnp.zeros(acc_ref.shape[1:], F32)

        def body(kb, carry, grp=grp, m_prev=m_prev):
            mx, base = carry
            if grp > 0:
                pv_part(grp - 1, kb, m_prev)
            if grp < NG:
                mx, base = qk_part(grp, kb, mx, base)
            return mx, base

        mx, _ = over_blocks(body, (mx0, jnp.zeros((1, tq), F32)))
        if grp > 0:
            finish(grp - 1)
        m_prev = jnp.max(mx, axis=0, keepdims=True)


def _dsa(bq, iq, misct, kk, v1t, B, T, tq, tk, topk):
    N = B * T
    nq = T // tq
    qrow = lambda b, i: (b * nq + i, 0)
    r = lax.broadcasted_iota(jnp.int32, (tk, tk), 0)
    c = lax.broadcasted_iota(jnp.int32, (tk, tk), 1)
    tri = jnp.where(c <= r, 1.0, 0.0).astype(BF16)
    return pl.pallas_call(
        functools.partial(_dsa_kernel, tq=tq, tk=tk, topk=topk),
        grid=(B, nq),
        in_specs=[
            pl.BlockSpec((tq, DSA_Q), qrow), pl.BlockSpec((tq, IDX_Q), qrow),
            pl.BlockSpec((LANES, tq), lambda b, i: (0, b * nq + i)),
            pl.BlockSpec((T, LANES), lambda b, i: (b, 0)),
            pl.BlockSpec((LANES, T), lambda b, i: (0, b)),
            pl.BlockSpec((tk, tk), lambda b, i: (0, 0)),
        ],
        out_specs=pl.BlockSpec((tq, DSA_Q), qrow),
        out_shape=jax.ShapeDtypeStruct((N, DSA_Q), BF16),
        scratch_shapes=[
            pltpu.VMEM((T // tk, tk, tq), F32),
            pltpu.VMEM((2, T // tk, tk, DSA_GROUP_LANES), F32),
            pltpu.VMEM((DSA_HEADS * tq, DSA_HD), BF16),
            pltpu.VMEM((2, LANES, DSA_GROUP_LANES), F32),
        ],
        compiler_params=_params("parallel", "arbitrary"),
        name="dsa",
    )(bq, iq, misct, kk, v1t, tri)


FFN_CHUNKS = (768, 768, 768, 512)


def _ffn_kernel(*refs, tm, tiles_per_seq, final_norm, attn_out, rb):
    if attn_out:
        oa_ref, ob_ref, wa_ref, wb_ref = refs[:4]
        refs = refs[4:]
    x_ref, g_ref, wup_ref, cw_ref, cb_ref, wd_ref, fg_ref, o_ref, h_ref, act_ref, acc_ref = refs[:11]
    ab_refs = refs[11:]
    i = pl.program_id(0)
    H = SUBLANES
    n_c = len(FFN_CHUNKS)
    ag_refs, au_refs = ab_refs[:n_c], ab_refs[n_c:]
    offs = [sum(FFN_CHUNKS[:c]) for c in range(n_c)]
    first = (i % tiles_per_seq) == 0

    @pl.when(first)
    def _():
        for ref in ab_refs:
            ref[0:H, :] = jnp.zeros((H, ref.shape[1]), F32)

    @pl.when(jnp.logical_not(first))
    def _():
        for ref in ab_refs:
            ref[0:H, :] = ref[tm:tm + H, :]

    def up_proj(c):
        o, w = offs[c], FFN_CHUNKS[c]
        ag_refs[c][H:H + tm, :] = _dot(hb, wup_ref[:, o:o + w])
        au_refs[c][H:H + tm, :] = _dot(hb, wup_ref[:, D_FF + o:D_FF + o + w])

    def gate_act(c):
        o, w = offs[c], FFN_CHUNKS[c]
        cwg, cwu = cw_ref[:, o:o + w], cw_ref[:, D_FF + o:D_FF + o + w]
        cbg, cbu = cb_ref[:, o:o + w], cb_ref[:, D_FF + o:D_FF + o + w]
        for r0 in range(0, tm, rb):
            wg = ag_refs[c][r0:r0 + rb + H, :]
            wu = au_refs[c][r0:r0 + rb + H, :]
            gate, up = cbg, cbu
            for j in range(CONV_W):
                s0 = H - (CONV_W - 1) + j
                gate = gate + cwg[j:j + 1, :] * wg[s0:s0 + rb, :]
                up = up + cwu[j:j + 1, :] * wu[s0:s0 + rb, :]
            act_ref[r0:r0 + rb, o:o + w] = (gate * jax.nn.sigmoid(gate) * up).astype(BF16)

    def down_proj(c):
        o, w = offs[c], FFN_CHUNKS[c]
        d = _dot(act_ref[:, o:o + w], wd_ref[o:o + w, :])
        if c == 0:
            acc_ref[...] = d
        else:
            acc_ref[...] += d

    if attn_out:
        o_ref[...] = x_ref[...] + _dot(oa_ref[...], wa_ref[...]) + _dot(ob_ref[...], wb_ref[...])
        xin_ref = o_ref
    else:
        xin_ref = x_ref
    h_ref[...] = _rmsnorm_rows(xin_ref[...], g_ref[...]).astype(BF16)
    hb = h_ref[...]
    up_proj(0)
    for c in range(n_c):
        if c + 1 < n_c:
            up_proj(c + 1)
        gate_act(c)
        down_proj(c)
    y = xin_ref[...] + acc_ref[...]
    if final_norm:
        y = _rmsnorm_rows(y, fg_ref[...])
    o_ref[...] = y


def _conv_ffn(x2, g, w_up, conv_w, conv_b, w_down, final_g, layer, T, tm, final_norm, attn=None):
    N = x2.shape[0]
    row = lambda i: (i, 0)
    fixed = lambda i: (0, 0)
    def wspec(shape):
        return pl.BlockSpec((None,) + shape, lambda i: (layer, 0, 0), pipeline_mode=pl.Buffered(1))
    attn_args, attn_specs = (), []
    if attn is not None:
        oa, ob, w_o = attn
        half = w_o.shape[0] // 2
        w_o = w_o.astype(BF16)
        attn_args = (oa, ob, w_o[:half], w_o[half:])
        attn_specs = [pl.BlockSpec((tm, half), row), pl.BlockSpec((tm, half), row),
                      pl.BlockSpec((half, D_MODEL), fixed), pl.BlockSpec((half, D_MODEL), fixed)]
    return pl.pallas_call(
        functools.partial(_ffn_kernel, tm=tm, tiles_per_seq=T // tm, final_norm=final_norm,
                          attn_out=attn is not None, rb=64),
        grid=(N // tm,),
        in_specs=attn_specs + [
            pl.BlockSpec((tm, D_MODEL), row),
            pl.BlockSpec((1, D_MODEL), fixed),
            wspec((D_MODEL, 2 * D_FF)),
            pl.BlockSpec((CONV_W, 2 * D_FF), fixed),
            pl.BlockSpec((1, 2 * D_FF), fixed),
            wspec((D_FF, D_MODEL)),
            pl.BlockSpec((1, D_MODEL), fixed),
        ],
        out_specs=pl.BlockSpec((tm, D_MODEL), row),
        out_shape=jax.ShapeDtypeStruct((N, D_MODEL), F32),
        scratch_shapes=[
            pltpu.VMEM((tm, D_MODEL), BF16),
            pltpu.VMEM((tm, D_FF), BF16),
            pltpu.VMEM((tm, D_MODEL), F32),
        ] + [pltpu.VMEM((tm + SUBLANES, w), F32) for w in FFN_CHUNKS] * 2,
        compiler_params=_params("arbitrary"),
        name="conv_ffn",
    )(*attn_args, x2, g.reshape(1, -1), w_up, conv_w, conv_b.reshape(1, -1), w_down, final_g.reshape(1, -1))


def _sgu_kernel(x_ref, g_ref, wuv_ref, lng_ref, lnb_ref, ws_ref, bs_ref, wo_ref, o_ref,
                u_ref, v_ref, gated_ref, *, tm):
    W = D_MODEL
    gw = W // SGU_GROUPS
    h = _rmsnorm_rows(x_ref[...], g_ref[...]).astype(BF16)
    u_ref[...] = jax.nn.gelu(_dot(h, wuv_ref[:, 0:W]))
    v = jax.nn.gelu(_dot(h, wuv_ref[:, W:2 * W]))
    mu = jnp.mean(v, axis=-1, keepdims=True)
    vc = v - mu
    v = vc * lax.rsqrt(jnp.mean(vc * vc, axis=-1, keepdims=True) + LN_EPS) * lng_ref[...] + lnb_ref[...]
    v_ref[...] = v.astype(BF16)
    ri = lax.broadcasted_iota(jnp.int32, (SGU_CHUNK, SGU_CHUNK), 0)
    ci = lax.broadcasted_iota(jnp.int32, (SGU_CHUNK, SGU_CHUNK), 1)
    tril = ri >= ci
    for gi in range(SGU_GROUPS):
        cs = slice(gi * gw, (gi + 1) * gw)
        ws = jnp.where(tril, ws_ref[gi], 0.0).astype(BF16)
        bias = bs_ref[:, gi:gi + 1]
        for n in range(tm // SGU_CHUNK):
            rs = slice(n * SGU_CHUNK, (n + 1) * SGU_CHUNK)
            mixed = _dot(ws, v_ref[rs, cs]) + bias
            gated_ref[rs, cs] = (u_ref[rs, cs] * mixed).astype(BF16)
    o_ref[...] = x_ref[...] + _dot(gated_ref[...], wo_ref[...])


def _sgu(x2, g, w_uv, ln_g, ln_b, w_s, b_s, w_out, tm):
    N = x2.shape[0]
    row = lambda i: (i, 0)
    fixed = lambda i: (0, 0)
    return pl.pallas_call(
        functools.partial(_sgu_kernel, tm=tm),
        grid=(N // tm,),
        in_specs=[
            pl.BlockSpec((tm, D_MODEL), row),
            pl.BlockSpec((1, D_MODEL), fixed),
            pl.BlockSpec((D_MODEL, 2 * D_MODEL), fixed),
            pl.BlockSpec((1, D_MODEL), fixed), pl.BlockSpec((1, D_MODEL), fixed),
            pl.BlockSpec((SGU_GROUPS, SGU_CHUNK, SGU_CHUNK), lambda i: (0, 0, 0)),
            pl.BlockSpec((SGU_CHUNK, SGU_GROUPS), fixed),
            pl.BlockSpec((D_MODEL, D_MODEL), fixed),
        ],
        out_specs=pl.BlockSpec((tm, D_MODEL), row),
        out_shape=jax.ShapeDtypeStruct((N, D_MODEL), F32),
        scratch_shapes=[pltpu.VMEM((tm, D_MODEL), F32), pltpu.VMEM((tm, D_MODEL), BF16),
                        pltpu.VMEM((tm, D_MODEL), BF16)],
        compiler_params=_params("parallel"),
        name="sgu",
    )(x2, g.reshape(1, -1), w_uv.astype(BF16), ln_g.reshape(1, -1), ln_b.reshape(1, -1),
      w_s, b_s.T, w_out.astype(BF16))


def kernel(x, attn_norm, attn_w_in, gla_w_a2, gla_b_a, gla_head_g, attn_w_o, sgu_norm, sgu_w_uv, sgu_ln_g,
           sgu_ln_b, sgu_w_s, sgu_b_s, sgu_w_out, ffn_norm, ffn_w_up, ffn_conv_w, ffn_conv_b, ffn_w_down,
           final_norm):
    B, T, D = x.shape
    assert D == D_MODEL and T % TOKEN_TILE == 0 and T % DSA_K_TILE == 0
    topk = min(TOPK_MAX, T // 4)
    depth = ffn_norm.shape[0]
    x2 = x.reshape(B * T, D)
    w_up_bf, w_down_bf = ffn_w_up.astype(BF16), ffn_w_down.astype(BF16)
    for i in range(depth):
        j = i // 2
        if i % 2 == 0:
            aq, ak, av, ar, la, bq, iq, kk, v1t, misct = _in_proj(
                x2, attn_norm[j], attn_w_in[j], gla_w_a2[j], gla_b_a[j], T, TOKEN_TILE)
            oa = _gla(aq, ak, av, la, ar, gla_head_g[j], B, T, TOKEN_TILE)
            ob = _dsa(bq, iq, misct, kk, v1t, B, T, DSA_Q_TILE, DSA_K_TILE, topk)
            attn = (oa, ob, attn_w_o[j])
        else:
            attn = None
            x2 = _sgu(x2, sgu_norm[j], sgu_w_uv[j], sgu_ln_g[j], sgu_ln_b[j], sgu_w_s[j], sgu_b_s[j],
                      sgu_w_out[j], TOKEN_TILE)
        x2 = _conv_ffn(x2, ffn_norm[i], w_up_bf, ffn_conv_w[i], ffn_conv_b[i], w_down_bf,
                       final_norm, i, T, TOKEN_TILE, i == depth - 1, attn)
    return x2.reshape(B, T, D)
```

```python
import functools

import jax
import jax.numpy as jnp
import numpy as np
from jax import lax
from jax.experimental import pallas as pl
from jax.experimental.pallas import tpu as pltpu

D_MODEL = 1024
GLA_HEADS = 4
GLA_DK = 64
GLA_DV = 128
GLA_GATE_RANK = 16
GLA_GATE_TAU = 16.0
GLA_CHUNK = 64
DSA_HEADS = 8
DSA_HD = 64
IDX_HEADS = 4
IDX_HD = 64
TOPK_MAX = 256
ROPE_THETA = 500000.0
ROPE_FRAC_DIV = 4
SGU_CHUNK = 128
SGU_GROUPS = 8
D_FF = 2816
CONV_W = 3
EPS = 1e-6
LN_EPS = 1e-5

LANES = 128
SUBLANES = 8
VMEM_LIMIT = 56 * 1024 * 1024

TOKEN_TILE = 512
DSA_Q_TILE = 256
DSA_K_TILE = 512

GLA_QK = GLA_HEADS * GLA_DK
GLA_V = GLA_HEADS * GLA_DV
DSA_Q = DSA_HEADS * DSA_HD
IDX_Q = IDX_HEADS * IDX_HD

def _segments(*widths):
    edges = [0]
    for w in widths:
        edges.append(edges[-1] + w)
    return [(edges[i], edges[i + 1]) for i in range(len(widths))], edges[-1]


(_SEG_AQ, _SEG_AK, _SEG_AV, _SEG_AR, _SEG_BQ, _SEG_IQ, _SEG_KK, _SEG_MISC), IN_PAD = _segments(
    GLA_QK, GLA_QK, GLA_V, GLA_V, DSA_Q, IDX_Q, LANES, LANES)
MISC_ALR = 64
MISC_IW = 80

F32 = jnp.float32
BF16 = jnp.bfloat16
NEG_BIG = -1e30
LOG2E = 1.4426950408889634


def _dot(a, b):
    return jnp.dot(a, b, preferred_element_type=F32)


def _dot_nt(a, b):
    return lax.dot_general(a, b, (((1,), (1,)), ((), ())), preferred_element_type=F32)


def _dot_tn(a, b):
    return lax.dot_general(a, b, (((0,), (0,)), ((), ())), preferred_element_type=F32)


def _rmsnorm_rows(x, g):
    ms = jnp.mean(x * x, axis=-1, keepdims=True)
    return x * lax.rsqrt(ms + EPS) * g


def _params(*sem):
    return pltpu.CompilerParams(dimension_semantics=sem, vmem_limit_bytes=VMEM_LIMIT)


def _rope_slab(x, tab):
    half = DSA_HD // ROPE_FRAC_DIV // 2
    c = tab[:, 0:LANES]
    s_up = tab[:, LANES:2 * LANES]
    s_dn = tab[:, 2 * LANES:3 * LANES]
    return x * c + pltpu.roll(x, half, 1) * s_up + pltpu.roll(x, LANES - half, 1) * s_dn


def _in_proj_kernel(x_ref, g_ref, w_ref, wa2_ref, ba_ref, tab2_ref,
                    aq_ref, ak_ref, av_ref, ar_ref, la_ref, bq_ref, iq_ref, kk_ref, v1t_ref, misct_ref):
    h = _rmsnorm_rows(x_ref[...], g_ref[...]).astype(BF16)

    def seg(s):
        return _dot(h, w_ref[:, s[0]:s[1]])

    aq_ref[...] = (seg(_SEG_AQ) * (GLA_DK ** -0.5)).astype(aq_ref.dtype)
    ak_ref[...] = seg(_SEG_AK).astype(ak_ref.dtype)
    av_ref[...] = seg(_SEG_AV).astype(av_ref.dtype)
    ar_ref[...] = seg(_SEG_AR).astype(ar_ref.dtype)

    tab2 = tab2_ref[...]
    bq = seg(_SEG_BQ)
    for j in range((_SEG_BQ[1] - _SEG_BQ[0]) // LANES):
        sl = slice(j * LANES, (j + 1) * LANES)
        bq_ref[:, sl] = (_rope_slab(bq[:, sl], tab2) * (DSA_HD ** -0.5 * LOG2E)).astype(bq_ref.dtype)
    iq = seg(_SEG_IQ)
    for j in range((_SEG_IQ[1] - _SEG_IQ[0]) // LANES):
        sl = slice(j * LANES, (j + 1) * LANES)
        iq_ref[:, sl] = (_rope_slab(iq[:, sl], tab2) * (IDX_HD ** -0.5)).astype(iq_ref.dtype)
    kk_ref[...] = _rope_slab(seg(_SEG_KK), tab2).astype(kk_ref.dtype)
    misc = seg(_SEG_MISC)
    misct = misc.T
    misct_ref[...] = misct
    row = lax.broadcasted_iota(jnp.int32, misct.shape, 0)
    v1t = jnp.where(row < DSA_HD, misct, jnp.where(row == DSA_HD, 1.0, 0.0))
    v1t_ref[...] = v1t.astype(v1t_ref.dtype)
    z = _dot(misc.astype(BF16), wa2_ref[...]) + ba_ref[...]
    la_ref[...] = (jnp.minimum(z, 0.0) - jnp.log(1.0 + jnp.exp(-jnp.abs(z)))) * (1.0 / GLA_GATE_TAU)


def _rope_tables(T):
    rd = DSA_HD // ROPE_FRAC_DIV
    half = rd // 2
    pos = jnp.arange(T, dtype=F32)
    inv = jnp.power(ROPE_THETA, -(jnp.arange(half, dtype=F32) * 2.0 / rd))
    ang = pos[:, None] * inv[None, :]
    cs = jnp.concatenate([jnp.cos(ang), jnp.sin(ang)], axis=1)
    sel = np.zeros((2 * half, 3 * LANES), np.float32)
    one = np.zeros((1, 3 * LANES), np.float32)
    for head in range(LANES // DSA_HD):
        o = head * DSA_HD
        one[0, o + rd:o + DSA_HD] = 1.0
        for j in range(half):
            sel[j, o + j] = 1.0
            sel[j, o + half + j] = 1.0
            sel[half + j, LANES + o + half + j] = 1.0
            sel[half + j, 2 * LANES + o + j] = -1.0
    return jnp.dot(cs, jnp.asarray(sel), precision=lax.Precision.HIGHEST) + jnp.asarray(one)


def _in_proj(x2, g, w_in, w_a2, b_a, T, tm):
    N = x2.shape[0]
    widths = (GLA_QK, GLA_QK, GLA_V, GLA_V, GLA_GATE_RANK, DSA_Q, DSA_HD, DSA_HD, IDX_Q, IDX_HD, IDX_HEADS)
    offs = [0]
    for w in widths:
        offs.append(offs[-1] + w)
    w_in = w_in.astype(BF16)
    aq, ak, av, ar, alr, bq, bk, bv, iq, ik, iw = [w_in[:, offs[i]:offs[i + 1]] for i in range(11)]
    pad = jnp.zeros((D_MODEL, LANES - IDX_HD - GLA_GATE_RANK - IDX_HEADS), w_in.dtype)
    wp = jnp.concatenate([aq, ak, av, ar, bq, iq, bk, ik, bv, alr, iw * (IDX_HEADS ** -0.5), pad],
                         axis=1)
    wa2 = jnp.zeros((LANES, GLA_HEADS * GLA_DK), F32).at[MISC_ALR:MISC_ALR + GLA_GATE_RANK].set(w_a2).astype(BF16)
    tab2 = _rope_tables(T)
    nt = T // tm
    row = lambda i: (i, 0)
    fixed = lambda i: (0, 0)
    tabm = lambda i: (i % nt, 0)
    outs = [(GLA_QK, BF16), (GLA_QK, BF16), (GLA_V, BF16), (GLA_V, BF16), (GLA_QK, F32),
            (DSA_Q, BF16), (IDX_Q, BF16), (LANES, BF16)]
    col = lambda i: (0, i)
    return pl.pallas_call(
        _in_proj_kernel,
        grid=(N // tm,),
        in_specs=[
            pl.BlockSpec((tm, D_MODEL), row),
            pl.BlockSpec((1, D_MODEL), fixed),
            pl.BlockSpec((D_MODEL, IN_PAD), fixed),
            pl.BlockSpec((LANES, GLA_HEADS * GLA_DK), fixed),
            pl.BlockSpec((1, GLA_HEADS * GLA_DK), fixed),
            pl.BlockSpec((tm, 3 * LANES), tabm),
        ],
        out_specs=[pl.BlockSpec((tm, w), row) for w, _ in outs]
        + [pl.BlockSpec((LANES, tm), col), pl.BlockSpec((LANES, tm), col)],
        out_shape=[jax.ShapeDtypeStruct((N, w), d) for w, d in outs]
        + [jax.ShapeDtypeStruct((LANES, N), BF16), jax.ShapeDtypeStruct((LANES, N), F32)],
        compiler_params=_params("parallel"),
        name="in_proj",
    )(x2, g.reshape(1, -1), wp, wa2, b_a.reshape(1, -1), tab2)


def _gla_kernel(q_ref, k_ref, v_ref, la_ref, r_ref, hg_ref, o_ref, st_ref, *, n_chunks):
    C = GLA_CHUNK

    @pl.when(pl.program_id(1) == 0)
    def _():
        st_ref[...] = jnp.zeros_like(st_ref)

    ri = lax.broadcasted_iota(jnp.int32, (C, C), 0)
    ci = lax.broadcasted_iota(jnp.int32, (C, C), 1)
    tril = ri >= ci
    row_id = lax.broadcasted_iota(jnp.int32, (C, GLA_HEADS * GLA_DK), 0)
    hg = hg_ref[...]
    state = [st_ref[hh] for hh in range(GLA_HEADS)]

    for c in range(n_chunks):
        rows = slice(c * C, (c + 1) * C)
        la = la_ref[rows, :]
        b = la
        for sh in (1, 2, 4, 8, 16, 32):
            b = b + jnp.where(row_id >= sh, pltpu.roll(b, sh, 0), 0.0)
        b_mid = b[C // 2:C // 2 + 1, :]
        b_last = b[C - 1:C, :]
        q = q_ref[rows, :].astype(F32)
        k = k_ref[rows, :].astype(F32)
        qe = (q * jnp.exp(b - b_mid)).astype(BF16)
        ke = (k * jnp.exp(b_mid - b)).astype(BF16)
        kl = (k * jnp.exp(b_last - b)).astype(BF16)
        qb = (q * jnp.exp(b)).astype(BF16)
        dec = jnp.exp(b_last)
        for hh in range(GLA_HEADS):
            ks = slice(hh * GLA_DK, (hh + 1) * GLA_DK)
            vs = slice(hh * GLA_DV, (hh + 1) * GLA_DV)
            v = v_ref[rows, vs]
            att = jnp.where(tril, _dot_nt(qe[:, ks], ke[:, ks]), 0.0)
            o = _dot(att.astype(BF16), v) + _dot_nt(qb[:, ks], state[hh].astype(BF16))
            state[hh] = state[hh] * dec[:, ks] + _dot_tn(v, kl[:, ks])
            o = o * lax.rsqrt(jnp.mean(o * o, axis=-1, keepdims=True) + EPS) * hg
            r = r_ref[rows, vs].astype(F32)
            o_ref[rows, vs] = (o * (r * jax.nn.sigmoid(r))).astype(o_ref.dtype)
    for hh in range(GLA_HEADS):
        st_ref[hh] = state[hh]


def _gla(aq, ak, av, la, ar, head_g, B, T, tg):
    N = B * T
    nt = T // tg
    row = lambda b, i: (b * nt + i, 0)
    return pl.pallas_call(
        functools.partial(_gla_kernel, n_chunks=tg // GLA_CHUNK),
        grid=(B, nt),
        in_specs=[
            pl.BlockSpec((tg, GLA_QK), row), pl.BlockSpec((tg, GLA_QK), row), pl.BlockSpec((tg, GLA_V), row),
            pl.BlockSpec((tg, GLA_QK), row), pl.BlockSpec((tg, GLA_V), row),
            pl.BlockSpec((1, GLA_DV), lambda b, i: (0, 0)),
        ],
        out_specs=pl.BlockSpec((tg, GLA_V), row),
        out_shape=jax.ShapeDtypeStruct((N, GLA_V), BF16),
        scratch_shapes=[pltpu.VMEM((GLA_HEADS, GLA_DV, GLA_DK), F32)],
        compiler_params=_params("parallel", "arbitrary"),
        name="gla",
    )(aq, ak, av, la, ar, head_g.reshape(1, -1))


DSA_GROUP_LANES = 1024
MIN_NORMAL = 1.1754943508222875e-38
TINY_BRACKET = 2.0 ** -60
TINY_SCALE = 2.0 ** 64
SEARCH_WARMUP_ROUNDS = 5
SEARCH_INTERP_ROUNDS = 18
SEARCH_BISECT_ROUNDS = 150


def _dsa_kernel(bq_ref, iq_ref, qmisct_ref, kk_ref, v1t_ref, tri_ref, o_ref,
                sc_ref, s_ref, qs_ref, acc_ref, *, tq, tk, topk):
    qi = pl.program_id(1)
    q0 = qi * tq
    nkb = (q0 + tq + tk - 1) // tk
    qpos = q0 + lax.broadcasted_iota(jnp.int32, (1, tq), 1)
    key_iota = lax.broadcasted_iota(jnp.int32, (tk, tq), 0)
    H = DSA_HEADS
    S = SUBLANES

    def krows(kb):
        return pl.ds(pl.multiple_of(kb * tk, tk), tk)

    def over_blocks(body, carry, widths=(4, 2, 1)):
        start = 0
        for w in widths:
            def group(j, c, w=w):
                for i in range(w):
                    c = body(w * j + i, c)
                return c
            carry = lax.fori_loop(start // w, nkb // w, group, carry)
            start = (nkb // w) * w
        return carry

    def fold_rows(x, op):
        parts = [x[j * S:(j + 1) * S, :] for j in range(x.shape[0] // S)]
        while len(parts) > 1:
            parts = [op(parts[j], parts[j + 1]) for j in range(0, len(parts) - 1, 2)] + (
                [parts[-1]] if len(parts) % 2 else [])
        return parts[0]

    iw = [qmisct_ref[MISC_IW + h:MISC_IW + h + 1, :] for h in range(IDX_HEADS)]
    for h in range(IDX_HEADS):
        qs_ref[h * tq:(h + 1) * tq, :] = iq_ref[:, h * IDX_HD:(h + 1) * IDX_HD]

    def score_block(kb, mm):
        x = _dot_nt(kk_ref[krows(kb), IDX_HD:2 * IDX_HD], qs_ref[0:IDX_HEADS * tq, :])
        sc = iw[0] * jnp.maximum(x[:, 0:tq], 0.0)
        for h in range(1, IDX_HEADS):
            sc = sc + iw[h] * jnp.maximum(x[:, h * tq:(h + 1) * tq], 0.0)
        sc = sc + 0.0
        sc_ref[kb] = jnp.where(kb * tk + key_iota <= qpos, sc, -jnp.inf)
        return jnp.maximum(mm[0], fold_rows(sc, jnp.maximum)), jnp.minimum(mm[1], fold_rows(sc, jnp.minimum))

    mx8, mn8 = over_blocks(score_block, (jnp.full((S, tq), -jnp.inf, F32), jnp.full((S, tq), jnp.inf, F32)))
    hi_bound = jnp.max(mx8, axis=0, keepdims=True)
    lo_bound = jnp.min(mn8, axis=0, keepdims=True)

    def count_ge(cand):
        def body(kb, cnt):
            return cnt + fold_rows(jnp.where(sc_ref[kb] >= cand, 1, 0), jnp.add)
        cnt = over_blocks(body, jnp.zeros((S, tq), jnp.int32), widths=(2, 1))
        return jnp.sum(cnt, axis=0, keepdims=True)

    def midpoint(lo, hi):
        tiny = jnp.maximum(jnp.abs(lo), jnp.abs(hi)) < TINY_BRACKET
        up = jnp.where(tiny, TINY_SCALE, 1.0)
        return (0.5 * (lo * up) + 0.5 * (hi * up)) * jnp.where(tiny, 1.0 / TINY_SCALE, 1.0)

    def finished(lo, hi, c_lo):
        mid = midpoint(lo, hi)
        closed = jnp.where(mid <= lo, 1, jnp.where(mid >= hi, 1, 0))
        closed = jnp.where(lo == 0.0, jnp.where(hi == MIN_NORMAL, 1, closed), closed)
        return jnp.where(c_lo == topk, 1, closed)

    def step(st, pick):
        lo, hi, c_lo, c_hi, g_lo, g_hi, last = st
        done = finished(lo, hi, c_lo)
        cand = pick(lo, hi, g_lo, g_hi)
        cand = jnp.where(cand > lo, jnp.where(cand < hi, cand, midpoint(lo, hi)), midpoint(lo, hi))
        cand = jnp.where(done > 0, lo, cand)
        c = count_ge(cand)
        g = c.astype(F32) - (topk - 0.5)
        up = jnp.where(done > 0, 0, jnp.where(c >= topk, 1, 0))
        dn = jnp.where(done > 0, 0, jnp.where(c >= topk, 0, 1))
        g_hi = jnp.where(up * last > 0, g_hi * 0.5, g_hi)
        g_lo = jnp.where(dn * last < 0, g_lo * 0.5, g_lo)
        return (jnp.where(up > 0, cand, lo), jnp.where(dn > 0, cand, hi),
                jnp.where(up > 0, c, c_lo), jnp.where(dn > 0, c, c_hi),
                jnp.where(up > 0, g, g_lo), jnp.where(dn > 0, g, g_hi),
                jnp.where(up > 0, 1, jnp.where(dn > 0, -1, last)))

    def interpolate(lo, hi, g_lo, g_hi):
        return lo + (hi - lo) * (g_lo / (g_lo - g_hi))

    def bisect(lo, hi, g_lo, g_hi):
        return midpoint(lo, hi)

    def unfinished(st):
        return jnp.max(jnp.where(finished(st[0], st[1], st[2]) > 0, 0.0, 1.0)) > 0.0

    def snap(lo, hi, g_lo, g_hi):
        def body(kb, mx):
            sc = sc_ref[kb]
            return jnp.maximum(mx, fold_rows(jnp.where(sc < hi, sc, -jnp.inf), jnp.maximum))
        mx = over_blocks(body, jnp.full((S, tq), -jnp.inf, F32), widths=(2, 1))
        return jnp.max(mx, axis=0, keepdims=True)

    def one_round(st, picks):
        for pick in picks:
            st = step(st, pick)
        return st

    def search(st, picks, max_rounds):
        return lax.while_loop(lambda c: jnp.logical_and(c[0] < max_rounds, unfinished(c[1])),
                              lambda c: (c[0] + 1, one_round(c[1], picks)), (jnp.int32(0), st))[1]

    n_causal = qpos + 1
    small = n_causal <= topk
    lo0 = jnp.where(small, -jnp.inf, lo_bound)
    hi0 = hi_bound + jnp.maximum(jnp.abs(hi_bound) * 1e-6, 1e-37)
    c_lo0 = jnp.where(small, topk, n_causal)
    st = (lo0, hi0, c_lo0, jnp.zeros((1, tq), jnp.int32), c_lo0.astype(F32) - (topk - 0.5),
          jnp.full((1, tq), 0.5 - topk, F32), jnp.zeros((1, tq), jnp.int32))
    st = step(st, lambda lo, hi, g_lo, g_hi: jnp.zeros_like(lo))
    st = step(st, lambda lo, hi, g_lo, g_hi: jnp.full_like(lo, MIN_NORMAL))
    st = lax.fori_loop(0, SEARCH_WARMUP_ROUNDS, lambda _, st: one_round(st, (interpolate, interpolate)), st)
    st = search(st, (snap, interpolate), SEARCH_INTERP_ROUNDS)
    st = search(st, (bisect, bisect), SEARCH_BISECT_ROUNDS)
    thr, c_lo, c_hi = st[0], st[2], st[3]
    need = jnp.where(small, 0, jnp.where(c_lo == topk, topk, topk - c_hi)).astype(F32)

    tri = tri_ref[...]
    GL = s_ref.shape[3]
    HG = GL // tq
    NG = H // HG
    for h in range(H):
        qs_ref[h * tq:(h + 1) * tq, :] = bq_ref[:, h * DSA_HD:(h + 1) * DSA_HD]

    def qk_part(grp, kb, mx, base):
        sc = sc_ref[kb]
        if grp == 0:
            eq = jnp.where(sc == thr, 1.0, 0.0).astype(BF16)
            rank = _dot(tri, eq) + base
            bias = jnp.where(sc > thr, 0.0,
                             jnp.where(sc == thr, jnp.where(rank <= need, 0.0, NEG_BIG), NEG_BIG))
            sc_ref[kb] = bias
            base = rank[tk - 1:tk, :]
        else:
            bias = sc
        s = _dot_nt(kk_ref[krows(kb), 0:DSA_HD], qs_ref[grp * GL:(grp + 1) * GL, :])
        s = s + jnp.concatenate([bias] * HG, axis=1)
        s_ref[grp % 2, kb] = s
        return jnp.maximum(mx, fold_rows(s, jnp.maximum)), base

    def pv_part(grp, kb, m):
        p = jnp.exp2(s_ref[grp % 2, kb] - m).astype(BF16)
        acc_ref[grp % 2] += _dot(v1t_ref[:, krows(kb)], p)

    def finish(grp):
        acc = acc_ref[grp % 2]
        out = acc[0:DSA_HD, :] / acc[DSA_HD:DSA_HD + 1, :]
        for j in range(HG):
            h = grp * HG + j
            o_ref[:, h * DSA_HD:(h + 1) * DSA_HD] = out[:, j * tq:(j + 1) * tq].T.astype(o_ref.dtype)

    mx0 = jnp.full((S, GL), NEG_BIG, F32)
    m_prev = None
    for grp in range(NG + 1):
        if grp > 0:
            acc_ref[(grp - 1) % 2] = jnp.zeros(acc_ref.shape[1:], F32)

        def body(kb, carry, grp=grp, m_prev=m_prev):
            mx, base = carry
            if grp > 0:
                pv_part(grp - 1, kb, m_prev)
            if grp < NG:
                mx, base = qk_part(grp, kb, mx, base)
            return mx, base

        mx, _ = over_blocks(body, (mx0, jnp.zeros((1, tq), F32)))
        if grp > 0:
            finish(grp - 1)
        m_prev = jnp.max(mx, axis=0, keepdims=True)


def _dsa(bq, iq, misct, kk, v1t, B, T, tq, tk, topk):
    N = B * T
    nq = T // tq
    qrow = lambda b, i: (b * nq + i, 0)
    r = lax.broadcasted_iota(jnp.int32, (tk, tk), 0)
    c = lax.broadcasted_iota(jnp.int32, (tk, tk), 1)
    tri = jnp.where(c <= r, 1.0, 0.0).astype(BF16)
    return pl.pallas_call(
        functools.partial(_dsa_kernel, tq=tq, tk=tk, topk=topk),
        grid=(B, nq),
        in_specs=[
            pl.BlockSpec((tq, DSA_Q), qrow), pl.BlockSpec((tq, IDX_Q), qrow),
            pl.BlockSpec((LANES, tq), lambda b, i: (0, b * nq + i)),
            pl.BlockSpec((T, LANES), lambda b, i: (b, 0)),
            pl.BlockSpec((LANES, T), lambda b, i: (0, b)),
            pl.BlockSpec((tk, tk), lambda b, i: (0, 0)),
        ],
        out_specs=pl.BlockSpec((tq, DSA_Q), qrow),
        out_shape=jax.ShapeDtypeStruct((N, DSA_Q), BF16),
        scratch_shapes=[
            pltpu.VMEM((T // tk, tk, tq), F32),
            pltpu.VMEM((2, T // tk, tk, DSA_GROUP_LANES), F32),
            pltpu.VMEM((DSA_HEADS * tq, DSA_HD), BF16),
            pltpu.VMEM((2, LANES, DSA_GROUP_LANES), F32),
        ],
        compiler_params=_params("parallel", "arbitrary"),
        name="dsa",
    )(bq, iq, misct, kk, v1t, tri)


FFN_CHUNKS = (768, 768, 768, 512)


def _ffn_kernel(*refs, tm, tiles_per_seq, final_norm, attn_out, rb):
    if attn_out:
        oa_ref, ob_ref, wa_ref, wb_ref = refs[:4]
        refs = refs[4:]
    x_ref, g_ref, wup_ref, cw_ref, cb_ref, wd_ref, fg_ref, o_ref, h_ref, act_ref, acc_ref = refs[:11]
    ab_refs = refs[11:]
    i = pl.program_id(0)
    H = SUBLANES
    n_c = len(FFN_CHUNKS)
    ag_refs, au_refs = ab_refs[:n_c], ab_refs[n_c:]
    offs = [sum(FFN_CHUNKS[:c]) for c in range(n_c)]
    first = (i % tiles_per_seq) == 0

    @pl.when(first)
    def _():
        for ref in ab_refs:
            ref[0:H, :] = jnp.zeros((H, ref.shape[1]), F32)

    @pl.when(jnp.logical_not(first))
    def _():
        for ref in ab_refs:
            ref[0:H, :] = ref[tm:tm + H, :]

    def up_proj(c):
        o, w = offs[c], FFN_CHUNKS[c]
        ag_refs[c][H:H + tm, :] = _dot(hb, wup_ref[:, o:o + w])
        au_refs[c][H:H + tm, :] = _dot(hb, wup_ref[:, D_FF + o:D_FF + o + w])

    def gate_act(c):
        o, w = offs[c], FFN_CHUNKS[c]
        cwg, cwu = cw_ref[:, o:o + w], cw_ref[:, D_FF + o:D_FF + o + w]
        cbg, cbu = cb_ref[:, o:o + w], cb_ref[:, D_FF + o:D_FF + o + w]
        for r0 in range(0, tm, rb):
            wg = ag_refs[c][r0:r0 + rb + H, :]
            wu = au_refs[c][r0:r0 + rb + H, :]
            gate, up = cbg, cbu
            for j in range(CONV_W):
                s0 = H - (CONV_W - 1) + j
                gate = gate + cwg[j:j + 1, :] * wg[s0:s0 + rb, :]
                up = up + cwu[j:j + 1, :] * wu[s0:s0 + rb, :]
            act_ref[r0:r0 + rb, o:o + w] = (gate * jax.nn.sigmoid(gate) * up).astype(BF16)

    def down_proj(c):
        o, w = offs[c], FFN_CHUNKS[c]
        d = _dot(act_ref[:, o:o + w], wd_ref[o:o + w, :])
        if c == 0:
            acc_ref[...] = d
        else:
            acc_ref[...] += d

    if attn_out:
        o_ref[...] = x_ref[...] + _dot(oa_ref[...], wa_ref[...]) + _dot(ob_ref[...], wb_ref[...])
        xin_ref = o_ref
    else:
        xin_ref = x_ref
    h_ref[...] = _rmsnorm_rows(xin_ref[...], g_ref[...]).astype(BF16)
    hb = h_ref[...]
    up_proj(0)
    for c in range(n_c):
        if c + 1 < n_c:
            up_proj(c + 1)
        gate_act(c)
        down_proj(c)
    y = xin_ref[...] + acc_ref[...]
    if final_norm:
        y = _rmsnorm_rows(y, fg_ref[...])
    o_ref[...] = y


def _conv_ffn(x2, g, w_up, conv_w, conv_b, w_down, final_g, layer, T, tm, final_norm, attn=None):
    N = x2.shape[0]
    row = lambda i: (i, 0)
    fixed = lambda i: (0, 0)
    def wspec(shape):
        return pl.BlockSpec((None,) + shape, lambda i: (layer, 0, 0), pipeline_mode=pl.Buffered(1))
    attn_args, attn_specs = (), []
    if attn is not None:
        oa, ob, w_o = attn
        half = w_o.shape[0] // 2
        w_o = w_o.astype(BF16)
        attn_args = (oa, ob, w_o[:half], w_o[half:])
        attn_specs = [pl.BlockSpec((tm, half), row), pl.BlockSpec((tm, half), row),
                      pl.BlockSpec((half, D_MODEL), fixed), pl.BlockSpec((half, D_MODEL), fixed)]
    return pl.pallas_call(
        functools.partial(_ffn_kernel, tm=tm, tiles_per_seq=T // tm, final_norm=final_norm,
                          attn_out=attn is not None, rb=64),
        grid=(N // tm,),
        in_specs=attn_specs + [
            pl.BlockSpec((tm, D_MODEL), row),
            pl.BlockSpec((1, D_MODEL), fixed),
            wspec((D_MODEL, 2 * D_FF)),
            pl.BlockSpec((CONV_W, 2 * D_FF), fixed),
            pl.BlockSpec((1, 2 * D_FF), fixed),
            wspec((D_FF, D_MODEL)),
            pl.BlockSpec((1, D_MODEL), fixed),
        ],
        out_specs=pl.BlockSpec((tm, D_MODEL), row),
        out_shape=jax.ShapeDtypeStruct((N, D_MODEL), F32),
        scratch_shapes=[
            pltpu.VMEM((tm, D_MODEL), BF16),
            pltpu.VMEM((tm, D_FF), BF16),
            pltpu.VMEM((tm, D_MODEL), F32),
        ] + [pltpu.VMEM((tm + SUBLANES, w), F32) for w in FFN_CHUNKS] * 2,
        compiler_params=_params("arbitrary"),
        name="conv_ffn",
    )(*attn_args, x2, g.reshape(1, -1), w_up, conv_w, conv_b.reshape(1, -1), w_down, final_g.reshape(1, -1))


def _sgu_kernel(x_ref, g_ref, wuv_ref, lng_ref, lnb_ref, ws_ref, bs_ref, wo_ref, o_ref,
                u_ref, v_ref, gated_ref, *, tm):
    W = D_MODEL
    gw = W // SGU_GROUPS
    h = _rmsnorm_rows(x_ref[...], g_ref[...]).astype(BF16)
    u_ref[...] = jax.nn.gelu(_dot(h, wuv_ref[:, 0:W]))
    v = jax.nn.gelu(_dot(h, wuv_ref[:, W:2 * W]))
    mu = jnp.mean(v, axis=-1, keepdims=True)
    vc = v - mu
    v = vc * lax.rsqrt(jnp.mean(vc * vc, axis=-1, keepdims=True) + LN_EPS) * lng_ref[...] + lnb_ref[...]
    v_ref[...] = v.astype(BF16)
    ri = lax.broadcasted_iota(jnp.int32, (SGU_CHUNK, SGU_CHUNK), 0)
    ci = lax.broadcasted_iota(jnp.int32, (SGU_CHUNK, SGU_CHUNK), 1)
    tril = ri >= ci
    for gi in range(SGU_GROUPS):
        cs = slice(gi * gw, (gi + 1) * gw)
        ws = jnp.where(tril, ws_ref[gi], 0.0).astype(BF16)
        bias = bs_ref[:, gi:gi + 1]
        for n in range(tm // SGU_CHUNK):
            rs = slice(n * SGU_CHUNK, (n + 1) * SGU_CHUNK)
            mixed = _dot(ws, v_ref[rs, cs]) + bias
            gated_ref[rs, cs] = (u_ref[rs, cs] * mixed).astype(BF16)
    o_ref[...] = x_ref[...] + _dot(gated_ref[...], wo_ref[...])


def _sgu(x2, g, w_uv, ln_g, ln_b, w_s, b_s, w_out, tm):
    N = x2.shape[0]
    row = lambda i: (i, 0)
    fixed = lambda i: (0, 0)
    return pl.pallas_call(
        functools.partial(_sgu_kernel, tm=tm),
        grid=(N // tm,),
        in_specs=[
            pl.BlockSpec((tm, D_MODEL), row),
            pl.BlockSpec((1, D_MODEL), fixed),
            pl.BlockSpec((D_MODEL, 2 * D_MODEL), fixed),
            pl.BlockSpec((1, D_MODEL), fixed), pl.BlockSpec((1, D_MODEL), fixed),
            pl.BlockSpec((SGU_GROUPS, SGU_CHUNK, SGU_CHUNK), lambda i: (0, 0, 0)),
            pl.BlockSpec((SGU_CHUNK, SGU_GROUPS), fixed),
            pl.BlockSpec((D_MODEL, D_MODEL), fixed),
        ],
        out_specs=pl.BlockSpec((tm, D_MODEL), row),
        out_shape=jax.ShapeDtypeStruct((N, D_MODEL), F32),
        scratch_shapes=[pltpu.VMEM((tm, D_MODEL), F32), pltpu.VMEM((tm, D_MODEL), BF16),
                        pltpu.VMEM((tm, D_MODEL), BF16)],
        compiler_params=_params("parallel"),
        name="sgu",
    )(x2, g.reshape(1, -1), w_uv.astype(BF16), ln_g.reshape(1, -1), ln_b.reshape(1, -1),
      w_s, b_s.T, w_out.astype(BF16))


def kernel(x, attn_norm, attn_w_in, gla_w_a2, gla_b_a, gla_head_g, attn_w_o, sgu_norm, sgu_w_uv, sgu_ln_g,
           sgu_ln_b, sgu_w_s, sgu_b_s, sgu_w_out, ffn_norm, ffn_w_up, ffn_conv_w, ffn_conv_b, ffn_w_down,
           final_norm):
    B, T, D = x.shape
    assert D == D_MODEL and T % TOKEN_TILE == 0 and T % DSA_K_TILE == 0
    topk = min(TOPK_MAX, T // 4)
    depth = ffn_norm.shape[0]
    x2 = x.reshape(B * T, D)
    w_up_bf, w_down_bf = ffn_w_up.astype(BF16), ffn_w_down.astype(BF16)
    for i in range(depth):
        j = i // 2
        if i % 2 == 0:
            aq, ak, av, ar, la, bq, iq, kk, v1t, misct = _in_proj(
                x2, attn_norm[j], attn_w_in[j], gla_w_a2[j], gla_b_a[j], T, TOKEN_TILE)
            oa = _gla(aq, ak, av, la, ar, gla_head_g[j], B, T, TOKEN_TILE)
            ob = _dsa(bq, iq, misct, kk, v1t, B, T, DSA_Q_TILE, DSA_K_TILE, topk)
            attn = (oa, ob, attn_w_o[j])
        else:
            attn = None
            x2 = _sgu(x2, sgu_norm[j], sgu_w_uv[j], sgu_ln_g[j], sgu_ln_b[j], sgu_w_s[j], sgu_b_s[j],
                      sgu_w_out[j], TOKEN_TILE)
        x2 = _conv_ffn(x2, ffn_norm[i], w_up_bf, ffn_conv_w[i], ffn_conv_b[i], w_down_bf,
                       final_norm, i, T, TOKEN_TILE, i == depth - 1, attn)
    return x2.reshape(B, T, D)
```

```python
import functools

import jax
import jax.numpy as jnp
import numpy as np
from jax import lax
from jax.experimental import pallas as pl
from jax.experimental.pallas import tpu as pltpu

D_MODEL = 1024
GLA_HEADS = 4
GLA_DK = 64
GLA_DV = 128
GLA_GATE_RANK = 16
GLA_GATE_TAU = 16.0
GLA_CHUNK = 64
DSA_HEADS = 8
DSA_HD = 64
IDX_HEADS = 4
IDX_HD = 64
TOPK_MAX = 256
ROPE_THETA = 500000.0
ROPE_FRAC_DIV = 4
SGU_CHUNK = 128
SGU_GROUPS = 8
D_FF = 2816
CONV_W = 3
EPS = 1e-6
LN_EPS = 1e-5

LANES = 128
SUBLANES = 8
VMEM_LIMIT = 56 * 1024 * 1024

TOKEN_TILE = 512
IN_PROJ_TILE = 1024
DSA_Q_TILE = 256
DSA_K_TILE = 512

GLA_QK = GLA_HEADS * GLA_DK
GLA_V = GLA_HEADS * GLA_DV
DSA_Q = DSA_HEADS * DSA_HD
IDX_Q = IDX_HEADS * IDX_HD

def _segments(*widths):
    edges = [0]
    for w in widths:
        edges.append(edges[-1] + w)
    return [(edges[i], edges[i + 1]) for i in range(len(widths))], edges[-1]


(_SEG_AQ, _SEG_AK, _SEG_AV, _SEG_AR, _SEG_BQ, _SEG_IQ, _SEG_KK, _SEG_MISC), IN_PAD = _segments(
    GLA_QK, GLA_QK, GLA_V, GLA_V, DSA_Q, IDX_Q, LANES, LANES)
MISC_ALR = 64
MISC_IW = 80

F32 = jnp.float32
BF16 = jnp.bfloat16
NEG_BIG = -1e30
LOG2E = 1.4426950408889634


def _dot(a, b):
    return jnp.dot(a, b, preferred_element_type=F32)


def _dot_nt(a, b):
    return lax.dot_general(a, b, (((1,), (1,)), ((), ())), preferred_element_type=F32)


def _dot_tn(a, b):
    return lax.dot_general(a, b, (((0,), (0,)), ((), ())), preferred_element_type=F32)


def _rmsnorm_rows(x, g):
    ms = jnp.mean(x * x, axis=-1, keepdims=True)
    return x * lax.rsqrt(ms + EPS) * g


def _params(*sem):
    return pltpu.CompilerParams(dimension_semantics=sem, vmem_limit_bytes=VMEM_LIMIT)


def _rope_slab(x, tab):
    half = DSA_HD // ROPE_FRAC_DIV // 2
    c = tab[:, 0:LANES]
    s_up = tab[:, LANES:2 * LANES]
    s_dn = tab[:, 2 * LANES:3 * LANES]
    return x * c + pltpu.roll(x, half, 1) * s_up + pltpu.roll(x, LANES - half, 1) * s_dn


def _in_proj_kernel(x_ref, g_ref, w_ref, wa2_ref, ba_ref, tab2_ref,
                    aq_ref, ak_ref, av_ref, ar_ref, la_ref, bq_ref, iq_ref, kk_ref, v1t_ref, misct_ref):
    h = _rmsnorm_rows(x_ref[...], g_ref[...]).astype(BF16)

    def seg(s):
        return _dot(h, w_ref[:, s[0]:s[1]])

    aq_ref[...] = (seg(_SEG_AQ) * (GLA_DK ** -0.5)).astype(aq_ref.dtype)
    ak_ref[...] = seg(_SEG_AK).astype(ak_ref.dtype)
    av_ref[...] = seg(_SEG_AV).astype(av_ref.dtype)
    ar_ref[...] = seg(_SEG_AR).astype(ar_ref.dtype)

    tab2 = tab2_ref[...]
    bq = seg(_SEG_BQ)
    for j in range((_SEG_BQ[1] - _SEG_BQ[0]) // LANES):
        sl = slice(j * LANES, (j + 1) * LANES)
        bq_ref[:, sl] = (_rope_slab(bq[:, sl], tab2) * (DSA_HD ** -0.5 * LOG2E)).astype(bq_ref.dtype)
    iq = seg(_SEG_IQ)
    for j in range((_SEG_IQ[1] - _SEG_IQ[0]) // LANES):
        sl = slice(j * LANES, (j + 1) * LANES)
        iq_ref[:, sl] = (_rope_slab(iq[:, sl], tab2) * (IDX_HD ** -0.5)).astype(iq_ref.dtype)
    kk_ref[...] = _rope_slab(seg(_SEG_KK), tab2).astype(kk_ref.dtype)
    misc = seg(_SEG_MISC)
    misct = misc.T
    misct_ref[...] = misct
    row = lax.broadcasted_iota(jnp.int32, misct.shape, 0)
    v1t = jnp.where(row < DSA_HD, misct, jnp.where(row == DSA_HD, 1.0, 0.0))
    v1t_ref[...] = v1t.astype(v1t_ref.dtype)
    z = _dot(misc.astype(BF16), wa2_ref[...]) + ba_ref[...]
    la_ref[...] = (jnp.minimum(z, 0.0) - jnp.log(1.0 + jnp.exp(-jnp.abs(z)))) * (1.0 / GLA_GATE_TAU)


def _rope_tables(T):
    rd = DSA_HD // ROPE_FRAC_DIV
    half = rd // 2
    pos = jnp.arange(T, dtype=F32)
    inv = jnp.power(ROPE_THETA, -(jnp.arange(half, dtype=F32) * 2.0 / rd))
    ang = pos[:, None] * inv[None, :]
    cs = jnp.concatenate([jnp.cos(ang), jnp.sin(ang)], axis=1)
    sel = np.zeros((2 * half, 3 * LANES), np.float32)
    one = np.zeros((1, 3 * LANES), np.float32)
    for head in range(LANES // DSA_HD):
        o = head * DSA_HD
        one[0, o + rd:o + DSA_HD] = 1.0
        for j in range(half):
            sel[j, o + j] = 1.0
            sel[j, o + half + j] = 1.0
            sel[half + j, LANES + o + half + j] = 1.0
            sel[half + j, 2 * LANES + o + j] = -1.0
    return jnp.dot(cs, jnp.asarray(sel), precision=lax.Precision.HIGHEST) + jnp.asarray(one)


def _in_proj(x2, g, w_in, w_a2, b_a, T, tm):
    N = x2.shape[0]
    widths = (GLA_QK, GLA_QK, GLA_V, GLA_V, GLA_GATE_RANK, DSA_Q, DSA_HD, DSA_HD, IDX_Q, IDX_HD, IDX_HEADS)
    offs = [0]
    for w in widths:
        offs.append(offs[-1] + w)
    w_in = w_in.astype(BF16)
    aq, ak, av, ar, alr, bq, bk, bv, iq, ik, iw = [w_in[:, offs[i]:offs[i + 1]] for i in range(11)]
    pad = jnp.zeros((D_MODEL, LANES - IDX_HD - GLA_GATE_RANK - IDX_HEADS), w_in.dtype)
    wp = jnp.concatenate([aq, ak, av, ar, bq, iq, bk, ik, bv, alr, iw * (IDX_HEADS ** -0.5), pad],
                         axis=1)
    wa2 = jnp.zeros((LANES, GLA_HEADS * GLA_DK), F32).at[MISC_ALR:MISC_ALR + GLA_GATE_RANK].set(w_a2).astype(BF16)
    tab2 = _rope_tables(T)
    nt = T // tm
    row = lambda i: (i, 0)
    fixed = lambda i: (0, 0)
    tabm = lambda i: (i % nt, 0)
    outs = [(GLA_QK, BF16), (GLA_QK, BF16), (GLA_V, BF16), (GLA_V, BF16), (GLA_QK, F32),
            (DSA_Q, BF16), (IDX_Q, BF16), (LANES, BF16)]
    col = lambda i: (0, i)
    return pl.pallas_call(
        _in_proj_kernel,
        grid=(N // tm,),
        in_specs=[
            pl.BlockSpec((tm, D_MODEL), row),
            pl.BlockSpec((1, D_MODEL), fixed),
            pl.BlockSpec((D_MODEL, IN_PAD), fixed),
            pl.BlockSpec((LANES, GLA_HEADS * GLA_DK), fixed),
            pl.BlockSpec((1, GLA_HEADS * GLA_DK), fixed),
            pl.BlockSpec((tm, 3 * LANES), tabm),
        ],
        out_specs=[pl.BlockSpec((tm, w), row) for w, _ in outs]
        + [pl.BlockSpec((LANES, tm), col), pl.BlockSpec((LANES, tm), col)],
        out_shape=[jax.ShapeDtypeStruct((N, w), d) for w, d in outs]
        + [jax.ShapeDtypeStruct((LANES, N), BF16), jax.ShapeDtypeStruct((LANES, N), F32)],
        compiler_params=_params("parallel"),
        name="in_proj",
    )(x2, g.reshape(1, -1), wp, wa2, b_a.reshape(1, -1), tab2)


def _gla_kernel(q_ref, k_ref, v_ref, la_ref, r_ref, hg_ref, o_ref, st_ref, *, n_chunks):
    C = GLA_CHUNK

    @pl.when(pl.program_id(1) == 0)
    def _():
        st_ref[...] = jnp.zeros_like(st_ref)

    ri = lax.broadcasted_iota(jnp.int32, (C, C), 0)
    ci = lax.broadcasted_iota(jnp.int32, (C, C), 1)
    tril = ri >= ci
    row_id = lax.broadcasted_iota(jnp.int32, (C, GLA_HEADS * GLA_DK), 0)
    hg = hg_ref[...]
    state = [st_ref[hh] for hh in range(GLA_HEADS)]

    for c in range(n_chunks):
        rows = slice(c * C, (c + 1) * C)
        la = la_ref[rows, :]
        b = la
        for sh in (1, 2, 4, 8, 16, 32):
            b = b + jnp.where(row_id >= sh, pltpu.roll(b, sh, 0), 0.0)
        b_mid = b[C // 2:C // 2 + 1, :]
        b_last = b[C - 1:C, :]
        q = q_ref[rows, :].astype(F32)
        k = k_ref[rows, :].astype(F32)
        qe = (q * jnp.exp(b - b_mid)).astype(BF16)
        ke = (k * jnp.exp(b_mid - b)).astype(BF16)
        kl = (k * jnp.exp(b_last - b)).astype(BF16)
        qb = (q * jnp.exp(b)).astype(BF16)
        dec = jnp.exp(b_last)
        for hh in range(GLA_HEADS):
            ks = slice(hh * GLA_DK, (hh + 1) * GLA_DK)
            vs = slice(hh * GLA_DV, (hh + 1) * GLA_DV)
            v = v_ref[rows, vs]
            att = jnp.where(tril, _dot_nt(qe[:, ks], ke[:, ks]), 0.0)
            o = _dot(att.astype(BF16), v) + _dot_nt(qb[:, ks], state[hh].astype(BF16))
            state[hh] = state[hh] * dec[:, ks] + _dot_tn(v, kl[:, ks])
            o = o * lax.rsqrt(jnp.mean(o * o, axis=-1, keepdims=True) + EPS) * hg
            r = r_ref[rows, vs].astype(F32)
            o_ref[rows, vs] = (o * (r * jax.nn.sigmoid(r))).astype(o_ref.dtype)
    for hh in range(GLA_HEADS):
        st_ref[hh] = state[hh]


def _gla(aq, ak, av, la, ar, head_g, B, T, tg):
    N = B * T
    nt = T // tg
    row = lambda b, i: (b * nt + i, 0)
    return pl.pallas_call(
        functools.partial(_gla_kernel, n_chunks=tg // GLA_CHUNK),
        grid=(B, nt),
        in_specs=[
            pl.BlockSpec((tg, GLA_QK), row), pl.BlockSpec((tg, GLA_QK), row), pl.BlockSpec((tg, GLA_V), row),
            pl.BlockSpec((tg, GLA_QK), row), pl.BlockSpec((tg, GLA_V), row),
            pl.BlockSpec((1, GLA_DV), lambda b, i: (0, 0)),
        ],
        out_specs=pl.BlockSpec((tg, GLA_V), row),
        out_shape=jax.ShapeDtypeStruct((N, GLA_V), BF16),
        scratch_shapes=[pltpu.VMEM((GLA_HEADS, GLA_DV, GLA_DK), F32)],
        compiler_params=_params("parallel", "arbitrary"),
        name="gla",
    )(aq, ak, av, la, ar, head_g.reshape(1, -1))


DSA_GROUP_LANES = 1024
MIN_NORMAL = 1.1754943508222875e-38
TINY_BRACKET = 2.0 ** -60
TINY_SCALE = 2.0 ** 64
SEARCH_WARMUP_ROUNDS = 5
SEARCH_INTERP_ROUNDS = 18
SEARCH_BISECT_ROUNDS = 150


def _dsa_kernel(bq_ref, iq_ref, qmisct_ref, kk_ref, v1t_ref, tri_ref, o_ref,
                sc_ref, s_ref, qs_ref, acc_ref, *, tq, tk, topk):
    qi = pl.program_id(1)
    q0 = qi * tq
    nkb = (q0 + tq + tk - 1) // tk
    qpos = q0 + lax.broadcasted_iota(jnp.int32, (1, tq), 1)
    key_iota = lax.broadcasted_iota(jnp.int32, (tk, tq), 0)
    H = DSA_HEADS
    S = SUBLANES

    def krows(kb):
        return pl.ds(pl.multiple_of(kb * tk, tk), tk)

    def over_blocks(body, carry, widths=(4, 2, 1)):
        start = 0
        for w in widths:
            def group(j, c, w=w):
                for i in range(w):
                    c = body(w * j + i, c)
                return c
            carry = lax.fori_loop(start // w, nkb // w, group, carry)
            start = (nkb // w) * w
        return carry

    def fold_rows(x, op):
        parts = [x[j * S:(j + 1) * S, :] for j in range(x.shape[0] // S)]
        while len(parts) > 1:
            parts = [op(parts[j], parts[j + 1]) for j in range(0, len(parts) - 1, 2)] + (
                [parts[-1]] if len(parts) % 2 else [])
        return parts[0]

    iw = [qmisct_ref[MISC_IW + h:MISC_IW + h + 1, :] for h in range(IDX_HEADS)]
    for h in range(IDX_HEADS):
        qs_ref[h * tq:(h + 1) * tq, :] = iq_ref[:, h * IDX_HD:(h + 1) * IDX_HD]

    def score_block(kb, mm):
        x = _dot_nt(kk_ref[krows(kb), IDX_HD:2 * IDX_HD], qs_ref[0:IDX_HEADS * tq, :])
        sc = iw[0] * jnp.maximum(x[:, 0:tq], 0.0)
        for h in range(1, IDX_HEADS):
            sc = sc + iw[h] * jnp.maximum(x[:, h * tq:(h + 1) * tq], 0.0)
        sc = sc + 0.0
        sc_ref[kb] = jnp.where(kb * tk + key_iota <= qpos, sc, -jnp.inf)
        return jnp.maximum(mm[0], fold_rows(sc, jnp.maximum)), jnp.minimum(mm[1], fold_rows(sc, jnp.minimum))

    mx8, mn8 = over_blocks(score_block, (jnp.full((S, tq), -jnp.inf, F32), jnp.full((S, tq), jnp.inf, F32)))
    hi_bound = jnp.max(mx8, axis=0, keepdims=True)
    lo_bound = jnp.min(mn8, axis=0, keepdims=True)

    def count_ge(cand):
        def body(kb, cnt):
            return cnt + fold_rows(jnp.where(sc_ref[kb] >= cand, 1, 0), jnp.add)
        cnt = over_blocks(body, jnp.zeros((S, tq), jnp.int32), widths=(2, 1))
        return jnp.sum(cnt, axis=0, keepdims=True)

    def midpoint(lo, hi):
        tiny = jnp.maximum(jnp.abs(lo), jnp.abs(hi)) < TINY_BRACKET
        up = jnp.where(tiny, TINY_SCALE, 1.0)
        return (0.5 * (lo * up) + 0.5 * (hi * up)) * jnp.where(tiny, 1.0 / TINY_SCALE, 1.0)

    def finished(lo, hi, c_lo):
        mid = midpoint(lo, hi)
        closed = jnp.where(mid <= lo, 1, jnp.where(mid >= hi, 1, 0))
        closed = jnp.where(lo == 0.0, jnp.where(hi == MIN_NORMAL, 1, closed), closed)
        return jnp.where(c_lo == topk, 1, closed)

    def step(st, pick):
        lo, hi, c_lo, c_hi, g_lo, g_hi, last = st
        done = finished(lo, hi, c_lo)
        cand = pick(lo, hi, g_lo, g_hi)
        cand = jnp.where(cand > lo, jnp.where(cand < hi, cand, midpoint(lo, hi)), midpoint(lo, hi))
        cand = jnp.where(done > 0, lo, cand)
        c = count_ge(cand)
        g = c.astype(F32) - (topk - 0.5)
        up = jnp.where(done > 0, 0, jnp.where(c >= topk, 1, 0))
        dn = jnp.where(done > 0, 0, jnp.where(c >= topk, 0, 1))
        g_hi = jnp.where(up * last > 0, g_hi * 0.5, g_hi)
        g_lo = jnp.where(dn * last < 0, g_lo * 0.5, g_lo)
        return (jnp.where(up > 0, cand, lo), jnp.where(dn > 0, cand, hi),
                jnp.where(up > 0, c, c_lo), jnp.where(dn > 0, c, c_hi),
                jnp.where(up > 0, g, g_lo), jnp.where(dn > 0, g, g_hi),
                jnp.where(up > 0, 1, jnp.where(dn > 0, -1, last)))

    def interpolate(lo, hi, g_lo, g_hi):
        return lo + (hi - lo) * (g_lo / (g_lo - g_hi))

    def bisect(lo, hi, g_lo, g_hi):
        return midpoint(lo, hi)

    def unfinished(st):
        return jnp.max(jnp.where(finished(st[0], st[1], st[2]) > 0, 0.0, 1.0)) > 0.0

    def snap(lo, hi, g_lo, g_hi):
        def body(kb, mx):
            sc = sc_ref[kb]
            return jnp.maximum(mx, fold_rows(jnp.where(sc < hi, sc, -jnp.inf), jnp.maximum))
        mx = over_blocks(body, jnp.full((S, tq), -jnp.inf, F32), widths=(2, 1))
        return jnp.max(mx, axis=0, keepdims=True)

    def one_round(st, picks):
        for pick in picks:
            st = step(st, pick)
        return st

    def search(st, picks, max_rounds):
        return lax.while_loop(lambda c: jnp.logical_and(c[0] < max_rounds, unfinished(c[1])),
                              lambda c: (c[0] + 1, one_round(c[1], picks)), (jnp.int32(0), st))[1]

    n_causal = qpos + 1
    small = n_causal <= topk
    lo0 = jnp.where(small, -jnp.inf, lo_bound)
    hi0 = hi_bound + jnp.maximum(jnp.abs(hi_bound) * 1e-6, 1e-37)
    c_lo0 = jnp.where(small, topk, n_causal)
    st = (lo0, hi0, c_lo0, jnp.zeros((1, tq), jnp.int32), c_lo0.astype(F32) - (topk - 0.5),
          jnp.full((1, tq), 0.5 - topk, F32), jnp.zeros((1, tq), jnp.int32))
    st = step(st, lambda lo, hi, g_lo, g_hi: jnp.zeros_like(lo))
    st = step(st, lambda lo, hi, g_lo, g_hi: jnp.full_like(lo, MIN_NORMAL))
    st = lax.fori_loop(0, SEARCH_WARMUP_ROUNDS, lambda _, st: one_round(st, (interpolate, interpolate)), st)
    st = search(st, (interpolate, snap), SEARCH_INTERP_ROUNDS)
    st = search(st, (bisect, bisect), SEARCH_BISECT_ROUNDS)
    thr, c_lo, c_hi = st[0], st[2], st[3]
    need = jnp.where(small, 0, jnp.where(c_lo == topk, topk, topk - c_hi)).astype(F32)

    tri = tri_ref[...]
    GL = s_ref.shape[3]
    HG = GL // tq
    NG = H // HG
    for h in range(H):
        qs_ref[h * tq:(h + 1) * tq, :] = bq_ref[:, h * DSA_HD:(h + 1) * DSA_HD]

    def qk_part(grp, kb, mx, base):
        sc = sc_ref[kb]
        if grp == 0:
            eq = jnp.where(sc == thr, 1.0, 0.0).astype(BF16)
            rank = _dot(tri, eq) + base
            bias = jnp.where(sc > thr, 0.0,
                             jnp.where(sc == thr, jnp.where(rank <= need, 0.0, NEG_BIG), NEG_BIG))
            sc_ref[kb] = bias
            base = rank[tk - 1:tk, :]
        else:
            bias = sc
        s = _dot_nt(kk_ref[krows(kb), 0:DSA_HD], qs_ref[grp * GL:(grp + 1) * GL, :])
        s = s + jnp.concatenate([bias] * HG, axis=1)
        s_ref[grp % 2, kb] = s
        return jnp.maximum(mx, fold_rows(s, jnp.maximum)), base

    def pv_part(grp, kb, m):
        p = jnp.exp2(s_ref[grp % 2, kb] - m).astype(BF16)
        acc_ref[grp % 2] += _dot(v1t_ref[:, krows(kb)], p)

    def finish(grp):
        acc = acc_ref[grp % 2]
        out = acc[0:DSA_HD, :] / acc[DSA_HD:DSA_HD + 1, :]
        for j in range(HG):
            h = grp * HG + j
            o_ref[:, h * DSA_HD:(h + 1) * DSA_HD] = out[:, j * tq:(j + 1) * tq].T.astype(o_ref.dtype)

    mx0 = jnp.full((S, GL), NEG_BIG, F32)
    m_prev = None
    for grp in range(NG + 1):
        if grp > 0:
            acc_ref[(grp - 1) % 2] = jnp.zeros(acc_ref.shape[1:], F32)

        def body(kb, carry, grp=grp, m_prev=m_prev):
            mx, base = carry
            if grp > 0:
                pv_part(grp - 1, kb, m_prev)
            if grp < NG:
                mx, base = qk_part(grp, kb, mx, base)
            return mx, base

        mx, _ = over_blocks(body, (mx0, jnp.zeros((1, tq), F32)))
        if grp > 0:
            finish(grp - 1)
        m_prev = jnp.max(mx, axis=0, keepdims=True)


def _dsa(bq, iq, misct, kk, v1t, B, T, tq, tk, topk):
    N = B * T
    nq = T // tq
    qrow = lambda b, i: (b * nq + i, 0)
    r = lax.broadcasted_iota(jnp.int32, (tk, tk), 0)
    c = lax.broadcasted_iota(jnp.int32, (tk, tk), 1)
    tri = jnp.where(c <= r, 1.0, 0.0).astype(BF16)
    return pl.pallas_call(
        functools.partial(_dsa_kernel, tq=tq, tk=tk, topk=topk),
        grid=(B, nq),
        in_specs=[
            pl.BlockSpec((tq, DSA_Q), qrow), pl.BlockSpec((tq, IDX_Q), qrow),
            pl.BlockSpec((LANES, tq), lambda b, i: (0, b * nq + i)),
            pl.BlockSpec((T, LANES), lambda b, i: (b, 0)),
            pl.BlockSpec((LANES, T), lambda b, i: (0, b)),
            pl.BlockSpec((tk, tk), lambda b, i: (0, 0)),
        ],
        out_specs=pl.BlockSpec((tq, DSA_Q), qrow),
        out_shape=jax.ShapeDtypeStruct((N, DSA_Q), BF16),
        scratch_shapes=[
            pltpu.VMEM((T // tk, tk, tq), F32),
            pltpu.VMEM((2, T // tk, tk, DSA_GROUP_LANES), F32),
            pltpu.VMEM((DSA_HEADS * tq, DSA_HD), BF16),
            pltpu.VMEM((2, LANES, DSA_GROUP_LANES), F32),
        ],
        compiler_params=_params("parallel", "arbitrary"),
        name="dsa",
    )(bq, iq, misct, kk, v1t, tri)


FFN_CHUNKS = (768, 768, 768, 512)


def _ffn_kernel(*refs, tm, tiles_per_seq, final_norm, attn_out, rb):
    if attn_out:
        oa_ref, ob_ref, wa_ref, wb_ref = refs[:4]
        refs = refs[4:]
    x_ref, g_ref, wup_ref, cw_ref, cb_ref, wd_ref, fg_ref, o_ref, h_ref, act_ref, acc_ref = refs[:11]
    ab_refs = refs[11:]
    i = pl.program_id(0)
    H = SUBLANES
    n_c = len(FFN_CHUNKS)
    ag_refs, au_refs = ab_refs[:n_c], ab_refs[n_c:]
    offs = [sum(FFN_CHUNKS[:c]) for c in range(n_c)]
    first = (i % tiles_per_seq) == 0

    @pl.when(first)
    def _():
        for ref in ab_refs:
            ref[0:H, :] = jnp.zeros((H, ref.shape[1]), F32)

    @pl.when(jnp.logical_not(first))
    def _():
        for ref in ab_refs:
            ref[0:H, :] = ref[tm:tm + H, :]

    def up_proj(c):
        o, w = offs[c], FFN_CHUNKS[c]
        ag_refs[c][H:H + tm, :] = _dot(hb, wup_ref[:, o:o + w])
        au_refs[c][H:H + tm, :] = _dot(hb, wup_ref[:, D_FF + o:D_FF + o + w])

    def gate_act(c):
        o, w = offs[c], FFN_CHUNKS[c]
        cwg, cwu = cw_ref[:, o:o + w], cw_ref[:, D_FF + o:D_FF + o + w]
        cbg, cbu = cb_ref[:, o:o + w], cb_ref[:, D_FF + o:D_FF + o + w]
        for r0 in range(0, tm, rb):
            wg = ag_refs[c][r0:r0 + rb + H, :]
            wu = au_refs[c][r0:r0 + rb + H, :]
            gate, up = cbg, cbu
            for j in range(CONV_W):
                s0 = H - (CONV_W - 1) + j
                gate = gate + cwg[j:j + 1, :] * wg[s0:s0 + rb, :]
                up = up + cwu[j:j + 1, :] * wu[s0:s0 + rb, :]
            act_ref[r0:r0 + rb, o:o + w] = (gate * jax.nn.sigmoid(gate) * up).astype(BF16)

    def down_proj(c):
        o, w = offs[c], FFN_CHUNKS[c]
        d = _dot(act_ref[:, o:o + w], wd_ref[o:o + w, :])
        if c == 0:
            acc_ref[...] = d
        else:
            acc_ref[...] += d

    if attn_out:
        o_ref[...] = x_ref[...] + _dot(oa_ref[...], wa_ref[...]) + _dot(ob_ref[...], wb_ref[...])
        xin_ref = o_ref
    else:
        xin_ref = x_ref
    h_ref[...] = _rmsnorm_rows(xin_ref[...], g_ref[...]).astype(BF16)
    hb = h_ref[...]
    up_proj(0)
    for c in range(n_c):
        if c + 1 < n_c:
            up_proj(c + 1)
        gate_act(c)
        down_proj(c)
    y = xin_ref[...] + acc_ref[...]
    if final_norm:
        y = _rmsnorm_rows(y, fg_ref[...])
    o_ref[...] = y


def _conv_ffn(x2, g, w_up, conv_w, conv_b, w_down, final_g, layer, T, tm, final_norm, attn=None):
    N = x2.shape[0]
    row = lambda i: (i, 0)
    fixed = lambda i: (0, 0)
    def wspec(shape):
        return pl.BlockSpec((None,) + shape, lambda i: (layer, 0, 0), pipeline_mode=pl.Buffered(1))
    attn_args, attn_specs = (), []
    if attn is not None:
        oa, ob, w_o = attn
        half = w_o.shape[0] // 2
        w_o = w_o.astype(BF16)
        attn_args = (oa, ob, w_o[:half], w_o[half:])
        attn_specs = [pl.BlockSpec((tm, half), row), pl.BlockSpec((tm, half), row),
                      pl.BlockSpec((half, D_MODEL), fixed), pl.BlockSpec((half, D_MODEL), fixed)]
    return pl.pallas_call(
        functools.partial(_ffn_kernel, tm=tm, tiles_per_seq=T // tm, final_norm=final_norm,
                          attn_out=attn is not None, rb=64),
        grid=(N // tm,),
        in_specs=attn_specs + [
            pl.BlockSpec((tm, D_MODEL), row),
            pl.BlockSpec((1, D_MODEL), fixed),
            wspec((D_MODEL, 2 * D_FF)),
            pl.BlockSpec((CONV_W, 2 * D_FF), fixed),
            pl.BlockSpec((1, 2 * D_FF), fixed),
            wspec((D_FF, D_MODEL)),
            pl.BlockSpec((1, D_MODEL), fixed),
        ],
        out_specs=pl.BlockSpec((tm, D_MODEL), row),
        out_shape=jax.ShapeDtypeStruct((N, D_MODEL), F32),
        scratch_shapes=[
            pltpu.VMEM((tm, D_MODEL), BF16),
            pltpu.VMEM((tm, D_FF), BF16),
            pltpu.VMEM((tm, D_MODEL), F32),
        ] + [pltpu.VMEM((tm + SUBLANES, w), F32) for w in FFN_CHUNKS] * 2,
        compiler_params=_params("arbitrary"),
        name="conv_ffn",
    )(*attn_args, x2, g.reshape(1, -1), w_up, conv_w, conv_b.reshape(1, -1), w_down, final_g.reshape(1, -1))


def _sgu_kernel(x_ref, g_ref, wuv_ref, lng_ref, lnb_ref, ws_ref, bs_ref, wo_ref, o_ref,
                u_ref, v_ref, gated_ref, *, tm):
    W = D_MODEL
    gw = W // SGU_GROUPS
    h = _rmsnorm_rows(x_ref[...], g_ref[...]).astype(BF16)
    u_ref[...] = jax.nn.gelu(_dot(h, wuv_ref[:, 0:W]))
    v = jax.nn.gelu(_dot(h, wuv_ref[:, W:2 * W]))
    mu = jnp.mean(v, axis=-1, keepdims=True)
    vc = v - mu
    v = vc * lax.rsqrt(jnp.mean(vc * vc, axis=-1, keepdims=True) + LN_EPS) * lng_ref[...] + lnb_ref[...]
    v_ref[...] = v.astype(BF16)
    ri = lax.broadcasted_iota(jnp.int32, (SGU_CHUNK, SGU_CHUNK), 0)
    ci = lax.broadcasted_iota(jnp.int32, (SGU_CHUNK, SGU_CHUNK), 1)
    tril = ri >= ci
    for gi in range(SGU_GROUPS):
        cs = slice(gi * gw, (gi + 1) * gw)
        ws = jnp.where(tril, ws_ref[gi], 0.0).astype(BF16)
        bias = bs_ref[:, gi:gi + 1]
        for n in range(tm // SGU_CHUNK):
            rs = slice(n * SGU_CHUNK, (n + 1) * SGU_CHUNK)
            mixed = _dot(ws, v_ref[rs, cs]) + bias
            gated_ref[rs, cs] = (u_ref[rs, cs] * mixed).astype(BF16)
    o_ref[...] = x_ref[...] + _dot(gated_ref[...], wo_ref[...])


def _sgu(x2, g, w_uv, ln_g, ln_b, w_s, b_s, w_out, tm):
    N = x2.shape[0]
    row = lambda i: (i, 0)
    fixed = lambda i: (0, 0)
    return pl.pallas_call(
        functools.partial(_sgu_kernel, tm=tm),
        grid=(N // tm,),
        in_specs=[
            pl.BlockSpec((tm, D_MODEL), row),
            pl.BlockSpec((1, D_MODEL), fixed),
            pl.BlockSpec((D_MODEL, 2 * D_MODEL), fixed),
            pl.BlockSpec((1, D_MODEL), fixed), pl.BlockSpec((1, D_MODEL), fixed),
            pl.BlockSpec((SGU_GROUPS, SGU_CHUNK, SGU_CHUNK), lambda i: (0, 0, 0)),
            pl.BlockSpec((SGU_CHUNK, SGU_GROUPS), fixed),
            pl.BlockSpec((D_MODEL, D_MODEL), fixed),
        ],
        out_specs=pl.BlockSpec((tm, D_MODEL), row),
        out_shape=jax.ShapeDtypeStruct((N, D_MODEL), F32),
        scratch_shapes=[pltpu.VMEM((tm, D_MODEL), F32), pltpu.VMEM((tm, D_MODEL), BF16),
                        pltpu.VMEM((tm, D_MODEL), BF16)],
        compiler_params=_params("parallel"),
        name="sgu",
    )(x2, g.reshape(1, -1), w_uv.astype(BF16), ln_g.reshape(1, -1), ln_b.reshape(1, -1),
      w_s, b_s.T, w_out.astype(BF16))


def kernel(x, attn_norm, attn_w_in, gla_w_a2, gla_b_a, gla_head_g, attn_w_o, sgu_norm, sgu_w_uv, sgu_ln_g,
           sgu_ln_b, sgu_w_s, sgu_b_s, sgu_w_out, ffn_norm, ffn_w_up, ffn_conv_w, ffn_conv_b, ffn_w_down,
           final_norm):
    B, T, D = x.shape
    assert D == D_MODEL and T % IN_PROJ_TILE == 0 and T % TOKEN_TILE == 0 and T % DSA_K_TILE == 0
    topk = min(TOPK_MAX, T // 4)
    depth = ffn_norm.shape[0]
    x2 = x.reshape(B * T, D)
    w_up_bf, w_down_bf = ffn_w_up.astype(BF16), ffn_w_down.astype(BF16)
    for i in range(depth):
        j = i // 2
        if i % 2 == 0:
            aq, ak, av, ar, la, bq, iq, kk, v1t, misct = _in_proj(
                x2, attn_norm[j], attn_w_in[j], gla_w_a2[j], gla_b_a[j], T, IN_PROJ_TILE)
            oa = _gla(aq, ak, av, la, ar, gla_head_g[j], B, T, TOKEN_TILE)
            ob = _dsa(bq, iq, misct, kk, v1t, B, T, DSA_Q_TILE, DSA_K_TILE, topk)
            attn = (oa, ob, attn_w_o[j])
        else:
            attn = None
            x2 = _sgu(x2, sgu_norm[j], sgu_w_uv[j], sgu_ln_g[j], sgu_ln_b[j], sgu_w_s[j], sgu_b_s[j],
                      sgu_w_out[j], TOKEN_TILE)
        x2 = _conv_ffn(x2, ffn_norm[i], w_up_bf, ffn_conv_w[i], ffn_conv_b[i], w_down_bf,
                       final_norm, i, T, TOKEN_TILE, i == depth - 1, attn)
    return x2.reshape(B, T, D)
```

```python
import functools

import jax
import jax.numpy as jnp
import numpy as np
from jax import lax
from jax.experimental import pallas as pl
from jax.experimental.pallas import tpu as pltpu

D_MODEL = 1024
GLA_HEADS = 4
GLA_DK = 64
GLA_DV = 128
GLA_GATE_RANK = 16
GLA_GATE_TAU = 16.0
GLA_CHUNK = 64
DSA_HEADS = 8
DSA_HD = 64
IDX_HEADS = 4
IDX_HD = 64
TOPK_MAX = 256
ROPE_THETA = 500000.0
ROPE_FRAC_DIV = 4
SGU_CHUNK = 128
SGU_GROUPS = 8
D_FF = 2816
CONV_W = 3
EPS = 1e-6
LN_EPS = 1e-5

LANES = 128
SUBLANES = 8
VMEM_LIMIT = 56 * 1024 * 1024

TOKEN_TILE = 512
WIDE_TILE = 1024
DSA_Q_TILE = 256
DSA_K_TILE = 512

GLA_QK = GLA_HEADS * GLA_DK
GLA_V = GLA_HEADS * GLA_DV
DSA_Q = DSA_HEADS * DSA_HD
IDX_Q = IDX_HEADS * IDX_HD

def _segments(*widths):
    edges = [0]
    for w in widths:
        edges.append(edges[-1] + w)
    return [(edges[i], edges[i + 1]) for i in range(len(widths))], edges[-1]


(_SEG_AQ, _SEG_AK, _SEG_AV, _SEG_AR, _SEG_BQ, _SEG_IQ, _SEG_KK, _SEG_MISC), IN_PAD = _segments(
    GLA_QK, GLA_QK, GLA_V, GLA_V, DSA_Q, IDX_Q, LANES, LANES)
MISC_ALR = 64
MISC_IW = 80

F32 = jnp.float32
BF16 = jnp.bfloat16
NEG_BIG = -1e30
LOG2E = 1.4426950408889634


def _dot(a, b):
    return jnp.dot(a, b, preferred_element_type=F32)


def _dot_nt(a, b):
    return lax.dot_general(a, b, (((1,), (1,)), ((), ())), preferred_element_type=F32)


def _dot_tn(a, b):
    return lax.dot_general(a, b, (((0,), (0,)), ((), ())), preferred_element_type=F32)


def _rmsnorm_rows(x, g):
    ms = jnp.mean(x * x, axis=-1, keepdims=True)
    return x * lax.rsqrt(ms + EPS) * g


def _params(*sem):
    return pltpu.CompilerParams(dimension_semantics=sem, vmem_limit_bytes=VMEM_LIMIT)


def _rope_slab(x, tab):
    half = DSA_HD // ROPE_FRAC_DIV // 2
    c = tab[:, 0:LANES]
    s_up = tab[:, LANES:2 * LANES]
    s_dn = tab[:, 2 * LANES:3 * LANES]
    return x * c + pltpu.roll(x, half, 1) * s_up + pltpu.roll(x, LANES - half, 1) * s_dn


def _in_proj_kernel(x_ref, g_ref, w_ref, wa2_ref, ba_ref, tab2_ref,
                    aq_ref, ak_ref, av_ref, ar_ref, la_ref, bq_ref, iq_ref, kk_ref, v1t_ref, misct_ref):
    h = _rmsnorm_rows(x_ref[...], g_ref[...]).astype(BF16)

    def seg(s):
        return _dot(h, w_ref[:, s[0]:s[1]])

    aq_ref[...] = (seg(_SEG_AQ) * (GLA_DK ** -0.5)).astype(aq_ref.dtype)
    ak_ref[...] = seg(_SEG_AK).astype(ak_ref.dtype)
    av_ref[...] = seg(_SEG_AV).astype(av_ref.dtype)
    ar_ref[...] = seg(_SEG_AR).astype(ar_ref.dtype)

    tab2 = tab2_ref[...]
    bq = seg(_SEG_BQ)
    for j in range((_SEG_BQ[1] - _SEG_BQ[0]) // LANES):
        sl = slice(j * LANES, (j + 1) * LANES)
        bq_ref[:, sl] = (_rope_slab(bq[:, sl], tab2) * (DSA_HD ** -0.5 * LOG2E)).astype(bq_ref.dtype)
    iq = seg(_SEG_IQ)
    for j in range((_SEG_IQ[1] - _SEG_IQ[0]) // LANES):
        sl = slice(j * LANES, (j + 1) * LANES)
        iq_ref[:, sl] = (_rope_slab(iq[:, sl], tab2) * (IDX_HD ** -0.5)).astype(iq_ref.dtype)
    kk_ref[...] = _rope_slab(seg(_SEG_KK), tab2).astype(kk_ref.dtype)
    misc = seg(_SEG_MISC)
    misct = misc.T
    misct_ref[...] = misct
    row = lax.broadcasted_iota(jnp.int32, misct.shape, 0)
    v1t = jnp.where(row < DSA_HD, misct, jnp.where(row == DSA_HD, 1.0, 0.0))
    v1t_ref[...] = v1t.astype(v1t_ref.dtype)
    z = _dot(misc.astype(BF16), wa2_ref[...]) + ba_ref[...]
    la_ref[...] = (jnp.minimum(z, 0.0) - jnp.log(1.0 + jnp.exp(-jnp.abs(z)))) * (1.0 / GLA_GATE_TAU)


def _rope_tables(T):
    rd = DSA_HD // ROPE_FRAC_DIV
    half = rd // 2
    pos = jnp.arange(T, dtype=F32)
    inv = jnp.power(ROPE_THETA, -(jnp.arange(half, dtype=F32) * 2.0 / rd))
    ang = pos[:, None] * inv[None, :]
    cs = jnp.concatenate([jnp.cos(ang), jnp.sin(ang)], axis=1)
    sel = np.zeros((2 * half, 3 * LANES), np.float32)
    one = np.zeros((1, 3 * LANES), np.float32)
    for head in range(LANES // DSA_HD):
        o = head * DSA_HD
        one[0, o + rd:o + DSA_HD] = 1.0
        for j in range(half):
            sel[j, o + j] = 1.0
            sel[j, o + half + j] = 1.0
            sel[half + j, LANES + o + half + j] = 1.0
            sel[half + j, 2 * LANES + o + j] = -1.0
    return jnp.dot(cs, jnp.asarray(sel), precision=lax.Precision.HIGHEST) + jnp.asarray(one)


def _in_proj(x2, g, w_in, w_a2, b_a, T, tm):
    N = x2.shape[0]
    widths = (GLA_QK, GLA_QK, GLA_V, GLA_V, GLA_GATE_RANK, DSA_Q, DSA_HD, DSA_HD, IDX_Q, IDX_HD, IDX_HEADS)
    offs = [0]
    for w in widths:
        offs.append(offs[-1] + w)
    w_in = w_in.astype(BF16)
    aq, ak, av, ar, alr, bq, bk, bv, iq, ik, iw = [w_in[:, offs[i]:offs[i + 1]] for i in range(11)]
    pad = jnp.zeros((D_MODEL, LANES - IDX_HD - GLA_GATE_RANK - IDX_HEADS), w_in.dtype)
    wp = jnp.concatenate([aq, ak, av, ar, bq, iq, bk, ik, bv, alr, iw * (IDX_HEADS ** -0.5), pad],
                         axis=1)
    wa2 = jnp.zeros((LANES, GLA_HEADS * GLA_DK), F32).at[MISC_ALR:MISC_ALR + GLA_GATE_RANK].set(w_a2).astype(BF16)
    tab2 = _rope_tables(T)
    nt = T // tm
    row = lambda i: (i, 0)
    fixed = lambda i: (0, 0)
    tabm = lambda i: (i % nt, 0)
    outs = [(GLA_QK, BF16), (GLA_QK, BF16), (GLA_V, BF16), (GLA_V, BF16), (GLA_QK, F32),
            (DSA_Q, BF16), (IDX_Q, BF16), (LANES, BF16)]
    col = lambda i: (0, i)
    return pl.pallas_call(
        _in_proj_kernel,
        grid=(N // tm,),
        in_specs=[
            pl.BlockSpec((tm, D_MODEL), row),
            pl.BlockSpec((1, D_MODEL), fixed),
            pl.BlockSpec((D_MODEL, IN_PAD), fixed),
            pl.BlockSpec((LANES, GLA_HEADS * GLA_DK), fixed),
            pl.BlockSpec((1, GLA_HEADS * GLA_DK), fixed),
            pl.BlockSpec((tm, 3 * LANES), tabm),
        ],
        out_specs=[pl.BlockSpec((tm, w), row) for w, _ in outs]
        + [pl.BlockSpec((LANES, tm), col), pl.BlockSpec((LANES, tm), col)],
        out_shape=[jax.ShapeDtypeStruct((N, w), d) for w, d in outs]
        + [jax.ShapeDtypeStruct((LANES, N), BF16), jax.ShapeDtypeStruct((LANES, N), F32)],
        compiler_params=_params("parallel"),
        name="in_proj",
    )(x2, g.reshape(1, -1), wp, wa2, b_a.reshape(1, -1), tab2)


def _gla_kernel(q_ref, k_ref, v_ref, la_ref, r_ref, hg_ref, o_ref, st_ref, *, n_chunks):
    C = GLA_CHUNK

    @pl.when(pl.program_id(1) == 0)
    def _():
        st_ref[...] = jnp.zeros_like(st_ref)

    ri = lax.broadcasted_iota(jnp.int32, (C, C), 0)
    ci = lax.broadcasted_iota(jnp.int32, (C, C), 1)
    tril = ri >= ci
    row_id = lax.broadcasted_iota(jnp.int32, (C, GLA_HEADS * GLA_DK), 0)
    hg = hg_ref[...]
    state = [st_ref[hh] for hh in range(GLA_HEADS)]

    for c in range(n_chunks):
        rows = slice(c * C, (c + 1) * C)
        la = la_ref[rows, :]
        b = la
        for sh in (1, 2, 4, 8, 16, 32):
            b = b + jnp.where(row_id >= sh, pltpu.roll(b, sh, 0), 0.0)
        b_mid = b[C // 2:C // 2 + 1, :]
        b_last = b[C - 1:C, :]
        q = q_ref[rows, :].astype(F32)
        k = k_ref[rows, :].astype(F32)
        qe = (q * jnp.exp(b - b_mid)).astype(BF16)
        ke = (k * jnp.exp(b_mid - b)).astype(BF16)
        kl = (k * jnp.exp(b_last - b)).astype(BF16)
        qb = (q * jnp.exp(b)).astype(BF16)
        dec = jnp.exp(b_last)
        for hh in range(GLA_HEADS):
            ks = slice(hh * GLA_DK, (hh + 1) * GLA_DK)
            vs = slice(hh * GLA_DV, (hh + 1) * GLA_DV)
            v = v_ref[rows, vs]
            att = jnp.where(tril, _dot_nt(qe[:, ks], ke[:, ks]), 0.0)
            o = _dot(att.astype(BF16), v) + _dot_nt(qb[:, ks], state[hh].astype(BF16))
            state[hh] = state[hh] * dec[:, ks] + _dot_tn(v, kl[:, ks])
            o = o * lax.rsqrt(jnp.mean(o * o, axis=-1, keepdims=True) + EPS) * hg
            r = r_ref[rows, vs].astype(F32)
            o_ref[rows, vs] = (o * (r * jax.nn.sigmoid(r))).astype(o_ref.dtype)
    for hh in range(GLA_HEADS):
        st_ref[hh] = state[hh]


def _gla(aq, ak, av, la, ar, head_g, B, T, tg):
    N = B * T
    nt = T // tg
    row = lambda b, i: (b * nt + i, 0)
    return pl.pallas_call(
        functools.partial(_gla_kernel, n_chunks=tg // GLA_CHUNK),
        grid=(B, nt),
        in_specs=[
            pl.BlockSpec((tg, GLA_QK), row), pl.BlockSpec((tg, GLA_QK), row), pl.BlockSpec((tg, GLA_V), row),
            pl.BlockSpec((tg, GLA_QK), row), pl.BlockSpec((tg, GLA_V), row),
            pl.BlockSpec((1, GLA_DV), lambda b, i: (0, 0)),
        ],
        out_specs=pl.BlockSpec((tg, GLA_V), row),
        out_shape=jax.ShapeDtypeStruct((N, GLA_V), BF16),
        scratch_shapes=[pltpu.VMEM((GLA_HEADS, GLA_DV, GLA_DK), F32)],
        compiler_params=_params("parallel", "arbitrary"),
        name="gla",
    )(aq, ak, av, la, ar, head_g.reshape(1, -1))


DSA_GROUP_LANES = 1024
MIN_NORMAL = 1.1754943508222875e-38
TINY_BRACKET = 2.0 ** -60
TINY_SCALE = 2.0 ** 64
SEARCH_WARMUP_ROUNDS = 5
SEARCH_INTERP_ROUNDS = 18
SEARCH_BISECT_ROUNDS = 150


def _dsa_kernel(bq_ref, iq_ref, qmisct_ref, kk_ref, v1t_ref, tri_ref, o_ref,
                sc_ref, s_ref, qs_ref, acc_ref, *, tq, tk, topk):
    qi = pl.program_id(1)
    q0 = qi * tq
    nkb = (q0 + tq + tk - 1) // tk
    qpos = q0 + lax.broadcasted_iota(jnp.int32, (1, tq), 1)
    key_iota = lax.broadcasted_iota(jnp.int32, (tk, tq), 0)
    H = DSA_HEADS
    S = SUBLANES

    def krows(kb):
        return pl.ds(pl.multiple_of(kb * tk, tk), tk)

    def over_blocks(body, carry, widths=(4, 2, 1)):
        start = 0
        for w in widths:
            def group(j, c, w=w):
                for i in range(w):
                    c = body(w * j + i, c)
                return c
            carry = lax.fori_loop(start // w, nkb // w, group, carry)
            start = (nkb // w) * w
        return carry

    def fold_rows(x, op):
        parts = [x[j * S:(j + 1) * S, :] for j in range(x.shape[0] // S)]
        while len(parts) > 1:
            parts = [op(parts[j], parts[j + 1]) for j in range(0, len(parts) - 1, 2)] + (
                [parts[-1]] if len(parts) % 2 else [])
        return parts[0]

    iw = [qmisct_ref[MISC_IW + h:MISC_IW + h + 1, :] for h in range(IDX_HEADS)]
    for h in range(IDX_HEADS):
        qs_ref[h * tq:(h + 1) * tq, :] = iq_ref[:, h * IDX_HD:(h + 1) * IDX_HD]

    def score_block(kb, mm):
        x = _dot_nt(kk_ref[krows(kb), IDX_HD:2 * IDX_HD], qs_ref[0:IDX_HEADS * tq, :])
        sc = iw[0] * jnp.maximum(x[:, 0:tq], 0.0)
        for h in range(1, IDX_HEADS):
            sc = sc + iw[h] * jnp.maximum(x[:, h * tq:(h + 1) * tq], 0.0)
        sc = sc + 0.0
        sc_ref[kb] = jnp.where(kb * tk + key_iota <= qpos, sc, -jnp.inf)
        return jnp.maximum(mm[0], fold_rows(sc, jnp.maximum)), jnp.minimum(mm[1], fold_rows(sc, jnp.minimum))

    mx8, mn8 = over_blocks(score_block, (jnp.full((S, tq), -jnp.inf, F32), jnp.full((S, tq), jnp.inf, F32)))
    hi_bound = jnp.max(mx8, axis=0, keepdims=True)
    lo_bound = jnp.min(mn8, axis=0, keepdims=True)

    def count_ge(cand):
        def body(kb, cnt):
            return cnt + fold_rows(jnp.where(sc_ref[kb] >= cand, 1, 0), jnp.add)
        cnt = over_blocks(body, jnp.zeros((S, tq), jnp.int32), widths=(2, 1))
        return jnp.sum(cnt, axis=0, keepdims=True)

    def midpoint(lo, hi):
        tiny = jnp.maximum(jnp.abs(lo), jnp.abs(hi)) < TINY_BRACKET
        up = jnp.where(tiny, TINY_SCALE, 1.0)
        return (0.5 * (lo * up) + 0.5 * (hi * up)) * jnp.where(tiny, 1.0 / TINY_SCALE, 1.0)

    def finished(lo, hi, c_lo):
        mid = midpoint(lo, hi)
        closed = jnp.where(mid <= lo, 1, jnp.where(mid >= hi, 1, 0))
        closed = jnp.where(lo == 0.0, jnp.where(hi == MIN_NORMAL, 1, closed), closed)
        return jnp.where(c_lo == topk, 1, closed)

    def step(st, pick):
        lo, hi, c_lo, c_hi, g_lo, g_hi, last = st
        done = finished(lo, hi, c_lo)
        cand = pick(lo, hi, g_lo, g_hi)
        cand = jnp.where(cand > lo, jnp.where(cand < hi, cand, midpoint(lo, hi)), midpoint(lo, hi))
        cand = jnp.where(done > 0, lo, cand)
        c = count_ge(cand)
        g = c.astype(F32) - (topk - 0.5)
        up = jnp.where(done > 0, 0, jnp.where(c >= topk, 1, 0))
        dn = jnp.where(done > 0, 0, jnp.where(c >= topk, 0, 1))
        g_hi = jnp.where(up * last > 0, g_hi * 0.5, g_hi)
        g_lo = jnp.where(dn * last < 0, g_lo * 0.5, g_lo)
        return (jnp.where(up > 0, cand, lo), jnp.where(dn > 0, cand, hi),
                jnp.where(up > 0, c, c_lo), jnp.where(dn > 0, c, c_hi),
                jnp.where(up > 0, g, g_lo), jnp.where(dn > 0, g, g_hi),
                jnp.where(up > 0, 1, jnp.where(dn > 0, -1, last)))

    def interpolate(lo, hi, g_lo, g_hi):
        return lo + (hi - lo) * (g_lo / (g_lo - g_hi))

    def bisect(lo, hi, g_lo, g_hi):
        return midpoint(lo, hi)

    def unfinished(st):
        return jnp.max(jnp.where(finished(st[0], st[1], st[2]) > 0, 0.0, 1.0)) > 0.0

    def snap(lo, hi, g_lo, g_hi):
        def body(kb, mx):
            sc = sc_ref[kb]
            return jnp.maximum(mx, fold_rows(jnp.where(sc < hi, sc, -jnp.inf), jnp.maximum))
        mx = over_blocks(body, jnp.full((S, tq), -jnp.inf, F32), widths=(2, 1))
        return jnp.max(mx, axis=0, keepdims=True)

    def one_round(st, picks):
        for pick in picks:
            st = step(st, pick)
        return st

    def search(st, picks, max_rounds):
        return lax.while_loop(lambda c: jnp.logical_and(c[0] < max_rounds, unfinished(c[1])),
                              lambda c: (c[0] + 1, one_round(c[1], picks)), (jnp.int32(0), st))[1]

    n_causal = qpos + 1
    small = n_causal <= topk
    lo0 = jnp.where(small, -jnp.inf, lo_bound)
    hi0 = hi_bound + jnp.maximum(jnp.abs(hi_bound) * 1e-6, 1e-37)
    c_lo0 = jnp.where(small, topk, n_causal)
    st = (lo0, hi0, c_lo0, jnp.zeros((1, tq), jnp.int32), c_lo0.astype(F32) - (topk - 0.5),
          jnp.full((1, tq), 0.5 - topk, F32), jnp.zeros((1, tq), jnp.int32))
    st = step(st, lambda lo, hi, g_lo, g_hi: jnp.zeros_like(lo))
    st = step(st, lambda lo, hi, g_lo, g_hi: jnp.full_like(lo, MIN_NORMAL))
    st = lax.fori_loop(0, SEARCH_WARMUP_ROUNDS, lambda _, st: one_round(st, (interpolate, interpolate)), st)
    st = search(st, (interpolate, snap), SEARCH_INTERP_ROUNDS)
    st = search(st, (bisect, bisect), SEARCH_BISECT_ROUNDS)
    thr, c_lo, c_hi = st[0], st[2], st[3]
    need = jnp.where(small, 0, jnp.where(c_lo == topk, topk, topk - c_hi)).astype(F32)

    tri = tri_ref[...]
    GL = s_ref.shape[3]
    HG = GL // tq
    NG = H // HG
    for h in range(H):
        qs_ref[h * tq:(h + 1) * tq, :] = bq_ref[:, h * DSA_HD:(h + 1) * DSA_HD]

    def qk_part(grp, kb, mx, base):
        sc = sc_ref[kb]
        if grp == 0:
            eq = jnp.where(sc == thr, 1.0, 0.0).astype(BF16)
            rank = _dot(tri, eq) + base
            bias = jnp.where(sc > thr, 0.0,
                             jnp.where(sc == thr, jnp.where(rank <= need, 0.0, NEG_BIG), NEG_BIG))
            sc_ref[kb] = bias
            base = rank[tk - 1:tk, :]
        else:
            bias = sc
        s = _dot_nt(kk_ref[krows(kb), 0:DSA_HD], qs_ref[grp * GL:(grp + 1) * GL, :])
        s = s + jnp.concatenate([bias] * HG, axis=1)
        s_ref[grp % 2, kb] = s
        return jnp.maximum(mx, fold_rows(s, jnp.maximum)), base

    def pv_part(grp, kb, m):
        p = jnp.exp2(s_ref[grp % 2, kb] - m).astype(BF16)
        acc_ref[grp % 2] += _dot(v1t_ref[:, krows(kb)], p)

    def finish(grp):
        acc = acc_ref[grp % 2]
        out = acc[0:DSA_HD, :] / acc[DSA_HD:DSA_HD + 1, :]
        for j in range(HG):
            h = grp * HG + j
            o_ref[:, h * DSA_HD:(h + 1) * DSA_HD] = out[:, j * tq:(j + 1) * tq].T.astype(o_ref.dtype)

    mx0 = jnp.full((S, GL), NEG_BIG, F32)
    m_prev = None
    for grp in range(NG + 1):
        if grp > 0:
            acc_ref[(grp - 1) % 2] = jnp.zeros(acc_ref.shape[1:], F32)

        def body(kb, carry, grp=grp, m_prev=m_prev):
            mx, base = carry
            if grp > 0:
                pv_part(grp - 1, kb, m_prev)
            if grp < NG:
                mx, base = qk_part(grp, kb, mx, base)
            return mx, base

        mx, _ = over_blocks(body, (mx0, jnp.zeros((1, tq), F32)))
        if grp > 0:
            finish(grp - 1)
        m_prev = jnp.max(mx, axis=0, keepdims=True)


def _dsa(bq, iq, misct, kk, v1t, B, T, tq, tk, topk):
    N = B * T
    nq = T // tq
    qrow = lambda b, i: (b * nq + i, 0)
    r = lax.broadcasted_iota(jnp.int32, (tk, tk), 0)
    c = lax.broadcasted_iota(jnp.int32, (tk, tk), 1)
    tri = jnp.where(c <= r, 1.0, 0.0).astype(BF16)
    return pl.pallas_call(
        functools.partial(_dsa_kernel, tq=tq, tk=tk, topk=topk),
        grid=(B, nq),
        in_specs=[
            pl.BlockSpec((tq, DSA_Q), qrow), pl.BlockSpec((tq, IDX_Q), qrow),
            pl.BlockSpec((LANES, tq), lambda b, i: (0, b * nq + i)),
            pl.BlockSpec((T, LANES), lambda b, i: (b, 0)),
            pl.BlockSpec((LANES, T), lambda b, i: (0, b)),
            pl.BlockSpec((tk, tk), lambda b, i: (0, 0)),
        ],
        out_specs=pl.BlockSpec((tq, DSA_Q), qrow),
        out_shape=jax.ShapeDtypeStruct((N, DSA_Q), BF16),
        scratch_shapes=[
            pltpu.VMEM((T // tk, tk, tq), F32),
            pltpu.VMEM((2, T // tk, tk, DSA_GROUP_LANES), F32),
            pltpu.VMEM((DSA_HEADS * tq, DSA_HD), BF16),
            pltpu.VMEM((2, LANES, DSA_GROUP_LANES), F32),
        ],
        compiler_params=_params("parallel", "arbitrary"),
        name="dsa",
    )(bq, iq, misct, kk, v1t, tri)


FFN_CHUNKS = (768, 768, 768, 512)


def _ffn_kernel(*refs, tm, tiles_per_seq, final_norm, attn_out, rb):
    if attn_out:
        oa_ref, ob_ref, wa_ref, wb_ref = refs[:4]
        refs = refs[4:]
    x_ref, g_ref, wup_ref, cw_ref, cb_ref, wd_ref, fg_ref, o_ref, h_ref, act_ref, acc_ref = refs[:11]
    ab_refs = refs[11:]
    i = pl.program_id(0)
    H = SUBLANES
    n_c = len(FFN_CHUNKS)
    ag_refs, au_refs = ab_refs[:n_c], ab_refs[n_c:]
    offs = [sum(FFN_CHUNKS[:c]) for c in range(n_c)]
    first = (i % tiles_per_seq) == 0

    @pl.when(first)
    def _():
        for ref in ab_refs:
            ref[0:H, :] = jnp.zeros((H, ref.shape[1]), F32)

    @pl.when(jnp.logical_not(first))
    def _():
        for ref in ab_refs:
            ref[0:H, :] = ref[tm:tm + H, :]

    def up_proj(c):
        o, w = offs[c], FFN_CHUNKS[c]
        ag_refs[c][H:H + tm, :] = _dot(hb, wup_ref[:, o:o + w])
        au_refs[c][H:H + tm, :] = _dot(hb, wup_ref[:, D_FF + o:D_FF + o + w])

    def gate_act(c):
        o, w = offs[c], FFN_CHUNKS[c]
        cwg, cwu = cw_ref[:, o:o + w], cw_ref[:, D_FF + o:D_FF + o + w]
        cbg, cbu = cb_ref[:, o:o + w], cb_ref[:, D_FF + o:D_FF + o + w]
        for r0 in range(0, tm, rb):
            wg = ag_refs[c][r0:r0 + rb + H, :]
            wu = au_refs[c][r0:r0 + rb + H, :]
            gate, up = cbg, cbu
            for j in range(CONV_W):
                s0 = H - (CONV_W - 1) + j
                gate = gate + cwg[j:j + 1, :] * wg[s0:s0 + rb, :]
                up = up + cwu[j:j + 1, :] * wu[s0:s0 + rb, :]
            act_ref[r0:r0 + rb, o:o + w] = (gate * jax.nn.sigmoid(gate) * up).astype(BF16)

    def down_proj(c):
        o, w = offs[c], FFN_CHUNKS[c]
        d = _dot(act_ref[:, o:o + w], wd_ref[o:o + w, :])
        if c == 0:
            acc_ref[...] = d
        else:
            acc_ref[...] += d

    if attn_out:
        o_ref[...] = x_ref[...] + _dot(oa_ref[...], wa_ref[...]) + _dot(ob_ref[...], wb_ref[...])
        xin_ref = o_ref
    else:
        xin_ref = x_ref
    h_ref[...] = _rmsnorm_rows(xin_ref[...], g_ref[...]).astype(BF16)
    hb = h_ref[...]
    up_proj(0)
    for c in range(n_c):
        if c + 1 < n_c:
            up_proj(c + 1)
        gate_act(c)
        down_proj(c)
    y = xin_ref[...] + acc_ref[...]
    if final_norm:
        y = _rmsnorm_rows(y, fg_ref[...])
    o_ref[...] = y


def _conv_ffn(x2, g, w_up, conv_w, conv_b, w_down, final_g, layer, T, tm, final_norm, attn=None):
    N = x2.shape[0]
    row = lambda i: (i, 0)
    fixed = lambda i: (0, 0)
    def wspec(shape):
        return pl.BlockSpec((None,) + shape, lambda i: (layer, 0, 0), pipeline_mode=pl.Buffered(1))
    attn_args, attn_specs = (), []
    if attn is not None:
        oa, ob, w_o = attn
        half = w_o.shape[0] // 2
        w_o = w_o.astype(BF16)
        attn_args = (oa, ob, w_o[:half], w_o[half:])
        attn_specs = [pl.BlockSpec((tm, half), row), pl.BlockSpec((tm, half), row),
                      pl.BlockSpec((half, D_MODEL), fixed), pl.BlockSpec((half, D_MODEL), fixed)]
    return pl.pallas_call(
        functools.partial(_ffn_kernel, tm=tm, tiles_per_seq=T // tm, final_norm=final_norm,
                          attn_out=attn is not None, rb=64),
        grid=(N // tm,),
        in_specs=attn_specs + [
            pl.BlockSpec((tm, D_MODEL), row),
            pl.BlockSpec((1, D_MODEL), fixed),
            wspec((D_MODEL, 2 * D_FF)),
            pl.BlockSpec((CONV_W, 2 * D_FF), fixed),
            pl.BlockSpec((1, 2 * D_FF), fixed),
            wspec((D_FF, D_MODEL)),
            pl.BlockSpec((1, D_MODEL), fixed),
        ],
        out_specs=pl.BlockSpec((tm, D_MODEL), row),
        out_shape=jax.ShapeDtypeStruct((N, D_MODEL), F32),
        scratch_shapes=[
            pltpu.VMEM((tm, D_MODEL), BF16),
            pltpu.VMEM((tm, D_FF), BF16),
            pltpu.VMEM((tm, D_MODEL), F32),
        ] + [pltpu.VMEM((tm + SUBLANES, w), F32) for w in FFN_CHUNKS] * 2,
        compiler_params=_params("arbitrary"),
        name="conv_ffn",
    )(*attn_args, x2, g.reshape(1, -1), w_up, conv_w, conv_b.reshape(1, -1), w_down, final_g.reshape(1, -1))


def _sgu_kernel(x_ref, g_ref, wuv_ref, lng_ref, lnb_ref, ws_ref, bs_ref, wo_ref, o_ref,
                u_ref, v_ref, gated_ref, *, tm):
    W = D_MODEL
    gw = W // SGU_GROUPS
    h = _rmsnorm_rows(x_ref[...], g_ref[...]).astype(BF16)
    u_ref[...] = jax.nn.gelu(_dot(h, wuv_ref[:, 0:W]))
    v = jax.nn.gelu(_dot(h, wuv_ref[:, W:2 * W]))
    mu = jnp.mean(v, axis=-1, keepdims=True)
    vc = v - mu
    v = vc * lax.rsqrt(jnp.mean(vc * vc, axis=-1, keepdims=True) + LN_EPS) * lng_ref[...] + lnb_ref[...]
    v_ref[...] = v.astype(BF16)
    ri = lax.broadcasted_iota(jnp.int32, (SGU_CHUNK, SGU_CHUNK), 0)
    ci = lax.broadcasted_iota(jnp.int32, (SGU_CHUNK, SGU_CHUNK), 1)
    tril = ri >= ci
    for gi in range(SGU_GROUPS):
        cs = slice(gi * gw, (gi + 1) * gw)
        ws = jnp.where(tril, ws_ref[gi], 0.0).astype(BF16)
        bias = bs_ref[:, gi:gi + 1]
        for n in range(tm // SGU_CHUNK):
            rs = slice(n * SGU_CHUNK, (n + 1) * SGU_CHUNK)
            mixed = _dot(ws, v_ref[rs, cs]) + bias
            gated_ref[rs, cs] = (u_ref[rs, cs] * mixed).astype(BF16)
    o_ref[...] = x_ref[...] + _dot(gated_ref[...], wo_ref[...])


def _sgu(x2, g, w_uv, ln_g, ln_b, w_s, b_s, w_out, tm):
    N = x2.shape[0]
    row = lambda i: (i, 0)
    fixed = lambda i: (0, 0)
    return pl.pallas_call(
        functools.partial(_sgu_kernel, tm=tm),
        grid=(N // tm,),
        in_specs=[
            pl.BlockSpec((tm, D_MODEL), row),
            pl.BlockSpec((1, D_MODEL), fixed),
            pl.BlockSpec((D_MODEL, 2 * D_MODEL), fixed),
            pl.BlockSpec((1, D_MODEL), fixed), pl.BlockSpec((1, D_MODEL), fixed),
            pl.BlockSpec((SGU_GROUPS, SGU_CHUNK, SGU_CHUNK), lambda i: (0, 0, 0)),
            pl.BlockSpec((SGU_CHUNK, SGU_GROUPS), fixed),
            pl.BlockSpec((D_MODEL, D_MODEL), fixed),
        ],
        out_specs=pl.BlockSpec((tm, D_MODEL), row),
        out_shape=jax.ShapeDtypeStruct((N, D_MODEL), F32),
        scratch_shapes=[pltpu.VMEM((tm, D_MODEL), F32), pltpu.VMEM((tm, D_MODEL), BF16),
                        pltpu.VMEM((tm, D_MODEL), BF16)],
        compiler_params=_params("parallel"),
        name="sgu",
    )(x2, g.reshape(1, -1), w_uv.astype(BF16), ln_g.reshape(1, -1), ln_b.reshape(1, -1),
      w_s, b_s.T, w_out.astype(BF16))


def kernel(x, attn_norm, attn_w_in, gla_w_a2, gla_b_a, gla_head_g, attn_w_o, sgu_norm, sgu_w_uv, sgu_ln_g,
           sgu_ln_b, sgu_w_s, sgu_b_s, sgu_w_out, ffn_norm, ffn_w_up, ffn_conv_w, ffn_conv_b, ffn_w_down,
           final_norm):
    B, T, D = x.shape
    assert D == D_MODEL and T % WIDE_TILE == 0 and T % TOKEN_TILE == 0 and T % DSA_K_TILE == 0
    topk = min(TOPK_MAX, T // 4)
    depth = ffn_norm.shape[0]
    x2 = x.reshape(B * T, D)
    w_up_bf, w_down_bf = ffn_w_up.astype(BF16), ffn_w_down.astype(BF16)
    for i in range(depth):
        j = i // 2
        if i % 2 == 0:
            aq, ak, av, ar, la, bq, iq, kk, v1t, misct = _in_proj(
                x2, attn_norm[j], attn_w_in[j], gla_w_a2[j], gla_b_a[j], T, WIDE_TILE)
            oa = _gla(aq, ak, av, la, ar, gla_head_g[j], B, T, TOKEN_TILE)
            ob = _dsa(bq, iq, misct, kk, v1t, B, T, DSA_Q_TILE, DSA_K_TILE, topk)
            attn = (oa, ob, attn_w_o[j])
        else:
            attn = None
            x2 = _sgu(x2, sgu_norm[j], sgu_w_uv[j], sgu_ln_g[j], sgu_ln_b[j], sgu_w_s[j], sgu_b_s[j],
                      sgu_w_out[j], WIDE_TILE)
        x2 = _conv_ffn(x2, ffn_norm[i], w_up_bf, ffn_conv_w[i], ffn_conv_b[i], w_down_bf,
                       final_norm, i, T, TOKEN_TILE, i == depth - 1, attn)
    return x2.reshape(B, T, D)
```

```python
import functools

import jax
import jax.numpy as jnp
import numpy as np
from jax import lax
from jax.experimental import pallas as pl
from jax.experimental.pallas import tpu as pltpu

D_MODEL = 1024
GLA_HEADS = 4
GLA_DK = 64
GLA_DV = 128
GLA_GATE_RANK = 16
GLA_GATE_TAU = 16.0
GLA_CHUNK = 64
DSA_HEADS = 8
DSA_HD = 64
IDX_HEADS = 4
IDX_HD = 64
TOPK_MAX = 256
ROPE_THETA = 500000.0
ROPE_FRAC_DIV = 4
SGU_CHUNK = 128
SGU_GROUPS = 8
D_FF = 2816
CONV_W = 3
EPS = 1e-6
LN_EPS = 1e-5

LANES = 128
SUBLANES = 8
VMEM_LIMIT = 56 * 1024 * 1024

TOKEN_TILE = 512
WIDE_TILE = 1024
DSA_Q_TILE = 256
DSA_K_TILE = 512

GLA_QK = GLA_HEADS * GLA_DK
GLA_V = GLA_HEADS * GLA_DV
DSA_Q = DSA_HEADS * DSA_HD
IDX_Q = IDX_HEADS * IDX_HD

def _segments(*widths):
    edges = [0]
    for w in widths:
        edges.append(edges[-1] + w)
    return [(edges[i], edges[i + 1]) for i in range(len(widths))], edges[-1]


(_SEG_AQ, _SEG_AK, _SEG_AV, _SEG_AR, _SEG_BQ, _SEG_IQ, _SEG_KK, _SEG_MISC), IN_PAD = _segments(
    GLA_QK, GLA_QK, GLA_V, GLA_V, DSA_Q, IDX_Q, LANES, LANES)
MISC_ALR = 64
MISC_IW = 80

F32 = jnp.float32
BF16 = jnp.bfloat16
NEG_BIG = -1e30
LOG2E = 1.4426950408889634


def _dot(a, b):
    return jnp.dot(a, b, preferred_element_type=F32)


def _dot_nt(a, b):
    return lax.dot_general(a, b, (((1,), (1,)), ((), ())), preferred_element_type=F32)


def _dot_tn(a, b):
    return lax.dot_general(a, b, (((0,), (0,)), ((), ())), preferred_element_type=F32)


def _rmsnorm_rows(x, g):
    ms = jnp.mean(x * x, axis=-1, keepdims=True)
    return x * lax.rsqrt(ms + EPS) * g


def _params(*sem):
    return pltpu.CompilerParams(dimension_semantics=sem, vmem_limit_bytes=VMEM_LIMIT)


def _rope_slab(x, tab):
    half = DSA_HD // ROPE_FRAC_DIV // 2
    c = tab[:, 0:LANES]
    s_up = tab[:, LANES:2 * LANES]
    s_dn = tab[:, 2 * LANES:3 * LANES]
    return x * c + pltpu.roll(x, half, 1) * s_up + pltpu.roll(x, LANES - half, 1) * s_dn


def _in_proj_kernel(x_ref, g_ref, w_ref, wa2_ref, ba_ref, tab2_ref,
                    aq_ref, ak_ref, av_ref, ar_ref, la_ref, bq_ref, iq_ref, kk_ref, v1t_ref, misct_ref):
    h = _rmsnorm_rows(x_ref[...], g_ref[...]).astype(BF16)

    def seg(s):
        return _dot(h, w_ref[:, s[0]:s[1]])

    aq_ref[...] = (seg(_SEG_AQ) * (GLA_DK ** -0.5)).astype(aq_ref.dtype)
    ak_ref[...] = seg(_SEG_AK).astype(ak_ref.dtype)
    av_ref[...] = seg(_SEG_AV).astype(av_ref.dtype)
    ar_ref[...] = seg(_SEG_AR).astype(ar_ref.dtype)

    tab2 = tab2_ref[...]
    bq = seg(_SEG_BQ)
    for j in range((_SEG_BQ[1] - _SEG_BQ[0]) // LANES):
        sl = slice(j * LANES, (j + 1) * LANES)
        bq_ref[:, sl] = (_rope_slab(bq[:, sl], tab2) * (DSA_HD ** -0.5 * LOG2E)).astype(bq_ref.dtype)
    iq = seg(_SEG_IQ)
    for j in range((_SEG_IQ[1] - _SEG_IQ[0]) // LANES):
        sl = slice(j * LANES, (j + 1) * LANES)
        iq_ref[:, sl] = (_rope_slab(iq[:, sl], tab2) * (IDX_HD ** -0.5)).astype(iq_ref.dtype)
    kk_ref[...] = _rope_slab(seg(_SEG_KK), tab2).astype(kk_ref.dtype)
    misc = seg(_SEG_MISC)
    misct = misc.T
    misct_ref[...] = misct
    row = lax.broadcasted_iota(jnp.int32, misct.shape, 0)
    v1t = jnp.where(row < DSA_HD, misct, jnp.where(row == DSA_HD, 1.0, 0.0))
    v1t_ref[...] = v1t.astype(v1t_ref.dtype)
    z = _dot(misc.astype(BF16), wa2_ref[...]) + ba_ref[...]
    la_ref[...] = (jnp.minimum(z, 0.0) - jnp.log(1.0 + jnp.exp(-jnp.abs(z)))) * (1.0 / GLA_GATE_TAU)


def _rope_tables(T):
    rd = DSA_HD // ROPE_FRAC_DIV
    half = rd // 2
    pos = jnp.arange(T, dtype=F32)
    inv = jnp.power(ROPE_THETA, -(jnp.arange(half, dtype=F32) * 2.0 / rd))
    ang = pos[:, None] * inv[None, :]
    cs = jnp.concatenate([jnp.cos(ang), jnp.sin(ang)], axis=1)
    sel = np.zeros((2 * half, 3 * LANES), np.float32)
    one = np.zeros((1, 3 * LANES), np.float32)
    for head in range(LANES // DSA_HD):
        o = head * DSA_HD
        one[0, o + rd:o + DSA_HD] = 1.0
        for j in range(half):
            sel[j, o + j] = 1.0
            sel[j, o + half + j] = 1.0
            sel[half + j, LANES + o + half + j] = 1.0
            sel[half + j, 2 * LANES + o + j] = -1.0
    return jnp.dot(cs, jnp.asarray(sel), precision=lax.Precision.HIGHEST) + jnp.asarray(one)


def _in_proj(x2, g, w_in, w_a2, b_a, T, tm):
    N = x2.shape[0]
    widths = (GLA_QK, GLA_QK, GLA_V, GLA_V, GLA_GATE_RANK, DSA_Q, DSA_HD, DSA_HD, IDX_Q, IDX_HD, IDX_HEADS)
    offs = [0]
    for w in widths:
        offs.append(offs[-1] + w)
    w_in = w_in.astype(BF16)
    aq, ak, av, ar, alr, bq, bk, bv, iq, ik, iw = [w_in[:, offs[i]:offs[i + 1]] for i in range(11)]
    pad = jnp.zeros((D_MODEL, LANES - IDX_HD - GLA_GATE_RANK - IDX_HEADS), w_in.dtype)
    wp = jnp.concatenate([aq, ak, av, ar, bq, iq, bk, ik, bv, alr, iw * (IDX_HEADS ** -0.5), pad],
                         axis=1)
    wa2 = jnp.zeros((LANES, GLA_HEADS * GLA_DK), F32).at[MISC_ALR:MISC_ALR + GLA_GATE_RANK].set(w_a2).astype(BF16)
    tab2 = _rope_tables(T)
    nt = T // tm
    row = lambda i: (i, 0)
    fixed = lambda i: (0, 0)
    tabm = lambda i: (i % nt, 0)
    outs = [(GLA_QK, BF16), (GLA_QK, BF16), (GLA_V, BF16), (GLA_V, BF16), (GLA_QK, F32),
            (DSA_Q, BF16), (IDX_Q, BF16), (LANES, BF16)]
    col = lambda i: (0, i)
    return pl.pallas_call(
        _in_proj_kernel,
        grid=(N // tm,),
        in_specs=[
            pl.BlockSpec((tm, D_MODEL), row),
            pl.BlockSpec((1, D_MODEL), fixed),
            pl.BlockSpec((D_MODEL, IN_PAD), fixed),
            pl.BlockSpec((LANES, GLA_HEADS * GLA_DK), fixed),
            pl.BlockSpec((1, GLA_HEADS * GLA_DK), fixed),
            pl.BlockSpec((tm, 3 * LANES), tabm),
        ],
        out_specs=[pl.BlockSpec((tm, w), row) for w, _ in outs]
        + [pl.BlockSpec((LANES, tm), col), pl.BlockSpec((LANES, tm), col)],
        out_shape=[jax.ShapeDtypeStruct((N, w), d) for w, d in outs]
        + [jax.ShapeDtypeStruct((LANES, N), BF16), jax.ShapeDtypeStruct((LANES, N), F32)],
        compiler_params=_params("parallel"),
        name="in_proj",
    )(x2, g.reshape(1, -1), wp, wa2, b_a.reshape(1, -1), tab2)


def _gla_kernel(q_ref, k_ref, v_ref, la_ref, r_ref, hg_ref, o_ref, st_ref, *, n_chunks):
    C = GLA_CHUNK

    @pl.when(pl.program_id(1) == 0)
    def _():
        st_ref[...] = jnp.zeros_like(st_ref)

    ri = lax.broadcasted_iota(jnp.int32, (C, C), 0)
    ci = lax.broadcasted_iota(jnp.int32, (C, C), 1)
    tril = ri >= ci
    row_id = lax.broadcasted_iota(jnp.int32, (C, GLA_HEADS * GLA_DK), 0)
    hg = hg_ref[...]
    state = [st_ref[hh] for hh in range(GLA_HEADS)]

    for c in range(n_chunks):
        rows = slice(c * C, (c + 1) * C)
        la = la_ref[rows, :]
        b = la
        for sh in (1, 2, 4, 8, 16, 32):
            b = b + jnp.where(row_id >= sh, pltpu.roll(b, sh, 0), 0.0)
        b_mid = b[C // 2:C // 2 + 1, :]
        b_last = b[C - 1:C, :]
        q = q_ref[rows, :].astype(F32)
        k = k_ref[rows, :].astype(F32)
        qe = (q * jnp.exp(b - b_mid)).astype(BF16)
        ke = (k * jnp.exp(b_mid - b)).astype(BF16)
        kl = (k * jnp.exp(b_last - b)).astype(BF16)
        qb = (q * jnp.exp(b)).astype(BF16)
        dec = jnp.exp(b_last)
        for hh in range(GLA_HEADS):
            ks = slice(hh * GLA_DK, (hh + 1) * GLA_DK)
            vs = slice(hh * GLA_DV, (hh + 1) * GLA_DV)
            v = v_ref[rows, vs]
            att = jnp.where(tril, _dot_nt(qe[:, ks], ke[:, ks]), 0.0)
            o = _dot(att.astype(BF16), v) + _dot_nt(qb[:, ks], state[hh].astype(BF16))
            state[hh] = state[hh] * dec[:, ks] + _dot_tn(v, kl[:, ks])
            o = o * lax.rsqrt(jnp.mean(o * o, axis=-1, keepdims=True) + EPS) * hg
            r = r_ref[rows, vs].astype(F32)
            o_ref[rows, vs] = (o * (r * jax.nn.sigmoid(r))).astype(o_ref.dtype)
    for hh in range(GLA_HEADS):
        st_ref[hh] = state[hh]


def _gla(aq, ak, av, la, ar, head_g, B, T, tg):
    N = B * T
    nt = T // tg
    row = lambda b, i: (b * nt + i, 0)
    return pl.pallas_call(
        functools.partial(_gla_kernel, n_chunks=tg // GLA_CHUNK),
        grid=(B, nt),
        in_specs=[
            pl.BlockSpec((tg, GLA_QK), row), pl.BlockSpec((tg, GLA_QK), row), pl.BlockSpec((tg, GLA_V), row),
            pl.BlockSpec((tg, GLA_QK), row), pl.BlockSpec((tg, GLA_V), row),
            pl.BlockSpec((1, GLA_DV), lambda b, i: (0, 0)),
        ],
        out_specs=pl.BlockSpec((tg, GLA_V), row),
        out_shape=jax.ShapeDtypeStruct((N, GLA_V), BF16),
        scratch_shapes=[pltpu.VMEM((GLA_HEADS, GLA_DV, GLA_DK), F32)],
        compiler_params=_params("parallel", "arbitrary"),
        name="gla",
    )(aq, ak, av, la, ar, head_g.reshape(1, -1))


DSA_GROUP_LANES = 1024
MIN_NORMAL = 1.1754943508222875e-38
TINY_BRACKET = 2.0 ** -60
TINY_SCALE = 2.0 ** 64
SEARCH_WARMUP_ROUNDS = 5
SEARCH_INTERP_ROUNDS = 18
SEARCH_BISECT_ROUNDS = 150


def _dsa_kernel(bq_ref, iq_ref, qmisct_ref, kk_ref, v1t_ref, tri_ref, o_ref,
                sc_ref, s_ref, qs_ref, acc_ref, *, tq, tk, topk):
    qi = pl.program_id(1)
    q0 = qi * tq
    nkb = (q0 + tq + tk - 1) // tk
    qpos = q0 + lax.broadcasted_iota(jnp.int32, (1, tq), 1)
    key_iota = lax.broadcasted_iota(jnp.int32, (tk, tq), 0)
    H = DSA_HEADS
    S = SUBLANES

    def krows(kb):
        return pl.ds(pl.multiple_of(kb * tk, tk), tk)

    def over_blocks(body, carry, widths=(4, 2, 1)):
        start = 0
        for w in widths:
            def group(j, c, w=w):
                for i in range(w):
                    c = body(w * j + i, c)
                return c
            carry = lax.fori_loop(start // w, nkb // w, group, carry)
            start = (nkb // w) * w
        return carry

    def fold_rows(x, op):
        parts = [x[j * S:(j + 1) * S, :] for j in range(x.shape[0] // S)]
        while len(parts) > 1:
            parts = [op(parts[j], parts[j + 1]) for j in range(0, len(parts) - 1, 2)] + (
                [parts[-1]] if len(parts) % 2 else [])
        return parts[0]

    iw = [qmisct_ref[MISC_IW + h:MISC_IW + h + 1, :] for h in range(IDX_HEADS)]
    for h in range(IDX_HEADS):
        qs_ref[h * tq:(h + 1) * tq, :] = iq_ref[:, h * IDX_HD:(h + 1) * IDX_HD]

    def score_block(kb, mm):
        x = _dot_nt(kk_ref[krows(kb), IDX_HD:2 * IDX_HD], qs_ref[0:IDX_HEADS * tq, :])
        sc = iw[0] * jnp.maximum(x[:, 0:tq], 0.0)
        for h in range(1, IDX_HEADS):
            sc = sc + iw[h] * jnp.maximum(x[:, h * tq:(h + 1) * tq], 0.0)
        sc = sc + 0.0
        sc_ref[kb] = jnp.where(kb * tk + key_iota <= qpos, sc, -jnp.inf)
        return jnp.maximum(mm[0], fold_rows(sc, jnp.maximum)), jnp.minimum(mm[1], fold_rows(sc, jnp.minimum))

    mx8, mn8 = over_blocks(score_block, (jnp.full((S, tq), -jnp.inf, F32), jnp.full((S, tq), jnp.inf, F32)))
    hi_bound = jnp.max(mx8, axis=0, keepdims=True)
    lo_bound = jnp.min(mn8, axis=0, keepdims=True)

    def count_ge(cand):
        def body(kb, cnt):
            return cnt + fold_rows(jnp.where(sc_ref[kb] >= cand, 1, 0), jnp.add)
        cnt = over_blocks(body, jnp.zeros((S, tq), jnp.int32), widths=(2, 1))
        return jnp.sum(cnt, axis=0, keepdims=True)

    def midpoint(lo, hi):
        tiny = jnp.maximum(jnp.abs(lo), jnp.abs(hi)) < TINY_BRACKET
        up = jnp.where(tiny, TINY_SCALE, 1.0)
        return (0.5 * (lo * up) + 0.5 * (hi * up)) * jnp.where(tiny, 1.0 / TINY_SCALE, 1.0)

    def finished(lo, hi, c_lo):
        mid = midpoint(lo, hi)
        closed = jnp.where(mid <= lo, 1, jnp.where(mid >= hi, 1, 0))
        closed = jnp.where(lo == 0.0, jnp.where(hi == MIN_NORMAL, 1, closed), closed)
        return jnp.where(c_lo == topk, 1, closed)

    def step(st, pick):
        lo, hi, c_lo, c_hi, g_lo, g_hi, last = st
        done = finished(lo, hi, c_lo)
        cand = pick(lo, hi, g_lo, g_hi)
        cand = jnp.where(cand > lo, jnp.where(cand < hi, cand, midpoint(lo, hi)), midpoint(lo, hi))
        cand = jnp.where(done > 0, lo, cand)
        c = count_ge(cand)
        g = c.astype(F32) - (topk - 0.5)
        up = jnp.where(done > 0, 0, jnp.where(c >= topk, 1, 0))
        dn = jnp.where(done > 0, 0, jnp.where(c >= topk, 0, 1))
        g_hi = jnp.where(up * last > 0, g_hi * 0.5, g_hi)
        g_lo = jnp.where(dn * last < 0, g_lo * 0.5, g_lo)
        return (jnp.where(up > 0, cand, lo), jnp.where(dn > 0, cand, hi),
                jnp.where(up > 0, c, c_lo), jnp.where(dn > 0, c, c_hi),
                jnp.where(up > 0, g, g_lo), jnp.where(dn > 0, g, g_hi),
                jnp.where(up > 0, 1, jnp.where(dn > 0, -1, last)))

    def interpolate(lo, hi, g_lo, g_hi):
        return lo + (hi - lo) * (g_lo / (g_lo - g_hi))

    def bisect(lo, hi, g_lo, g_hi):
        return midpoint(lo, hi)

    def unfinished(st):
        return jnp.max(jnp.where(finished(st[0], st[1], st[2]) > 0, 0.0, 1.0)) > 0.0

    def snap(lo, hi, g_lo, g_hi):
        def body(kb, mx):
            sc = sc_ref[kb]
            return jnp.maximum(mx, fold_rows(jnp.where(sc < hi, sc, -jnp.inf), jnp.maximum))
        mx = over_blocks(body, jnp.full((S, tq), -jnp.inf, F32), widths=(2, 1))
        return jnp.max(mx, axis=0, keepdims=True)

    def one_round(st, picks):
        for pick in picks:
            st = step(st, pick)
        return st

    def search(st, picks, max_rounds):
        return lax.while_loop(lambda c: jnp.logical_and(c[0] < max_rounds, unfinished(c[1])),
                              lambda c: (c[0] + 1, one_round(c[1], picks)), (jnp.int32(0), st))[1]

    n_causal = qpos + 1
    small = n_causal <= topk
    lo0 = jnp.where(small, -jnp.inf, lo_bound)
    hi0 = hi_bound + jnp.maximum(jnp.abs(hi_bound) * 1e-6, 1e-37)
    c_lo0 = jnp.where(small, topk, n_causal)
    st = (lo0, hi0, c_lo0, jnp.zeros((1, tq), jnp.int32), c_lo0.astype(F32) - (topk - 0.5),
          jnp.full((1, tq), 0.5 - topk, F32), jnp.zeros((1, tq), jnp.int32))
    st = step(st, lambda lo, hi, g_lo, g_hi: jnp.zeros_like(lo))
    st = step(st, lambda lo, hi, g_lo, g_hi: jnp.full_like(lo, MIN_NORMAL))
    st = lax.fori_loop(0, SEARCH_WARMUP_ROUNDS, lambda _, st: one_round(st, (interpolate, interpolate)), st)
    st = search(st, (interpolate, snap), SEARCH_INTERP_ROUNDS)
    st = search(st, (bisect, bisect), SEARCH_BISECT_ROUNDS)
    thr, c_lo, c_hi = st[0], st[2], st[3]
    need = jnp.where(small, 0, jnp.where(c_lo == topk, topk, topk - c_hi)).astype(F32)

    tri = tri_ref[...]
    GL = s_ref.shape[3]
    HG = GL // tq
    NG = H // HG
    for h in range(H):
        qs_ref[h * tq:(h + 1) * tq, :] = bq_ref[:, h * DSA_HD:(h + 1) * DSA_HD]

    def qk_part(grp, kb, mx, base):
        sc = sc_ref[kb]
        if grp == 0:
            eq = jnp.where(sc == thr, 1.0, 0.0).astype(BF16)
            rank = _dot(tri, eq) + base
            bias = jnp.where(sc > thr, 0.0,
                             jnp.where(sc == thr, jnp.where(rank <= need, 0.0, NEG_BIG), NEG_BIG))
            sc_ref[kb] = bias
            base = rank[tk - 1:tk, :]
        else:
            bias = sc
        s = _dot_nt(kk_ref[krows(kb), 0:DSA_HD], qs_ref[grp * GL:(grp + 1) * GL, :])
        s = s + jnp.concatenate([bias] * HG, axis=1)
        s_ref[grp % 2, kb] = s
        return jnp.maximum(mx, fold_rows(s, jnp.maximum)), base

    def pv_part(grp, kb, m):
        p = jnp.exp2(s_ref[grp % 2, kb] - m).astype(BF16)
        acc_ref[grp % 2] += _dot(v1t_ref[:, krows(kb)], p)

    def finish(grp):
        acc = acc_ref[grp % 2]
        out = acc[0:DSA_HD, :] / acc[DSA_HD:DSA_HD + 1, :]
        for j in range(HG):
            h = grp * HG + j
            o_ref[:, h * DSA_HD:(h + 1) * DSA_HD] = out[:, j * tq:(j + 1) * tq].T.astype(o_ref.dtype)

    mx0 = jnp.full((S, GL), NEG_BIG, F32)
    m_prev = None
    for grp in range(NG + 1):
        if grp > 0:
            acc_ref[(grp - 1) % 2] = jnp.zeros(acc_ref.shape[1:], F32)

        def body(kb, carry, grp=grp, m_prev=m_prev):
            mx, base = carry
            if grp > 0:
                pv_part(grp - 1, kb, m_prev)
            if grp < NG:
                mx, base = qk_part(grp, kb, mx, base)
            return mx, base

        mx, _ = over_blocks(body, (mx0, jnp.zeros((1, tq), F32)))
        if grp > 0:
            finish(grp - 1)
        m_prev = jnp.max(mx, axis=0, keepdims=True)


def _dsa(bq, iq, misct, kk, v1t, B, T, tq, tk, topk):
    N = B * T
    nq = T // tq
    qrow = lambda b, i: (b * nq + i, 0)
    r = lax.broadcasted_iota(jnp.int32, (tk, tk), 0)
    c = lax.broadcasted_iota(jnp.int32, (tk, tk), 1)
    tri = jnp.where(c <= r, 1.0, 0.0).astype(BF16)
    return pl.pallas_call(
        functools.partial(_dsa_kernel, tq=tq, tk=tk, topk=topk),
        grid=(B, nq),
        in_specs=[
            pl.BlockSpec((tq, DSA_Q), qrow), pl.BlockSpec((tq, IDX_Q), qrow),
            pl.BlockSpec((LANES, tq), lambda b, i: (0, b * nq + i)),
            pl.BlockSpec((T, LANES), lambda b, i: (b, 0)),
            pl.BlockSpec((LANES, T), lambda b, i: (0, b)),
            pl.BlockSpec((tk, tk), lambda b, i: (0, 0)),
        ],
        out_specs=pl.BlockSpec((tq, DSA_Q), qrow),
        out_shape=jax.ShapeDtypeStruct((N, DSA_Q), BF16),
        scratch_shapes=[
            pltpu.VMEM((T // tk, tk, tq), F32),
            pltpu.VMEM((2, T // tk, tk, DSA_GROUP_LANES), F32),
            pltpu.VMEM((DSA_HEADS * tq, DSA_HD), BF16),
            pltpu.VMEM((2, LANES, DSA_GROUP_LANES), F32),
        ],
        compiler_params=_params("parallel", "arbitrary"),
        name="dsa",
    )(bq, iq, misct, kk, v1t, tri)


FFN_CHUNKS = (768, 768, 768, 512)


def _ffn_kernel(*refs, tm, tiles_per_seq, final_norm, attn_out, rb):
    if attn_out:
        oa_ref, ob_ref, wa_ref, wb_ref = refs[:4]
        refs = refs[4:]
    x_ref, g_ref, wup_ref, cw_ref, cb_ref, wd_ref, fg_ref, o_ref, h_ref, act_ref, acc_ref = refs[:11]
    ab_refs = refs[11:]
    i = pl.program_id(0)
    H = SUBLANES
    n_c = len(FFN_CHUNKS)
    ag_refs, au_refs = ab_refs[:n_c], ab_refs[n_c:]
    offs = [sum(FFN_CHUNKS[:c]) for c in range(n_c)]
    first = (i % tiles_per_seq) == 0

    @pl.when(first)
    def _():
        for ref in ab_refs:
            ref[0:H, :] = jnp.zeros((H, ref.shape[1]), F32)

    @pl.when(jnp.logical_not(first))
    def _():
        for ref in ab_refs:
            ref[0:H, :] = ref[tm:tm + H, :]

    def up_proj(c):
        o, w = offs[c], FFN_CHUNKS[c]
        ag_refs[c][H:H + tm, :] = _dot(hb, wup_ref[:, o:o + w])
        au_refs[c][H:H + tm, :] = _dot(hb, wup_ref[:, D_FF + o:D_FF + o + w])

    def gate_act(c):
        o, w = offs[c], FFN_CHUNKS[c]
        cwg, cwu = cw_ref[:, o:o + w], cw_ref[:, D_FF + o:D_FF + o + w]
        cbg, cbu = cb_ref[:, o:o + w], cb_ref[:, D_FF + o:D_FF + o + w]
        for r0 in range(0, tm, rb):
            wg = ag_refs[c][r0:r0 + rb + H, :]
            wu = au_refs[c][r0:r0 + rb + H, :]
            gate, up = cbg, cbu
            for j in range(CONV_W):
                s0 = H - (CONV_W - 1) + j
                gate = gate + cwg[j:j + 1, :] * wg[s0:s0 + rb, :]
                up = up + cwu[j:j + 1, :] * wu[s0:s0 + rb, :]
            act_ref[r0:r0 + rb, o:o + w] = (gate * jax.nn.sigmoid(gate) * up).astype(BF16)

    def down_proj(c):
        o, w = offs[c], FFN_CHUNKS[c]
        d = _dot(act_ref[:, o:o + w], wd_ref[o:o + w, :])
        if c == 0:
            acc_ref[...] = d
        else:
            acc_ref[...] += d

    if attn_out:
        o_ref[...] = x_ref[...] + _dot(oa_ref[...], wa_ref[...]) + _dot(ob_ref[...], wb_ref[...])
        xin_ref = o_ref
    else:
        xin_ref = x_ref
    h_ref[...] = _rmsnorm_rows(xin_ref[...], g_ref[...]).astype(BF16)
    hb = h_ref[...]
    up_proj(0)
    for c in range(n_c):
        if c + 1 < n_c:
            up_proj(c + 1)
        gate_act(c)
        down_proj(c)
    y = xin_ref[...] + acc_ref[...]
    if final_norm:
        y = _rmsnorm_rows(y, fg_ref[...])
    o_ref[...] = y


def _conv_ffn(x2, g, w_up, conv_w, conv_b, w_down, final_g, layer, T, tm, final_norm, attn=None):
    N = x2.shape[0]
    row = lambda i: (i, 0)
    fixed = lambda i: (0, 0)
    def wspec(shape):
        return pl.BlockSpec((None,) + shape, lambda i: (layer, 0, 0), pipeline_mode=pl.Buffered(1))
    attn_args, attn_specs = (), []
    if attn is not None:
        oa, ob, w_o = attn
        half = w_o.shape[0] // 2
        w_o = w_o.astype(BF16)
        attn_args = (oa, ob, w_o[:half], w_o[half:])
        attn_specs = [pl.BlockSpec((tm, half), row), pl.BlockSpec((tm, half), row),
                      pl.BlockSpec((half, D_MODEL), fixed), pl.BlockSpec((half, D_MODEL), fixed)]
    return pl.pallas_call(
        functools.partial(_ffn_kernel, tm=tm, tiles_per_seq=T // tm, final_norm=final_norm,
                          attn_out=attn is not None, rb=64),
        grid=(N // tm,),
        in_specs=attn_specs + [
            pl.BlockSpec((tm, D_MODEL), row),
            pl.BlockSpec((1, D_MODEL), fixed),
            wspec((D_MODEL, 2 * D_FF)),
            pl.BlockSpec((CONV_W, 2 * D_FF), fixed),
            pl.BlockSpec((1, 2 * D_FF), fixed),
            wspec((D_FF, D_MODEL)),
            pl.BlockSpec((1, D_MODEL), fixed),
        ],
        out_specs=pl.BlockSpec((tm, D_MODEL), row),
        out_shape=jax.ShapeDtypeStruct((N, D_MODEL), F32),
        scratch_shapes=[
            pltpu.VMEM((tm, D_MODEL), BF16),
            pltpu.VMEM((tm, D_FF), BF16),
            pltpu.VMEM((tm, D_MODEL), F32),
        ] + [pltpu.VMEM((tm + SUBLANES, w), F32) for w in FFN_CHUNKS] * 2,
        compiler_params=_params("arbitrary"),
        name="conv_ffn",
    )(*attn_args, x2, g.reshape(1, -1), w_up, conv_w, conv_b.reshape(1, -1), w_down, final_g.reshape(1, -1))


def _sgu_kernel(x_ref, g_ref, wuv_ref, lng_ref, lnb_ref, ws_ref, bs_ref, wo_ref, o_ref,
                u_ref, v_ref, gated_ref, *, tm):
    W = D_MODEL
    gw = W // SGU_GROUPS
    h = _rmsnorm_rows(x_ref[...], g_ref[...]).astype(BF16)
    u_ref[...] = jax.nn.gelu(_dot(h, wuv_ref[:, 0:W]))
    v = jax.nn.gelu(_dot(h, wuv_ref[:, W:2 * W]))
    mu = jnp.mean(v, axis=-1, keepdims=True)
    vc = v - mu
    v = vc * lax.rsqrt(jnp.mean(vc * vc, axis=-1, keepdims=True) + LN_EPS) * lng_ref[...] + lnb_ref[...]
    v_ref[...] = v.astype(BF16)
    ri = lax.broadcasted_iota(jnp.int32, (SGU_CHUNK, SGU_CHUNK), 0)
    ci = lax.broadcasted_iota(jnp.int32, (SGU_CHUNK, SGU_CHUNK), 1)
    tril = ri >= ci
    for gi in range(SGU_GROUPS):
        cs = slice(gi * gw, (gi + 1) * gw)
        ws = jnp.where(tril, ws_ref[gi], 0.0).astype(BF16)
        bias = bs_ref[:, gi:gi + 1]
        for n in range(tm // SGU_CHUNK):
            rs = slice(n * SGU_CHUNK, (n + 1) * SGU_CHUNK)
            mixed = _dot(ws, v_ref[rs, cs]) + bias
            gated_ref[rs, cs] = (u_ref[rs, cs] * mixed).astype(BF16)
    o_ref[...] = x_ref[...] + _dot(gated_ref[...], wo_ref[...])


def _sgu(x2, g, w_uv, ln_g, ln_b, w_s, b_s, w_out, tm):
    N = x2.shape[0]
    row = lambda i: (i, 0)
    fixed = lambda i: (0, 0)
    return pl.pallas_call(
        functools.partial(_sgu_kernel, tm=tm),
        grid=(N // tm,),
        in_specs=[
            pl.BlockSpec((tm, D_MODEL), row),
            pl.BlockSpec((1, D_MODEL), fixed),
            pl.BlockSpec((D_MODEL, 2 * D_MODEL), fixed),
            pl.BlockSpec((1, D_MODEL), fixed), pl.BlockSpec((1, D_MODEL), fixed),
            pl.BlockSpec((SGU_GROUPS, SGU_CHUNK, SGU_CHUNK), lambda i: (0, 0, 0)),
            pl.BlockSpec((SGU_CHUNK, SGU_GROUPS), fixed),
            pl.BlockSpec((D_MODEL, D_MODEL), fixed),
        ],
        out_specs=pl.BlockSpec((tm, D_MODEL), row),
        out_shape=jax.ShapeDtypeStruct((N, D_MODEL), F32),
        scratch_shapes=[pltpu.VMEM((tm, D_MODEL), F32), pltpu.VMEM((tm, D_MODEL), BF16),
                        pltpu.VMEM((tm, D_MODEL), BF16)],
        compiler_params=_params("parallel"),
        name="sgu",
    )(x2, g.reshape(1, -1), w_uv.astype(BF16), ln_g.reshape(1, -1), ln_b.reshape(1, -1),
      w_s, b_s.T, w_out.astype(BF16))


def kernel(x, attn_norm, attn_w_in, gla_w_a2, gla_b_a, gla_head_g, attn_w_o, sgu_norm, sgu_w_uv, sgu_ln_g,
           sgu_ln_b, sgu_w_s, sgu_b_s, sgu_w_out, ffn_norm, ffn_w_up, ffn_conv_w, ffn_conv_b, ffn_w_down,
           final_norm):
    B, T, D = x.shape
    assert D == D_MODEL and T % WIDE_TILE == 0 and T % TOKEN_TILE == 0 and T % DSA_K_TILE == 0
    topk = min(TOPK_MAX, T // 4)
    depth = ffn_norm.shape[0]
    x2 = x.reshape(B * T, D)
    w_up_bf, w_down_bf = ffn_w_up.astype(BF16), ffn_w_down.astype(BF16)
    for i in range(depth):
        j = i // 2
        if i % 2 == 0:
            aq, ak, av, ar, la, bq, iq, kk, v1t, misct = _in_proj(
                x2, attn_norm[j], attn_w_in[j], gla_w_a2[j], gla_b_a[j], T, WIDE_TILE)
            oa = _gla(aq, ak, av, la, ar, gla_head_g[j], B, T, WIDE_TILE)
            ob = _dsa(bq, iq, misct, kk, v1t, B, T, DSA_Q_TILE, DSA_K_TILE, topk)
            attn = (oa, ob, attn_w_o[j])
        else:
            attn = None
            x2 = _sgu(x2, sgu_norm[j], sgu_w_uv[j], sgu_ln_g[j], sgu_ln_b[j], sgu_w_s[j], sgu_b_s[j],
                      sgu_w_out[j], WIDE_TILE)
        x2 = _conv_ffn(x2, ffn_norm[i], w_up_bf, ffn_conv_w[i], ffn_conv_b[i], w_down_bf,
                       final_norm, i, T, TOKEN_TILE, i == depth - 1, attn)
    return x2.reshape(B, T, D)
```
